```python
import math, functools
import jax, jax.numpy as jnp
from jax import lax
import numpy as np

D_MODEL = 2048
BATCH = 16
SEQ = 2048
DEPTH = 1
DEC_BATCH = 128
DEC_SEQ = 8
PAST_LEN = 16384
PAGE_SIZE = 128

MIX_WIDTH = D_MODEL
ATTN_WIDTH = MIX_WIDTH // 2
SGU_WIDTH = MIX_WIDTH - ATTN_WIDTH
HEAD_DIM = 64
N_HEADS = ATTN_WIDTH // HEAD_DIM
N_KV = max(1, N_HEADS // 8)
Q_PER_KV = N_HEADS // N_KV
WINDOW = 128
ROPE_THETA = 10000.0
NEG_INF = -1e30
SGU_CHUNK = 128
SGU_GROUP = 128
SGU_HEADS = SGU_WIDTH // SGU_GROUP
Q_COLS = N_HEADS * HEAD_DIM
KV_COLS = N_KV * HEAD_DIM
IN_COLS = Q_COLS + 2 * KV_COLS + 2 * SGU_WIDTH
SPLITS = (Q_COLS, Q_COLS + KV_COLS, Q_COLS + 2 * KV_COLS, Q_COLS + 2 * KV_COLS + SGU_WIDTH)
PEER_HEADS = 8
PEER_NKEYS = 128
PEER_N = PEER_NKEYS * PEER_NKEYS
PEER_KEY_DIM = 256
PEER_HALF = PEER_KEY_DIM // 2
PEER_TOPK = 16
PEER_BLOCK = 128
PEER_V_SCALE = 0.5
PLE_DIM = 256
EPS = 1e-6

kernel_name = 'swa_sink_gqa_sgu_peer_hybrid_step'


def rmsnorm(x, g):
    x32 = x.astype(jnp.float32)
    y = x32 * lax.rsqrt(jnp.mean(x32 * x32, axis=-1, keepdims=True) + EPS)
    return (y * g.astype(jnp.float32)).astype(x.dtype)


def layernorm(x, g, b):
    x32 = x.astype(jnp.float32)
    xc = x32 - jnp.mean(x32, axis=-1, keepdims=True)
    y = xc * lax.rsqrt(jnp.mean(xc * xc, axis=-1, keepdims=True) + EPS)
    return (y * g.astype(jnp.float32) + b.astype(jnp.float32)).astype(x.dtype)


def rope(x, pos):
    half = HEAD_DIM // 2
    inv_freq = 1.0 / (ROPE_THETA ** (jnp.arange(0, HEAD_DIM, 2, dtype=jnp.float32) / HEAD_DIM))
    ang = pos.astype(jnp.float32)[:, None] * inv_freq[None, :]
    cos = jnp.cos(ang)[:, None, :]
    sin = jnp.sin(ang)[:, None, :]
    x32 = x.astype(jnp.float32)
    x1, x2 = x32[..., :half], x32[..., half:]
    return jnp.concatenate([x1 * cos - x2 * sin, x2 * cos + x1 * sin], axis=-1).astype(x.dtype)


def sink_attention(q, k, v, q_pos, k_pos, sinks):
    qg = q.reshape(q.shape[:-2] + (N_KV, Q_PER_KV, HEAD_DIM))
    s = jnp.einsum('...qkgd,...skd->...kgqs', qg, k).astype(jnp.float32) * (HEAD_DIM ** -0.5)
    dist = q_pos[..., :, None] - k_pos[..., None, :]
    valid = (dist >= 0) & (dist < WINDOW) & (k_pos[..., None, :] >= 0)
    s = jnp.where(valid[..., None, None, :, :], s, NEG_INF)
    sink = jnp.broadcast_to(sinks.astype(jnp.float32).reshape(N_KV, Q_PER_KV, 1, 1), s.shape[:-1] + (1,))
    prob = jax.nn.softmax(jnp.concatenate([s, sink], axis=-1), axis=-1)[..., :-1]
    o = jnp.einsum('...kgqs,...skd->...qkgd', prob.astype(v.dtype), v)
    return o.reshape(q.shape[:-2] + (ATTN_WIDTH,))


def attn_prompt(q, k, v, sinks):
    b, s_len = q.shape[0], q.shape[1]
    nb = s_len // WINDOW
    qb = q.reshape(b, nb, WINDOW, N_HEADS, HEAD_DIM)
    kb = k.reshape(b, nb, WINDOW, N_KV, HEAD_DIM)
    vb = v.reshape(b, nb, WINDOW, N_KV, HEAD_DIM)
    k_band = jnp.concatenate([jnp.concatenate([jnp.zeros_like(kb[:, :1]), kb[:, :-1]], axis=1), kb], axis=2)
    v_band = jnp.concatenate([jnp.concatenate([jnp.zeros_like(vb[:, :1]), vb[:, :-1]], axis=1), vb], axis=2)
    starts = jnp.arange(nb, dtype=jnp.int32) * WINDOW
    q_pos = starts[:, None] + jnp.arange(WINDOW, dtype=jnp.int32)[None, :]
    k_pos = starts[:, None] - WINDOW + jnp.arange(2 * WINDOW, dtype=jnp.int32)[None, :]
    o = sink_attention(qb, k_band, v_band, q_pos, k_pos, sinks).reshape(b, s_len, ATTN_WIDTH)
    wc = min(WINDOW, s_len)
    return o, k[:, s_len - wc:], v[:, s_len - wc:]


def attn_sample(ck, cv, q, k, v, sinks):
    t_len = q.shape[1]
    w = ck.shape[1]
    k_all = jnp.concatenate([ck.astype(k.dtype), k], axis=1)
    v_all = jnp.concatenate([cv.astype(v.dtype), v], axis=1)
    k_pos = PAST_LEN - w + jnp.arange(w + t_len, dtype=jnp.int32)
    q_pos = PAST_LEN + jnp.arange(t_len, dtype=jnp.int32)
    o = sink_attention(q, k_all, v_all, q_pos, k_pos, sinks)
    return o, k_all[:, t_len:], v_all[:, t_len:]


def sgu_mix(vb, w_s, b_s):
    l_len = vb.shape[-3]
    causal = jnp.tril(jnp.ones((SGU_CHUNK, SGU_CHUNK), jnp.float32))
    ws = (w_s.astype(jnp.float32) * causal)[:, :l_len, :l_len].astype(vb.dtype)
    bias = jnp.transpose(b_s[:, :l_len]).astype(vb.dtype)[:, :, None]
    return jnp.einsum('gts,...sgc->...tgc', ws, vb) + bias


def sgu_prompt(u, vn, w_s, b_s):
    b, s_len = u.shape[0], u.shape[1]
    vb = vn.reshape(b, s_len // SGU_CHUNK, SGU_CHUNK, SGU_HEADS, SGU_GROUP)
    return u * sgu_mix(vb, w_s, b_s).reshape(b, s_len, SGU_WIDTH), None


def sgu_sample(u, vn, w_s, b_s):
    b, t_len = u.shape[0], u.shape[1]
    vb = vn.reshape(b, t_len, SGU_HEADS, SGU_GROUP)
    return u * sgu_mix(vb, w_s, b_s).reshape(b, t_len, SGU_WIDTH), vb


def peer(xn, w_pq, sub_k1, sub_k2, u_tab, v_tab):
    lead = xn.shape[:-1]
    xf = xn.reshape(-1, D_MODEL)
    n = xf.shape[0]
    q = (xf @ w_pq).reshape(n, PEER_HEADS, 2, PEER_HALF)
    s1 = jnp.einsum('nhd,hkd->nhk', q[:, :, 0], sub_k1).astype(jnp.float32)
    s2 = jnp.einsum('nhd,hkd->nhk', q[:, :, 1], sub_k2).astype(jnp.float32)
    t1, i1 = lax.top_k(s1, PEER_TOPK)
    t2, i2 = lax.top_k(s2, PEER_TOPK)
    cand = (t1[..., :, None] + t2[..., None, :]).reshape(n, PEER_HEADS, PEER_TOPK * PEER_TOPK)
    sc, ci = lax.top_k(cand, PEER_TOPK)
    e1 = jnp.take_along_axis(i1, ci // PEER_TOPK, axis=-1)
    e2 = jnp.take_along_axis(i2, ci % PEER_TOPK, axis=-1)
    idx = e1 * PEER_NKEYS + e2
    gate = jax.nn.softmax(sc, axis=-1)
    pad = (-n) % PEER_BLOCK
    xp = jnp.pad(xf, ((0, pad), (0, 0)))
    ip = jnp.pad(idx, ((0, pad), (0, 0), (0, 0)))
    gp = jnp.pad(gate, ((0, pad), (0, 0), (0, 0)))
    nb = (n + pad) // PEER_BLOCK

    def block(args):
        xb, ib, gb = args
        a = jnp.einsum('thkd,td->thk', u_tab[ib], xb)
        hk = (jax.nn.gelu(a.astype(jnp.float32), approximate=False) * gb).astype(xb.dtype)
        return jnp.einsum('thk,thkd->td', hk, v_tab[ib])

    out = lax.map(block, (xp.reshape(nb, PEER_BLOCK, D_MODEL),
                          ip.reshape(nb, PEER_BLOCK, PEER_HEADS, PEER_TOPK),
                          gp.reshape(nb, PEER_BLOCK, PEER_HEADS, PEER_TOPK)))
    return out.reshape(-1, D_MODEL)[:n].reshape(lead + (D_MODEL,))


def trunk_layer(x, p_i, pos, attn_fn, sgu_fn, g_mix, w_in, sinks, g_sgu, b_sgu, w_s, b_s,
                g_attn_out, g_sgu_out, w_out, g_ffn, w_pq, sub_k1, sub_k2, u_tab, v_tab,
                g_gate, w_gate, w_ple, g_ple):
    lead = x.shape[:-1]
    xn = rmsnorm(x, g_mix)
    z = xn @ w_in
    zq, zk, zv, zu, zsv = jnp.split(z, SPLITS, axis=-1)
    q = rope(zq.reshape(lead + (N_HEADS, HEAD_DIM)), pos)
    k = rope(zk.reshape(lead + (N_KV, HEAD_DIM)), pos)
    v = zv.reshape(lead + (N_KV, HEAD_DIM))
    a_out, k_state, v_state = attn_fn(q, k, v, sinks)
    u = jax.nn.gelu(zu, approximate=False)
    vn = layernorm(jax.nn.gelu(zsv, approximate=False), g_sgu, b_sgu)
    s_out, s_state = sgu_fn(u, vn, w_s, b_s)
    mixed = jnp.concatenate([rmsnorm(a_out, g_attn_out), rmsnorm(s_out, g_sgu_out)], axis=-1) @ w_out
    h = x + mixed
    h = h + peer(rmsnorm(h, g_ffn), w_pq, sub_k1, sub_k2, u_tab, v_tab)
    gate = jax.nn.sigmoid((rmsnorm(h, g_gate) @ w_gate).astype(jnp.float32)).astype(h.dtype)
    h = h + rmsnorm(p_i.astype(h.dtype) @ w_ple, g_ple) * gate
    return h, k_state, v_state, s_state


def setup_inputs(seed: int = 0) -> dict:
    key = jax.random.key(seed)
    ks = jax.random.split(key, 32)
    wc = min(WINDOW, PAST_LEN)

    def nrm(k, shape, scale):
        return jax.random.normal(k, shape, jnp.float32) * scale

    def gain(k, shape):
        return 1.0 + 0.02 * jax.random.normal(k, shape, jnp.float32)

    return {
        'x_prompt': nrm(ks[0], (BATCH, SEQ, D_MODEL), 1.0),
        'x_sample': nrm(ks[1], (DEC_BATCH, DEC_SEQ, D_MODEL), 1.0),
        'cache_k': nrm(ks[2], (DEPTH, DEC_BATCH, wc, N_KV, HEAD_DIM), 1.0),
        'cache_v': nrm(ks[3], (DEPTH, DEC_BATCH, wc, N_KV, HEAD_DIM), 1.0),
        'p_prompt': nrm(ks[4], (DEPTH, BATCH, SEQ, PLE_DIM), 1.0),
        'p_sample': nrm(ks[5], (DEPTH, DEC_BATCH, DEC_SEQ, PLE_DIM), 1.0),
        'g_mix': gain(ks[6], (DEPTH, D_MODEL)),
        'w_in': nrm(ks[7], (DEPTH, D_MODEL, IN_COLS), D_MODEL ** -0.5),
        'sinks': nrm(ks[8], (DEPTH, N_HEADS), 0.5),
        'g_sgu': gain(ks[9], (DEPTH, SGU_WIDTH)),
        'b_sgu': nrm(ks[10], (DEPTH, SGU_WIDTH), 0.02),
        'w_s': nrm(ks[11], (DEPTH, SGU_HEADS, SGU_CHUNK, SGU_CHUNK), SGU_CHUNK ** -0.5),
        'b_s': 1.0 + nrm(ks[12], (DEPTH, SGU_HEADS, SGU_CHUNK), 0.02),
        'g_attn_out': gain(ks[13], (DEPTH, ATTN_WIDTH)),
        'g_sgu_out': gain(ks[14], (DEPTH, SGU_WIDTH)),
        'w_out': nrm(ks[15], (DEPTH, MIX_WIDTH, D_MODEL), MIX_WIDTH ** -0.5),
        'g_ffn': gain(ks[16], (DEPTH, D_MODEL)),
        'w_pq': nrm(ks[17], (DEPTH, D_MODEL, PEER_HEADS * PEER_KEY_DIM), D_MODEL ** -0.5),
        'sub_k1': nrm(ks[18], (DEPTH, PEER_HEADS, PEER_NKEYS, PEER_HALF), PEER_HALF ** -0.5),
        'sub_k2': nrm(ks[19], (DEPTH, PEER_HEADS, PEER_NKEYS, PEER_HALF), PEER_HALF ** -0.5),
        'u_tab': nrm(ks[20], (DEPTH, PEER_N, D_MODEL), D_MODEL ** -0.5),
        'v_tab': nrm(ks[21], (DEPTH, PEER_N, D_MODEL), PEER_V_SCALE),
        'g_gate': gain(ks[22], (DEPTH, D_MODEL)),
        'w_gate': nrm(ks[23], (DEPTH, D_MODEL, D_MODEL), D_MODEL ** -0.5),
        'w_ple': nrm(ks[24], (DEPTH, PLE_DIM, D_MODEL), PLE_DIM ** -0.5),
        'g_ple': gain(ks[25], (DEPTH, D_MODEL)),
        'g_final': gain(ks[26], (D_MODEL,)),
    }


def reference(x_prompt, x_sample, cache_k, cache_v, p_prompt, p_sample, g_mix, w_in, sinks,
              g_sgu, b_sgu, w_s, b_s, g_attn_out, g_sgu_out, w_out, g_ffn, w_pq, sub_k1, sub_k2,
              u_tab, v_tab, g_gate, w_gate, w_ple, g_ple, g_final):
    pos_prompt = jnp.arange(x_prompt.shape[1], dtype=jnp.int32)
    pos_sample = PAST_LEN + jnp.arange(x_sample.shape[1], dtype=jnp.int32)
    hp, hs = x_prompt, x_sample
    kp_l, vp_l, ks_l, vs_l, gs_l = [], [], [], [], []
    for i in range(DEPTH):
        lw = (g_mix[i], w_in[i], sinks[i], g_sgu[i], b_sgu[i], w_s[i], b_s[i], g_attn_out[i],
              g_sgu_out[i], w_out[i], g_ffn[i], w_pq[i], sub_k1[i], sub_k2[i], u_tab[i], v_tab[i],
              g_gate[i], w_gate[i], w_ple[i], g_ple[i])
        hp, kp, vp, _ = trunk_layer(hp, p_prompt[i], pos_prompt, attn_prompt, sgu_prompt, *lw)
        attn_s = functools.partial(attn_sample, cache_k[i], cache_v[i])
        hs, ks_, vs_, gs_ = trunk_layer(hs, p_sample[i], pos_sample, attn_s, sgu_sample, *lw)
        kp_l.append(kp)
        vp_l.append(vp)
        ks_l.append(ks_)
        vs_l.append(vs_)
        gs_l.append(gs_)
    y_prompt = rmsnorm(hp, g_final)
    y_sample = rmsnorm(hs, g_final)
    new_k_prompt = jnp.stack(kp_l, axis=0)
    new_v_prompt = jnp.stack(vp_l, axis=0)
    new_k_sample = jnp.stack(ks_l, axis=0)
    new_v_sample = jnp.stack(vs_l, axis=0)
    new_sgu_v_sample = jnp.stack(gs_l, axis=0)
    return (y_prompt, y_sample, new_k_prompt, new_v_prompt, new_k_sample, new_v_sample, new_sgu_v_sample)
```

```python
import functools

import jax
import jax.numpy as jnp
from jax import lax
from jax.experimental import pallas as pl
from jax.experimental.pallas import tpu as pltpu

F32 = jnp.float32
BF16 = jnp.bfloat16

D_MODEL = 2048
PAST_LEN = 16384
HEAD_DIM = 64
N_HEADS = 16
N_KV = 2
Q_PER_KV = N_HEADS // N_KV
WINDOW = 128
ROPE_THETA = 10000.0
NEG_INF = -1e30
ATTN_WIDTH = N_HEADS * HEAD_DIM
SGU_CHUNK = 128
SGU_GROUP = 128
SGU_HEADS = 8
SGU_WIDTH = SGU_HEADS * SGU_GROUP
Q_COLS = N_HEADS * HEAD_DIM
KV_COLS = N_KV * HEAD_DIM
IN_COLS = Q_COLS + 2 * KV_COLS + 2 * SGU_WIDTH
PEER_HEADS = 8
PEER_NKEYS = 128
PEER_HALF = 128
PEER_TOPK = 16
PEER_PICKS = PEER_HEADS * PEER_TOPK
PLE_DIM = 256
EPS = 1e-6

LANES = 128
MIB = 1024 * 1024


def _rms(x, g):
    return x * lax.rsqrt(jnp.mean(x * x, axis=-1, keepdims=True) + EPS) * g


def _gelu(x):
    return 0.5 * x * (1.0 + lax.erf(x * (0.5 ** 0.5)))


def _const_spec(shape):
    nd = len(shape)
    return pl.BlockSpec(shape, lambda *_: (0,) * nd, pipeline_mode=pl.Buffered(1))


def _params(vmem_mib, n_grid=1):
    return pltpu.CompilerParams(
        dimension_semantics=("arbitrary",) * n_grid, vmem_limit_bytes=vmem_mib * MIB)


def _in_proj_kernel(x_ref, g_ref, w_ref, cos_ref, sin_ref, gs_ref, bs_ref,
                    q_ref, k_ref, v_ref, u_ref, vn_ref):
    xn = _rms(x_ref[...], g_ref[...]).astype(BF16)
    z = jnp.dot(xn, w_ref[...], preferred_element_type=F32)
    cos = cos_ref[...]
    sin = sin_ref[...]
    lane = lax.broadcasted_iota(jnp.int32, cos.shape, 1)
    first_half = (lane % HEAD_DIM) < (HEAD_DIM // 2)

    def rope(zc):
        swapped = jnp.where(first_half,
                            pltpu.roll(zc, LANES - HEAD_DIM // 2, 1),
                            pltpu.roll(zc, HEAD_DIM // 2, 1))
        return zc * cos + swapped * sin

    for c in range(Q_COLS // LANES):
        q_ref[:, c * LANES:(c + 1) * LANES] = rope(z[:, c * LANES:(c + 1) * LANES])
    k_ref[...] = rope(z[:, Q_COLS:Q_COLS + KV_COLS])
    v_ref[...] = z[:, Q_COLS + KV_COLS:Q_COLS + 2 * KV_COLS]
    u0 = Q_COLS + 2 * KV_COLS
    u_ref[...] = _gelu(z[:, u0:u0 + SGU_WIDTH])
    sv = _gelu(z[:, u0 + SGU_WIDTH:u0 + 2 * SGU_WIDTH])
    mu = jnp.mean(sv, axis=-1, keepdims=True)
    svc = sv - mu
    var = jnp.mean(svc * svc, axis=-1, keepdims=True)
    vn_ref[...] = svc * lax.rsqrt(var + EPS) * gs_ref[...] + bs_ref[...]


def _in_proj(x, g_mix, w_in_bf, cos_t, sin_t, g_sgu, b_sgu, tb):
    n = x.shape[0]
    period_blocks = cos_t.shape[0] // tb
    tok = lambda w: pl.BlockSpec((tb, w), lambda i: (i, 0))
    tab = pl.BlockSpec((tb, LANES), lambda i: (i % period_blocks, 0))
    return pl.pallas_call(
        _in_proj_kernel,
        grid=(n // tb,),
        in_specs=[tok(D_MODEL), _const_spec((1, D_MODEL)), _const_spec((D_MODEL, IN_COLS)),
                  tab, tab, _const_spec((1, SGU_WIDTH)), _const_spec((1, SGU_WIDTH))],
        out_specs=[tok(Q_COLS), tok(KV_COLS), tok(KV_COLS), tok(SGU_WIDTH), tok(SGU_WIDTH)],
        out_shape=[jax.ShapeDtypeStruct((n, w), F32)
                   for w in (Q_COLS, KV_COLS, KV_COLS, SGU_WIDTH, SGU_WIDTH)],
        compiler_params=_params(48),
        name="in_proj",
    )(x, g_mix, w_in_bf, cos_t, sin_t, g_sgu, b_sgu)


def _attn_kernel(sinks_ref, q_ref, ka_ref, kb_ref, va_ref, vb_ref, g_ref, o_ref, acc_ref,
                 *, blocks_per_seq, first_block_has_no_past):
    q = q_ref[...]
    kk = jnp.concatenate([ka_ref[...], kb_ref[...]], axis=0).astype(BF16)
    vv = jnp.concatenate([va_ref[...], vb_ref[...]], axis=0).astype(BF16)
    r = lax.broadcasted_iota(jnp.int32, (WINDOW, 2 * WINDOW), 0)
    c = lax.broadcasted_iota(jnp.int32, (WINDOW, 2 * WINDOW), 1)
    lo = r
    if first_block_has_no_past:
        j = pl.program_id(0) % blocks_per_seq
        lo = jnp.where(j == 0, jnp.maximum(r, WINDOW - 1), r)
    valid = (c > lo) & (c <= r + WINDOW)
    for g in range(N_KV):
        kg = kk[:, g * HEAD_DIM:(g + 1) * HEAD_DIM]
        vg = vv[:, g * HEAD_DIM:(g + 1) * HEAD_DIM]
        for hh in range(Q_PER_KV):
            h = g * Q_PER_KV + hh
            qh = q[:, h * HEAD_DIM:(h + 1) * HEAD_DIM].astype(BF16)
            s = lax.dot_general(qh, kg, (((1,), (1,)), ((), ())),
                                preferred_element_type=F32) * (HEAD_DIM ** -0.5)
            s = jnp.where(valid, s, NEG_INF)
            sink = sinks_ref[h]
            m = jnp.maximum(jnp.max(s, axis=1, keepdims=True), sink)
            p = jnp.exp(s - m)
            denom = jnp.sum(p, axis=1, keepdims=True) + jnp.exp(sink - m)
            p = (p / denom).astype(BF16)
            acc_ref[:, h * HEAD_DIM:(h + 1) * HEAD_DIM] = jnp.dot(p, vg, preferred_element_type=F32)
    o_ref[...] = _rms(acc_ref[...], g_ref[...]).astype(o_ref.dtype)


def _attention(sinks, q, k_a, k_b, v_a, v_b, g_attn_out, a_map, b_map, blocks_per_seq,
               first_block_has_no_past):
    n = q.shape[0]
    kv = lambda m: pl.BlockSpec((WINDOW, KV_COLS), m)
    kern = functools.partial(_attn_kernel, blocks_per_seq=blocks_per_seq,
                             first_block_has_no_past=first_block_has_no_past)
    return pl.pallas_call(
        kern,
        grid=(n // WINDOW,),
        in_specs=[pl.BlockSpec(memory_space=pltpu.SMEM),
                  pl.BlockSpec((WINDOW, Q_COLS), lambda i: (i, 0)),
                  kv(a_map), kv(b_map), kv(a_map), kv(b_map),
                  _const_spec((1, ATTN_WIDTH))],
        out_specs=pl.BlockSpec((WINDOW, ATTN_WIDTH), lambda i: (i, 0)),
        out_shape=jax.ShapeDtypeStruct((n, ATTN_WIDTH), BF16),
        scratch_shapes=[pltpu.VMEM((WINDOW, ATTN_WIDTH), F32)],
        compiler_params=_params(32),
        name="attention",
    )(sinks, q, k_a, k_b, v_a, v_b, g_attn_out)


def _sgu_out_kernel(u_ref, vn_ref, wm_ref, bias_ref, g_ref, an_ref, wo_ref, x_ref, h_ref, s_ref):
    tb = u_ref.shape[0]
    for ch in range(tb // SGU_CHUNK):
        rows = slice(ch * SGU_CHUNK, (ch + 1) * SGU_CHUNK)
        for g in range(SGU_HEADS):
            cols = slice(g * SGU_GROUP, (g + 1) * SGU_GROUP)
            mix = jnp.dot(wm_ref[g], vn_ref[rows, cols].astype(BF16),
                          preferred_element_type=F32) + bias_ref[g]
            s_ref[rows, cols] = u_ref[rows, cols] * mix
    sn = _rms(s_ref[...], g_ref[...]).astype(BF16)
    mixed = jnp.dot(an_ref[...], wo_ref[:ATTN_WIDTH, :], preferred_element_type=F32)
    mixed = mixed + jnp.dot(sn, wo_ref[ATTN_WIDTH:, :], preferred_element_type=F32)
    h_ref[...] = x_ref[...] + mixed


def _sgu_out(u, vn, wm_bf, bias, g_sgu_out, an, w_out_bf, x, tb):
    n = x.shape[0]
    tok = lambda w: pl.BlockSpec((tb, w), lambda i: (i, 0))
    return pl.pallas_call(
        _sgu_out_kernel,
        grid=(n // tb,),
        in_specs=[tok(SGU_WIDTH), tok(SGU_WIDTH),
                  _const_spec((SGU_HEADS, SGU_CHUNK, SGU_CHUNK)),
                  _const_spec((SGU_HEADS, SGU_CHUNK, SGU_GROUP)),
                  _const_spec((1, SGU_WIDTH)), tok(ATTN_WIDTH),
                  _const_spec((D_MODEL, D_MODEL)), tok(D_MODEL)],
        out_specs=tok(D_MODEL),
        out_shape=jax.ShapeDtypeStruct((n, D_MODEL), F32),
        scratch_shapes=[pltpu.VMEM((tb, SGU_WIDTH), F32)],
        compiler_params=_params(48),
        name="sgu_out",
    )(u, vn, wm_bf, bias, g_sgu_out, an, w_out_bf, x)


def _top16_rows(s, n_rows):
    row = lax.broadcasted_iota(jnp.int32, s.shape, 0)
    vals, idxs = [], []
    for _ in range(PEER_TOPK):
        m = jnp.max(s, axis=0, keepdims=True)
        i = jnp.min(jnp.where(s == m, row, n_rows), axis=0, keepdims=True)
        vals.append(m)
        idxs.append(i)
        s = jnp.where(row == i, -jnp.inf, s)
    return jnp.concatenate(vals, axis=0), jnp.concatenate(idxs, axis=0)


def _pick_rows(table, sel):
    out = jnp.zeros(sel.shape, table.dtype)
    for i in range(PEER_TOPK):
        out = jnp.where(sel == i, table[i:i + 1, :], out)
    return out


def _peer_topk_kernel(h_ref, g_ref, wq_ref, k1_ref, k2_ref, xn_ref, idx_ref, gate_ref):
    xn = _rms(h_ref[...], g_ref[...])
    xn_ref[...] = xn
    qp = jnp.dot(xn.astype(BF16), wq_ref[...], preferred_element_type=F32).astype(BF16)
    idx_parts, gate_parts = [], []
    nt = (((1,), (1,)), ((), ()))
    for h in range(PEER_HEADS):
        base = h * 2 * PEER_HALF
        s1 = lax.dot_general(k1_ref[h], qp[:, base:base + PEER_HALF], nt,
                             preferred_element_type=F32)
        s2 = lax.dot_general(k2_ref[h], qp[:, base + PEER_HALF:base + 2 * PEER_HALF], nt,
                             preferred_element_type=F32)
        t1, i1 = _top16_rows(s1, PEER_NKEYS)
        t2, i2 = _top16_rows(s2, PEER_NKEYS)
        cand = jnp.concatenate([t1[i:i + 1, :] + t2 for i in range(PEER_TOPK)], axis=0)
        sc, ci = _top16_rows(cand, PEER_TOPK * PEER_TOPK)
        e1 = _pick_rows(i1, ci // PEER_TOPK)
        e2 = _pick_rows(i2, ci % PEER_TOPK)
        idx_parts.append(e1 * PEER_NKEYS + e2)
        ex = jnp.exp(sc - sc[0:1, :])
        gate_parts.append(ex / jnp.sum(ex, axis=0, keepdims=True))
    idx_t = jnp.concatenate(idx_parts, axis=0)
    gate_t = jnp.concatenate(gate_parts, axis=0)
    tb = h_ref.shape[0]
    for c in range(tb // LANES):
        idx_ref[c * LANES:(c + 1) * LANES, :] = idx_t[:, c * LANES:(c + 1) * LANES].T
        gate_ref[c * LANES:(c + 1) * LANES, :] = gate_t[:, c * LANES:(c + 1) * LANES].T


def _peer_topk(h, g_ffn, w_pq_bf, k1_bf, k2_bf, tb):
    n = h.shape[0]
    tok = lambda w: pl.BlockSpec((tb, w), lambda i: (i, 0))
    return pl.pallas_call(
        _peer_topk_kernel,
        grid=(n // tb,),
        in_specs=[tok(D_MODEL), _const_spec((1, D_MODEL)),
                  _const_spec((D_MODEL, PEER_HEADS * 2 * PEER_HALF)),
                  _const_spec((PEER_HEADS, PEER_NKEYS, PEER_HALF)),
                  _const_spec((PEER_HEADS, PEER_NKEYS, PEER_HALF))],
        out_specs=[tok(D_MODEL), tok(PEER_PICKS), tok(PEER_PICKS)],
        out_shape=[jax.ShapeDtypeStruct((n, D_MODEL), F32),
                   jax.ShapeDtypeStruct((n, PEER_PICKS), jnp.int32),
                   jax.ShapeDtypeStruct((n, PEER_PICKS), F32)],
        compiler_params=_params(48),
        name="peer_topk",
    )(h, g_ffn, w_pq_bf, k1_bf, k2_bf)


PEER_RING = 3
PEER_LHS_ROWS = 16


def _peer_expert_kernel(idx_ref, xn_ref, gate_ref, h_ref, u_hbm, v_hbm, o_ref, ubuf, vbuf, sem,
                        *, tokens):
    def row_copy(tab, buf, which, slot, e, r):
        return pltpu.make_async_copy(tab.at[pl.ds(r, 1), :], buf.at[slot, pl.ds(e, 1), :],
                                     sem.at[which, slot])

    def issue(t, slot):
        for e in range(PEER_PICKS):
            r = idx_ref[t, e]
            row_copy(u_hbm, ubuf, 0, slot, e, r).start()
            row_copy(v_hbm, vbuf, 1, slot, e, r).start()

    def wait(slot):
        pltpu.make_async_copy(u_hbm.at[pl.ds(0, PEER_PICKS), :], ubuf.at[slot], sem.at[0, slot]).wait()
        pltpu.make_async_copy(v_hbm.at[pl.ds(0, PEER_PICKS), :], vbuf.at[slot], sem.at[1, slot]).wait()

    for s in range(PEER_RING - 1):
        issue(s, s)

    def body(t, carry):
        slot = t % PEER_RING
        nxt = t + PEER_RING - 1

        @pl.when(nxt < tokens)
        def _():
            issue(nxt, nxt % PEER_RING)

        wait(slot)
        xb = jnp.broadcast_to(xn_ref[pl.ds(t, 1), :], (PEER_LHS_ROWS, D_MODEL)).astype(BF16)
        a = lax.dot_general(xb, ubuf[slot].astype(BF16), (((1,), (1,)), ((), ())),
                            preferred_element_type=F32)
        hk = (_gelu(a) * gate_ref[pl.ds(t, 1), :]).astype(BF16)
        o = jnp.dot(hk, vbuf[slot].astype(BF16), preferred_element_type=F32)
        o_ref[pl.ds(t, 1), :] = h_ref[pl.ds(t, 1), :] + o[0:1, :]
        return carry

    lax.fori_loop(0, tokens, body, 0)


def _peer_expert(idx, xn, gate, h, u_tab, v_tab, tokens):
    n = h.shape[0]
    tok = lambda w: pl.BlockSpec((tokens, w), lambda i: (i, 0))
    kern = functools.partial(_peer_expert_kernel, tokens=tokens)
    return pl.pallas_call(
        kern,
        grid=(n // tokens,),
        in_specs=[pl.BlockSpec((tokens, PEER_PICKS), lambda i: (i, 0), memory_space=pltpu.SMEM),
                  tok(D_MODEL), tok(PEER_PICKS), tok(D_MODEL),
                  pl.BlockSpec(memory_space=pl.ANY), pl.BlockSpec(memory_space=pl.ANY)],
        out_specs=tok(D_MODEL),
        out_shape=jax.ShapeDtypeStruct((n, D_MODEL), F32),
        scratch_shapes=[pltpu.VMEM((PEER_RING, PEER_PICKS, D_MODEL), F32),
                        pltpu.VMEM((PEER_RING, PEER_PICKS, D_MODEL), F32),
                        pltpu.SemaphoreType.DMA((2, PEER_RING))],
        compiler_params=_params(40),
        name="peer_expert",
    )(idx, xn, gate, h, u_tab, v_tab)


def _final_kernel(h_ref, p_ref, gg_ref, wg_ref, wp_ref, gp_ref, gf_ref, y_ref):
    h = h_ref[...]
    gate = jax.nn.sigmoid(jnp.dot(_rms(h, gg_ref[...]).astype(BF16), wg_ref[...],
                                  preferred_element_type=F32))
    ple = jnp.dot(p_ref[...].astype(BF16), wp_ref[...], preferred_element_type=F32)
    h = h + _rms(ple, gp_ref[...]) * gate
    y_ref[...] = _rms(h, gf_ref[...])


def _final(h, p, g_gate, w_gate_bf, w_ple_bf, g_ple, g_final, tb):
    n = h.shape[0]
    tok = lambda w: pl.BlockSpec((tb, w), lambda i: (i, 0))
    return pl.pallas_call(
        _final_kernel,
        grid=(n // tb,),
        in_specs=[tok(D_MODEL), tok(PLE_DIM), _const_spec((1, D_MODEL)),
                  _const_spec((D_MODEL, D_MODEL)), _const_spec((PLE_DIM, D_MODEL)),
                  _const_spec((1, D_MODEL)), _const_spec((1, D_MODEL))],
        out_specs=tok(D_MODEL),
        out_shape=jax.ShapeDtypeStruct((n, D_MODEL), F32),
        compiler_params=_params(48),
        name="final",
    )(h, p, g_gate, w_gate_bf, w_ple_bf, g_ple, g_final)


def _rope_tables(pos):
    half = HEAD_DIM // 2
    inv_freq = 1.0 / (ROPE_THETA ** (jnp.arange(0, HEAD_DIM, 2, dtype=F32) / HEAD_DIM))
    ang = pos.astype(F32)[:, None] * inv_freq[None, :]
    cos, sin = jnp.cos(ang), jnp.sin(ang)
    reps = LANES // HEAD_DIM
    return (jnp.tile(jnp.concatenate([cos, cos], axis=1), (1, reps)),
            jnp.tile(jnp.concatenate([-sin, sin], axis=1), (1, reps)))


def _row(v):
    return v.reshape(1, -1).astype(F32)


def _token_stages(x, p, cos_t, sin_t, attn_fn, wm_bf, bias, lw, tb):
    q, k, v, u, vn = _in_proj(x, lw["g_mix"], lw["w_in"], cos_t, sin_t, lw["g_sgu"], lw["b_sgu"], tb)
    an = attn_fn(q, k, v)
    h = _sgu_out(u, vn, wm_bf, bias, lw["g_sgu_out"], an, lw["w_out"], x, tb)
    xn, idx, gate = _peer_topk(h, lw["g_ffn"], lw["w_pq"], lw["sub_k1"], lw["sub_k2"], tb)
    h = _peer_expert(idx, xn, gate, h, lw["u_tab"], lw["v_tab"], min(128, x.shape[0]))
    return h, k, v, vn


def kernel(x_prompt, x_sample, cache_k, cache_v, p_prompt, p_sample, g_mix, w_in, sinks, g_sgu, b_sgu,
           w_s, b_s, g_attn_out, g_sgu_out, w_out, g_ffn, w_pq, sub_k1, sub_k2, u_tab, v_tab, g_gate,
           w_gate, w_ple, g_ple, g_final):
    depth = w_in.shape[0]
    bsz, seq, _ = x_prompt.shape
    dbsz, dseq, _ = x_sample.shape
    wc = cache_k.shape[2]
    assert depth == 1, "the final kernel fuses the layer tail with the final norm"
    assert seq % WINDOW == 0 and wc == WINDOW and WINDOW % dseq == 0 and dseq <= SGU_CHUNK
    n_p, n_s = bsz * seq, dbsz * dseq
    tb_p = 256 if seq % 256 == 0 else WINDOW
    tb_s = WINDOW
    assert n_s % tb_s == 0 and seq % tb_p == 0
    nb = seq // WINDOW

    cos_p, sin_p = _rope_tables(jnp.arange(seq, dtype=jnp.int32))
    cos_s, sin_s = _rope_tables(PAST_LEN + (jnp.arange(tb_s, dtype=jnp.int32) % dseq))
    causal = jnp.tril(jnp.ones((SGU_CHUNK, SGU_CHUNK), F32))
    eye = jnp.eye(SGU_CHUNK // dseq, dtype=F32)

    hp = x_prompt.reshape(n_p, D_MODEL)
    hs = x_sample.reshape(n_s, D_MODEL)
    kp_l, vp_l, ks_l, vs_l, gs_l = [], [], [], [], []
    for i in range(depth):
        lw = dict(
            g_mix=_row(g_mix[i]), w_in=w_in[i].astype(BF16), g_sgu=_row(g_sgu[i]), b_sgu=_row(b_sgu[i]),
            g_sgu_out=_row(g_sgu_out[i]), w_out=w_out[i].astype(BF16), g_ffn=_row(g_ffn[i]),
            w_pq=w_pq[i].astype(BF16), sub_k1=sub_k1[i].astype(BF16), sub_k2=sub_k2[i].astype(BF16),
            u_tab=u_tab[i], v_tab=v_tab[i])
        sinks_i = sinks[i].astype(F32)
        g_ao = _row(g_attn_out[i])
        ws = w_s[i].astype(F32) * causal
        wm_p = ws.astype(BF16)
        bias_p = jnp.broadcast_to(b_s[i].astype(F32)[:, :, None], (SGU_HEADS, SGU_CHUNK, SGU_GROUP))
        wm_s = jnp.einsum("ab,gts->gatbs", eye, ws[:, :dseq, :dseq]).reshape(
            SGU_HEADS, SGU_CHUNK, SGU_CHUNK).astype(BF16)
        bias_s = jnp.broadcast_to(jnp.tile(b_s[i].astype(F32)[:, :dseq], (1, SGU_CHUNK // dseq))[:, :, None],
                                  (SGU_HEADS, SGU_CHUNK, SGU_GROUP))

        def attn_prompt(q, k, v):
            prev = lambda j: (jnp.where(j % nb == 0, j, j - 1), 0)
            cur = lambda j: (j, 0)
            return _attention(sinks_i, q, k, k, v, v, g_ao, prev, cur, nb, True)

        def attn_sample(q, k, v, ck=cache_k[i], cv=cache_v[i]):
            pad = lambda a: jnp.pad(a.reshape(dbsz, dseq, -1), ((0, 0), (0, WINDOW - dseq), (0, 0))
                                    ).reshape(dbsz * WINDOW, -1)
            ident = lambda j: (j, 0)
            an = _attention(sinks_i, pad(q), ck.reshape(dbsz * wc, KV_COLS), pad(k),
                            cv.reshape(dbsz * wc, KV_COLS), pad(v), g_ao, ident, ident, 1, False)
            return an.reshape(dbsz, WINDOW, ATTN_WIDTH)[:, :dseq].reshape(n_s, ATTN_WIDTH)

        hp, kp, vp, _ = _token_stages(hp, None, cos_p, sin_p, attn_prompt, wm_p, bias_p, lw, tb_p)
        hs, ks, vs, vns = _token_stages(hs, None, cos_s, sin_s, attn_sample, wm_s, bias_s, lw, tb_s)
        fin = (_row(g_gate[i]), w_gate[i].astype(BF16), w_ple[i].astype(BF16), _row(g_ple[i]),
               _row(g_final))
        hp = _final(hp, p_prompt[i].reshape(n_p, PLE_DIM), *fin, tb_p)
        hs = _final(hs, p_sample[i].reshape(n_s, PLE_DIM), *fin, tb_s)

        kp_l.append(kp.reshape(bsz, seq, N_KV, HEAD_DIM)[:, seq - WINDOW:])
        vp_l.append(vp.reshape(bsz, seq, N_KV, HEAD_DIM)[:, seq - WINDOW:])
        ks_l.append(jnp.concatenate([cache_k[i][:, dseq:], ks.reshape(dbsz, dseq, N_KV, HEAD_DIM)], axis=1))
        vs_l.append(jnp.concatenate([cache_v[i][:, dseq:], vs.reshape(dbsz, dseq, N_KV, HEAD_DIM)], axis=1))
        gs_l.append(vns.reshape(dbsz, dseq, SGU_HEADS, SGU_GROUP))

    return (hp.reshape(bsz, seq, D_MODEL), hs.reshape(dbsz, dseq, D_MODEL),
            jnp.stack(kp_l), jnp.stack(vp_l), jnp.stack(ks_l), jnp.stack(vs_l), jnp.stack(gs_l))
```

```python
import functools

import jax
import jax.numpy as jnp
from jax import lax
from jax.experimental import pallas as pl
from jax.experimental.pallas import tpu as pltpu

F32 = jnp.float32
BF16 = jnp.bfloat16

D_MODEL = 2048
PAST_LEN = 16384
HEAD_DIM = 64
N_HEADS = 16
N_KV = 2
Q_PER_KV = N_HEADS // N_KV
WINDOW = 128
ROPE_THETA = 10000.0
NEG_INF = -1e30
ATTN_WIDTH = N_HEADS * HEAD_DIM
SGU_CHUNK = 128
SGU_GROUP = 128
SGU_HEADS = 8
SGU_WIDTH = SGU_HEADS * SGU_GROUP
Q_COLS = N_HEADS * HEAD_DIM
KV_COLS = N_KV * HEAD_DIM
IN_COLS = Q_COLS + 2 * KV_COLS + 2 * SGU_WIDTH
PEER_HEADS = 8
PEER_NKEYS = 128
PEER_HALF = 128
PEER_TOPK = 16
PEER_PICKS = PEER_HEADS * PEER_TOPK
PLE_DIM = 256
EPS = 1e-6

LANES = 128
MIB = 1024 * 1024


def _rms(x, g):
    return x * lax.rsqrt(jnp.mean(x * x, axis=-1, keepdims=True) + EPS) * g


def _gelu(x):
    return 0.5 * x * (1.0 + lax.erf(x * (0.5 ** 0.5)))


def _const_spec(shape):
    nd = len(shape)
    return pl.BlockSpec(shape, lambda *_: (0,) * nd, pipeline_mode=pl.Buffered(1))


def _params(vmem_mib, n_grid=1):
    return pltpu.CompilerParams(
        dimension_semantics=("arbitrary",) * n_grid, vmem_limit_bytes=vmem_mib * MIB)


def _in_proj_kernel(x_ref, g_ref, w_ref, cos_ref, sin_ref, gs_ref, bs_ref,
                    q_ref, k_ref, v_ref, u_ref, vn_ref):
    xn = _rms(x_ref[...], g_ref[...]).astype(BF16)
    z = jnp.dot(xn, w_ref[...], preferred_element_type=F32)
    cos = cos_ref[...]
    sin = sin_ref[...]
    lane = lax.broadcasted_iota(jnp.int32, cos.shape, 1)
    first_half = (lane % HEAD_DIM) < (HEAD_DIM // 2)

    def rope(zc):
        swapped = jnp.where(first_half,
                            pltpu.roll(zc, LANES - HEAD_DIM // 2, 1),
                            pltpu.roll(zc, HEAD_DIM // 2, 1))
        return zc * cos + swapped * sin

    for c in range(Q_COLS // LANES):
        q_ref[:, c * LANES:(c + 1) * LANES] = rope(z[:, c * LANES:(c + 1) * LANES])
    k_ref[...] = rope(z[:, Q_COLS:Q_COLS + KV_COLS])
    v_ref[...] = z[:, Q_COLS + KV_COLS:Q_COLS + 2 * KV_COLS]
    u0 = Q_COLS + 2 * KV_COLS
    u_ref[...] = _gelu(z[:, u0:u0 + SGU_WIDTH])
    sv = _gelu(z[:, u0 + SGU_WIDTH:u0 + 2 * SGU_WIDTH])
    mu = jnp.mean(sv, axis=-1, keepdims=True)
    svc = sv - mu
    var = jnp.mean(svc * svc, axis=-1, keepdims=True)
    vn_ref[...] = svc * lax.rsqrt(var + EPS) * gs_ref[...] + bs_ref[...]


def _in_proj(x, g_mix, w_in_bf, cos_t, sin_t, g_sgu, b_sgu, tb):
    n = x.shape[0]
    period_blocks = cos_t.shape[0] // tb
    tok = lambda w: pl.BlockSpec((tb, w), lambda i: (i, 0))
    tab = pl.BlockSpec((tb, LANES), lambda i: (i % period_blocks, 0))
    return pl.pallas_call(
        _in_proj_kernel,
        grid=(n // tb,),
        in_specs=[tok(D_MODEL), _const_spec((1, D_MODEL)), _const_spec((D_MODEL, IN_COLS)),
                  tab, tab, _const_spec((1, SGU_WIDTH)), _const_spec((1, SGU_WIDTH))],
        out_specs=[tok(Q_COLS), tok(KV_COLS), tok(KV_COLS), tok(SGU_WIDTH), tok(SGU_WIDTH)],
        out_shape=[jax.ShapeDtypeStruct((n, w), F32)
                   for w in (Q_COLS, KV_COLS, KV_COLS, SGU_WIDTH, SGU_WIDTH)],
        compiler_params=_params(48),
        name="in_proj",
    )(x, g_mix, w_in_bf, cos_t, sin_t, g_sgu, b_sgu)


def _attn_kernel(sinks_ref, q_ref, ka_ref, kb_ref, va_ref, vb_ref, g_ref, o_ref, acc_ref,
                 *, blocks_per_seq, first_block_has_no_past):
    q = q_ref[...]
    kk = jnp.concatenate([ka_ref[...], kb_ref[...]], axis=0).astype(BF16)
    vv = jnp.concatenate([va_ref[...], vb_ref[...]], axis=0).astype(BF16)
    r = lax.broadcasted_iota(jnp.int32, (WINDOW, 2 * WINDOW), 0)
    c = lax.broadcasted_iota(jnp.int32, (WINDOW, 2 * WINDOW), 1)
    lo = r
    if first_block_has_no_past:
        j = pl.program_id(0) % blocks_per_seq
        lo = jnp.where(j == 0, jnp.maximum(r, WINDOW - 1), r)
    valid = (c > lo) & (c <= r + WINDOW)
    for g in range(N_KV):
        kg = kk[:, g * HEAD_DIM:(g + 1) * HEAD_DIM]
        vg = vv[:, g * HEAD_DIM:(g + 1) * HEAD_DIM]
        for hh in range(Q_PER_KV):
            h = g * Q_PER_KV + hh
            qh = q[:, h * HEAD_DIM:(h + 1) * HEAD_DIM].astype(BF16)
            s = lax.dot_general(qh, kg, (((1,), (1,)), ((), ())),
                                preferred_element_type=F32) * (HEAD_DIM ** -0.5)
            s = jnp.where(valid, s, NEG_INF)
            sink = sinks_ref[h]
            m = jnp.maximum(jnp.max(s, axis=1, keepdims=True), sink)
            p = jnp.exp(s - m)
            denom = jnp.sum(p, axis=1, keepdims=True) + jnp.exp(sink - m)
            p = (p / denom).astype(BF16)
            acc_ref[:, h * HEAD_DIM:(h + 1) * HEAD_DIM] = jnp.dot(p, vg, preferred_element_type=F32)
    o_ref[...] = _rms(acc_ref[...], g_ref[...]).astype(o_ref.dtype)


def _attention(sinks, q, k_a, k_b, v_a, v_b, g_attn_out, a_map, b_map, blocks_per_seq,
               first_block_has_no_past):
    n = q.shape[0]
    kv = lambda m: pl.BlockSpec((WINDOW, KV_COLS), m)
    kern = functools.partial(_attn_kernel, blocks_per_seq=blocks_per_seq,
                             first_block_has_no_past=first_block_has_no_past)
    return pl.pallas_call(
        kern,
        grid=(n // WINDOW,),
        in_specs=[pl.BlockSpec(memory_space=pltpu.SMEM),
                  pl.BlockSpec((WINDOW, Q_COLS), lambda i: (i, 0)),
                  kv(a_map), kv(b_map), kv(a_map), kv(b_map),
                  _const_spec((1, ATTN_WIDTH))],
        out_specs=pl.BlockSpec((WINDOW, ATTN_WIDTH), lambda i: (i, 0)),
        out_shape=jax.ShapeDtypeStruct((n, ATTN_WIDTH), BF16),
        scratch_shapes=[pltpu.VMEM((WINDOW, ATTN_WIDTH), F32)],
        compiler_params=_params(32),
        name="attention",
    )(sinks, q, k_a, k_b, v_a, v_b, g_attn_out)


def _sgu_out_kernel(u_ref, vn_ref, wm_ref, bias_ref, g_ref, an_ref, wo_ref, x_ref, h_ref, s_ref):
    tb = u_ref.shape[0]
    for ch in range(tb // SGU_CHUNK):
        rows = slice(ch * SGU_CHUNK, (ch + 1) * SGU_CHUNK)
        for g in range(SGU_HEADS):
            cols = slice(g * SGU_GROUP, (g + 1) * SGU_GROUP)
            mix = jnp.dot(wm_ref[g], vn_ref[rows, cols].astype(BF16),
                          preferred_element_type=F32) + bias_ref[g]
            s_ref[rows, cols] = u_ref[rows, cols] * mix
    sn = _rms(s_ref[...], g_ref[...]).astype(BF16)
    mixed = jnp.dot(an_ref[...], wo_ref[:ATTN_WIDTH, :], preferred_element_type=F32)
    mixed = mixed + jnp.dot(sn, wo_ref[ATTN_WIDTH:, :], preferred_element_type=F32)
    h_ref[...] = x_ref[...] + mixed


def _sgu_out(u, vn, wm_bf, bias, g_sgu_out, an, w_out_bf, x, tb):
    n = x.shape[0]
    tok = lambda w: pl.BlockSpec((tb, w), lambda i: (i, 0))
    return pl.pallas_call(
        _sgu_out_kernel,
        grid=(n // tb,),
        in_specs=[tok(SGU_WIDTH), tok(SGU_WIDTH),
                  _const_spec((SGU_HEADS, SGU_CHUNK, SGU_CHUNK)),
                  _const_spec((SGU_HEADS, SGU_CHUNK, SGU_GROUP)),
                  _const_spec((1, SGU_WIDTH)), tok(ATTN_WIDTH),
                  _const_spec((D_MODEL, D_MODEL)), tok(D_MODEL)],
        out_specs=tok(D_MODEL),
        out_shape=jax.ShapeDtypeStruct((n, D_MODEL), F32),
        scratch_shapes=[pltpu.VMEM((tb, SGU_WIDTH), F32)],
        compiler_params=_params(48),
        name="sgu_out",
    )(u, vn, wm_bf, bias, g_sgu_out, an, w_out_bf, x)


def _top16_rows(s, n_rows):
    row = lax.broadcasted_iota(jnp.int32, s.shape, 0)
    vals, idxs = [], []
    for _ in range(PEER_TOPK):
        m = jnp.max(s, axis=0, keepdims=True)
        i = jnp.min(jnp.where(s == m, row, n_rows), axis=0, keepdims=True)
        vals.append(m)
        idxs.append(i)
        s = jnp.where(row == i, -jnp.inf, s)
    return jnp.concatenate(vals, axis=0), jnp.concatenate(idxs, axis=0)


def _pick_rows(table, sel):
    out = jnp.zeros(sel.shape, table.dtype)
    for i in range(PEER_TOPK):
        out = jnp.where(sel == i, table[i:i + 1, :], out)
    return out


def _peer_topk_kernel(h_ref, g_ref, wq_ref, k1_ref, k2_ref, xn_ref, idx_ref, gate_ref):
    xn = _rms(h_ref[...], g_ref[...])
    xn_ref[...] = xn
    qp = jnp.dot(xn.astype(BF16), wq_ref[...], preferred_element_type=F32).astype(BF16)
    idx_parts, gate_parts = [], []
    nt = (((1,), (1,)), ((), ()))
    for h in range(PEER_HEADS):
        base = h * 2 * PEER_HALF
        s1 = lax.dot_general(k1_ref[h], qp[:, base:base + PEER_HALF], nt,
                             preferred_element_type=F32)
        s2 = lax.dot_general(k2_ref[h], qp[:, base + PEER_HALF:base + 2 * PEER_HALF], nt,
                             preferred_element_type=F32)
        t1, i1 = _top16_rows(s1, PEER_NKEYS)
        t2, i2 = _top16_rows(s2, PEER_NKEYS)
        cand = jnp.concatenate([t1[i:i + 1, :] + t2 for i in range(PEER_TOPK)], axis=0)
        sc, ci = _top16_rows(cand, PEER_TOPK * PEER_TOPK)
        e1 = _pick_rows(i1, ci // PEER_TOPK)
        e2 = _pick_rows(i2, ci % PEER_TOPK)
        idx_parts.append(e1 * PEER_NKEYS + e2)
        ex = jnp.exp(sc - sc[0:1, :])
        gate_parts.append(ex / jnp.sum(ex, axis=0, keepdims=True))
    idx_t = jnp.concatenate(idx_parts, axis=0)
    gate_t = jnp.concatenate(gate_parts, axis=0)
    tb = h_ref.shape[0]
    for c in range(tb // LANES):
        idx_ref[c * LANES:(c + 1) * LANES, :] = idx_t[:, c * LANES:(c + 1) * LANES].T
        gate_ref[c * LANES:(c + 1) * LANES, :] = gate_t[:, c * LANES:(c + 1) * LANES].T


def _peer_topk(h, g_ffn, w_pq_bf, k1_bf, k2_bf, tb):
    n = h.shape[0]
    tok = lambda w: pl.BlockSpec((tb, w), lambda i: (i, 0))
    return pl.pallas_call(
        _peer_topk_kernel,
        grid=(n // tb,),
        in_specs=[tok(D_MODEL), _const_spec((1, D_MODEL)),
                  _const_spec((D_MODEL, PEER_HEADS * 2 * PEER_HALF)),
                  _const_spec((PEER_HEADS, PEER_NKEYS, PEER_HALF)),
                  _const_spec((PEER_HEADS, PEER_NKEYS, PEER_HALF))],
        out_specs=[tok(D_MODEL), tok(PEER_PICKS), tok(PEER_PICKS)],
        out_shape=[jax.ShapeDtypeStruct((n, D_MODEL), F32),
                   jax.ShapeDtypeStruct((n, PEER_PICKS), jnp.int32),
                   jax.ShapeDtypeStruct((n, PEER_PICKS), F32)],
        compiler_params=_params(48),
        name="peer_topk",
    )(h, g_ffn, w_pq_bf, k1_bf, k2_bf)


SUBLANES = 8
PEER_RING = 8
PEER_CHUNKS = D_MODEL // LANES
PEER_EGROUPS = PEER_PICKS // SUBLANES
U_MASK = 0xFFFF0000


def _pack_uv(u_tab, v_tab):
    hi = lax.bitcast_convert_type(u_tab.astype(BF16), jnp.uint16).astype(jnp.uint32) << 16
    lo = lax.bitcast_convert_type(v_tab.astype(BF16), jnp.uint16).astype(jnp.uint32)
    return (hi | lo).reshape(u_tab.shape[0], PEER_CHUNKS, LANES)


def _peer_expert_kernel(idx_ref, xn_ref, gate_ref, h_ref, tab_hbm, o_ref, *scratch, tokens):
    bufs, (stage, hk_scr, sem) = scratch[:PEER_RING], scratch[PEER_RING:]

    def row_copy(slot, e, r):
        return pltpu.make_async_copy(tab_hbm.at[r], bufs[slot].at[e // SUBLANES, :, e % SUBLANES, :],
                                     sem.at[slot])

    def issue(t, slot):
        for e in range(PEER_PICKS):
            row_copy(slot, e, idx_ref[t, e]).start(priority=e % 2)

    def wait(slot):
        for e in range(PEER_PICKS):
            row_copy(slot, e, 0).wait()

    lane_minus_sub = (lax.broadcasted_iota(jnp.int32, (SUBLANES, LANES), 1)
                      - lax.broadcasted_iota(jnp.int32, (SUBLANES, LANES), 0))

    def u_pass(t, slot):
        row = pl.ds(t, 1)
        x_row = xn_ref[row, :]
        xb = [jnp.broadcast_to(x_row[:, c * LANES:(c + 1) * LANES], (SUBLANES, LANES))
              for c in range(PEER_CHUNKS)]
        a_diag = jnp.zeros((SUBLANES, LANES), F32)
        for g in range(PEER_EGROUPS):
            acc = None
            for c in range(PEER_CHUNKS):
                u = pltpu.bitcast(bufs[slot][g, c] & jnp.uint32(U_MASK), F32)
                acc = u * xb[c] if acc is None else acc + u * xb[c]
            a_g = jnp.sum(acc, axis=1, keepdims=True)
            a_diag = jnp.where(lane_minus_sub == g * SUBLANES, a_g, a_diag)
        a_row = jnp.sum(a_diag, axis=0, keepdims=True)
        hk_scr[slot:slot + 1, :] = _gelu(a_row) * gate_ref[row, :]

    def v_pass(t, slot):
        row = pl.ds(t, 1)
        hk_row = jnp.broadcast_to(hk_scr[slot:slot + 1, :], (SUBLANES, LANES))
        hk_g = [jnp.broadcast_to(
                    jnp.sum(jnp.where(lane_minus_sub == g * SUBLANES, hk_row, 0.0), axis=1, keepdims=True),
                    (SUBLANES, LANES))
                for g in range(PEER_EGROUPS)]
        for c in range(PEER_CHUNKS):
            acc = None
            for g in range(PEER_EGROUPS):
                v = pltpu.bitcast(bufs[slot][g, c] << 16, F32)
                acc = v * hk_g[g] if acc is None else acc + v * hk_g[g]
            stage[slot:slot + 1, c * LANES:(c + 1) * LANES] = jnp.sum(acc, axis=0, keepdims=True)
        o_ref[row, :] = h_ref[row, :] + stage[slot:slot + 1, :]

    def phase(k, s, do_v, do_issue):
        wait(s)
        if do_issue:
            issue(k - 2 + PEER_RING, (s - 2) % PEER_RING)
        u_pass(k, s)
        if do_v:
            v_pass(k - 1, (s - 1) % PEER_RING)

    def static_group(g):
        for s in range(PEER_RING):
            k = g * PEER_RING + s
            phase(k, s, k >= 1, k >= 2 and k - 2 + PEER_RING < tokens)

    def group(g, carry):
        for s in range(PEER_RING):
            phase(g * PEER_RING + s, s, True, True)
        return carry

    n_groups = tokens // PEER_RING
    for s in range(PEER_RING):
        issue(s, s)
    static_group(0)
    if n_groups > 1:
        lax.fori_loop(1, n_groups - 1, group, 0)
        static_group(n_groups - 1)
    v_pass(tokens - 1, (tokens - 1) % PEER_RING)


def _peer_expert(idx, xn, gate, h, tab, tokens):
    n = h.shape[0]
    assert tokens % PEER_RING == 0 and n % tokens == 0
    tok = lambda w: pl.BlockSpec((tokens, w), lambda i: (i, 0))
    kern = functools.partial(_peer_expert_kernel, tokens=tokens)
    return pl.pallas_call(
        kern,
        grid=(n // tokens,),
        in_specs=[pl.BlockSpec((tokens, PEER_PICKS), lambda i: (i, 0), memory_space=pltpu.SMEM),
                  tok(D_MODEL), tok(PEER_PICKS), tok(D_MODEL),
                  pl.BlockSpec(memory_space=pl.ANY)],
        out_specs=tok(D_MODEL),
        out_shape=jax.ShapeDtypeStruct((n, D_MODEL), F32),
        scratch_shapes=[pltpu.VMEM((PEER_EGROUPS, PEER_CHUNKS, SUBLANES, LANES), jnp.uint32)
                        for _ in range(PEER_RING)] + [
                        pltpu.VMEM((PEER_RING, D_MODEL), F32),
                        pltpu.VMEM((PEER_RING, PEER_PICKS), F32),
                        pltpu.SemaphoreType.DMA((PEER_RING,))],
        compiler_params=_params(40),
        name="peer_expert",
    )(idx, xn, gate, h, tab)


def _final_kernel(h_ref, p_ref, gg_ref, wg_ref, wp_ref, gp_ref, gf_ref, y_ref):
    h = h_ref[...]
    gate = jax.nn.sigmoid(jnp.dot(_rms(h, gg_ref[...]).astype(BF16), wg_ref[...],
                                  preferred_element_type=F32))
    ple = jnp.dot(p_ref[...].astype(BF16), wp_ref[...], preferred_element_type=F32)
    h = h + _rms(ple, gp_ref[...]) * gate
    y_ref[...] = _rms(h, gf_ref[...])


def _final(h, p, g_gate, w_gate_bf, w_ple_bf, g_ple, g_final, tb):
    n = h.shape[0]
    tok = lambda w: pl.BlockSpec((tb, w), lambda i: (i, 0))
    return pl.pallas_call(
        _final_kernel,
        grid=(n // tb,),
        in_specs=[tok(D_MODEL), tok(PLE_DIM), _const_spec((1, D_MODEL)),
                  _const_spec((D_MODEL, D_MODEL)), _const_spec((PLE_DIM, D_MODEL)),
                  _const_spec((1, D_MODEL)), _const_spec((1, D_MODEL))],
        out_specs=tok(D_MODEL),
        out_shape=jax.ShapeDtypeStruct((n, D_MODEL), F32),
        compiler_params=_params(48),
        name="final",
    )(h, p, g_gate, w_gate_bf, w_ple_bf, g_ple, g_final)


def _rope_tables(pos):
    half = HEAD_DIM // 2
    inv_freq = 1.0 / (ROPE_THETA ** (jnp.arange(0, HEAD_DIM, 2, dtype=F32) / HEAD_DIM))
    ang = pos.astype(F32)[:, None] * inv_freq[None, :]
    cos, sin = jnp.cos(ang), jnp.sin(ang)
    reps = LANES // HEAD_DIM
    return (jnp.tile(jnp.concatenate([cos, cos], axis=1), (1, reps)),
            jnp.tile(jnp.concatenate([-sin, sin], axis=1), (1, reps)))


def _row(v):
    return v.reshape(1, -1).astype(F32)


def _token_stages(x, p, cos_t, sin_t, attn_fn, wm_bf, bias, lw, tb):
    q, k, v, u, vn = _in_proj(x, lw["g_mix"], lw["w_in"], cos_t, sin_t, lw["g_sgu"], lw["b_sgu"], tb)
    an = attn_fn(q, k, v)
    h = _sgu_out(u, vn, wm_bf, bias, lw["g_sgu_out"], an, lw["w_out"], x, tb)
    xn, idx, gate = _peer_topk(h, lw["g_ffn"], lw["w_pq"], lw["sub_k1"], lw["sub_k2"], tb)
    h = _peer_expert(idx, xn, gate, h, lw["uv_tab"], min(256, x.shape[0]))
    return h, k, v, vn


def kernel(x_prompt, x_sample, cache_k, cache_v, p_prompt, p_sample, g_mix, w_in, sinks, g_sgu, b_sgu,
           w_s, b_s, g_attn_out, g_sgu_out, w_out, g_ffn, w_pq, sub_k1, sub_k2, u_tab, v_tab, g_gate,
           w_gate, w_ple, g_ple, g_final):
    depth = w_in.shape[0]
    bsz, seq, _ = x_prompt.shape
    dbsz, dseq, _ = x_sample.shape
    wc = cache_k.shape[2]
    assert depth == 1, "the final kernel fuses the layer tail with the final norm"
    assert seq % WINDOW == 0 and wc == WINDOW and WINDOW % dseq == 0 and dseq <= SGU_CHUNK
    n_p, n_s = bsz * seq, dbsz * dseq
    tb_p = 256 if seq % 256 == 0 else WINDOW
    tb_s = WINDOW
    assert n_s % tb_s == 0 and seq % tb_p == 0
    nb = seq // WINDOW

    cos_p, sin_p = _rope_tables(jnp.arange(seq, dtype=jnp.int32))
    cos_s, sin_s = _rope_tables(PAST_LEN + (jnp.arange(tb_s, dtype=jnp.int32) % dseq))
    causal = jnp.tril(jnp.ones((SGU_CHUNK, SGU_CHUNK), F32))
    eye = jnp.eye(SGU_CHUNK // dseq, dtype=F32)

    hp = x_prompt.reshape(n_p, D_MODEL)
    hs = x_sample.reshape(n_s, D_MODEL)
    kp_l, vp_l, ks_l, vs_l, gs_l = [], [], [], [], []
    for i in range(depth):
        lw = dict(
            g_mix=_row(g_mix[i]), w_in=w_in[i].astype(BF16), g_sgu=_row(g_sgu[i]), b_sgu=_row(b_sgu[i]),
            g_sgu_out=_row(g_sgu_out[i]), w_out=w_out[i].astype(BF16), g_ffn=_row(g_ffn[i]),
            w_pq=w_pq[i].astype(BF16), sub_k1=sub_k1[i].astype(BF16), sub_k2=sub_k2[i].astype(BF16),
            uv_tab=_pack_uv(u_tab[i], v_tab[i]))
        sinks_i = sinks[i].astype(F32)
        g_ao = _row(g_attn_out[i])
        ws = w_s[i].astype(F32) * causal
        wm_p = ws.astype(BF16)
        bias_p = jnp.broadcast_to(b_s[i].astype(F32)[:, :, None], (SGU_HEADS, SGU_CHUNK, SGU_GROUP))
        wm_s = jnp.einsum("ab,gts->gatbs", eye, ws[:, :dseq, :dseq]).reshape(
            SGU_HEADS, SGU_CHUNK, SGU_CHUNK).astype(BF16)
        bias_s = jnp.broadcast_to(jnp.tile(b_s[i].astype(F32)[:, :dseq], (1, SGU_CHUNK // dseq))[:, :, None],
                                  (SGU_HEADS, SGU_CHUNK, SGU_GROUP))

        def attn_prompt(q, k, v):
            prev = lambda j: (jnp.where(j % nb == 0, j, j - 1), 0)
            cur = lambda j: (j, 0)
            return _attention(sinks_i, q, k, k, v, v, g_ao, prev, cur, nb, True)

        def attn_sample(q, k, v, ck=cache_k[i], cv=cache_v[i]):
            pad = lambda a: jnp.pad(a.reshape(dbsz, dseq, -1), ((0, 0), (0, WINDOW - dseq), (0, 0))
                                    ).reshape(dbsz * WINDOW, -1)
            ident = lambda j: (j, 0)
            an = _attention(sinks_i, pad(q), ck.reshape(dbsz * wc, KV_COLS), pad(k),
                            cv.reshape(dbsz * wc, KV_COLS), pad(v), g_ao, ident, ident, 1, False)
            return an.reshape(dbsz, WINDOW, ATTN_WIDTH)[:, :dseq].reshape(n_s, ATTN_WIDTH)

        hp, kp, vp, _ = _token_stages(hp, None, cos_p, sin_p, attn_prompt, wm_p, bias_p, lw, tb_p)
        hs, ks, vs, vns = _token_stages(hs, None, cos_s, sin_s, attn_sample, wm_s, bias_s, lw, tb_s)
        fin = (_row(g_gate[i]), w_gate[i].astype(BF16), w_ple[i].astype(BF16), _row(g_ple[i]),
               _row(g_final))
        hp = _final(hp, p_prompt[i].reshape(n_p, PLE_DIM), *fin, tb_p)
        hs = _final(hs, p_sample[i].reshape(n_s, PLE_DIM), *fin, tb_s)

        kp_l.append(kp.reshape(bsz, seq, N_KV, HEAD_DIM)[:, seq - WINDOW:])
        vp_l.append(vp.reshape(bsz, seq, N_KV, HEAD_DIM)[:, seq - WINDOW:])
        ks_l.append(jnp.concatenate([cache_k[i][:, dseq:], ks.reshape(dbsz, dseq, N_KV, HEAD_DIM)], axis=1))
        vs_l.append(jnp.concatenate([cache_v[i][:, dseq:], vs.reshape(dbsz, dseq, N_KV, HEAD_DIM)], axis=1))
        gs_l.append(vns.reshape(dbsz, dseq, SGU_HEADS, SGU_GROUP))

    return (hp.reshape(bsz, seq, D_MODEL), hs.reshape(dbsz, dseq, D_MODEL),
            jnp.stack(kp_l), jnp.stack(vp_l), jnp.stack(ks_l), jnp.stack(vs_l), jnp.stack(gs_l))
```

```python
import functools

import jax
import jax.numpy as jnp
from jax import lax
from jax.experimental import pallas as pl
from jax.experimental.pallas import tpu as pltpu

F32 = jnp.float32
BF16 = jnp.bfloat16

D_MODEL = 2048
PAST_LEN = 16384
HEAD_DIM = 64
N_HEADS = 16
N_KV = 2
Q_PER_KV = N_HEADS // N_KV
WINDOW = 128
ROPE_THETA = 10000.0
NEG_INF = -1e30
ATTN_WIDTH = N_HEADS * HEAD_DIM
SGU_CHUNK = 128
SGU_GROUP = 128
SGU_HEADS = 8
SGU_WIDTH = SGU_HEADS * SGU_GROUP
Q_COLS = N_HEADS * HEAD_DIM
KV_COLS = N_KV * HEAD_DIM
IN_COLS = Q_COLS + 2 * KV_COLS + 2 * SGU_WIDTH
PEER_HEADS = 8
PEER_NKEYS = 128
PEER_HALF = 128
PEER_TOPK = 16
PEER_PICKS = PEER_HEADS * PEER_TOPK
PLE_DIM = 256
EPS = 1e-6

LANES = 128
MIB = 1024 * 1024


def _rms(x, g):
    return x * lax.rsqrt(jnp.mean(x * x, axis=-1, keepdims=True) + EPS) * g


def _gelu(x):
    return 0.5 * x * (1.0 + lax.erf(x * (0.5 ** 0.5)))


def _const_spec(shape):
    nd = len(shape)
    return pl.BlockSpec(shape, lambda *_: (0,) * nd, pipeline_mode=pl.Buffered(1))


def _params(vmem_mib, n_grid=1):
    return pltpu.CompilerParams(
        dimension_semantics=("arbitrary",) * n_grid, vmem_limit_bytes=vmem_mib * MIB)


def _in_proj_kernel(x_ref, g_ref, w_ref, cos_ref, sin_ref, gs_ref, bs_ref,
                    q_ref, k_ref, v_ref, u_ref, vn_ref):
    xn = _rms(x_ref[...], g_ref[...]).astype(BF16)
    z = jnp.dot(xn, w_ref[...], preferred_element_type=F32)
    cos = cos_ref[...]
    sin = sin_ref[...]
    lane = lax.broadcasted_iota(jnp.int32, cos.shape, 1)
    first_half = (lane % HEAD_DIM) < (HEAD_DIM // 2)

    def rope(zc):
        swapped = jnp.where(first_half,
                            pltpu.roll(zc, LANES - HEAD_DIM // 2, 1),
                            pltpu.roll(zc, HEAD_DIM // 2, 1))
        return zc * cos + swapped * sin

    for c in range(Q_COLS // LANES):
        q_ref[:, c * LANES:(c + 1) * LANES] = rope(z[:, c * LANES:(c + 1) * LANES])
    k_ref[...] = rope(z[:, Q_COLS:Q_COLS + KV_COLS])
    v_ref[...] = z[:, Q_COLS + KV_COLS:Q_COLS + 2 * KV_COLS]
    u0 = Q_COLS + 2 * KV_COLS
    u_ref[...] = _gelu(z[:, u0:u0 + SGU_WIDTH])
    sv = _gelu(z[:, u0 + SGU_WIDTH:u0 + 2 * SGU_WIDTH])
    mu = jnp.mean(sv, axis=-1, keepdims=True)
    svc = sv - mu
    var = jnp.mean(svc * svc, axis=-1, keepdims=True)
    vn_ref[...] = svc * lax.rsqrt(var + EPS) * gs_ref[...] + bs_ref[...]


def _in_proj(x, g_mix, w_in_bf, cos_t, sin_t, g_sgu, b_sgu, tb):
    n = x.shape[0]
    period_blocks = cos_t.shape[0] // tb
    tok = lambda w: pl.BlockSpec((tb, w), lambda i: (i, 0))
    tab = pl.BlockSpec((tb, LANES), lambda i: (i % period_blocks, 0))
    return pl.pallas_call(
        _in_proj_kernel,
        grid=(n // tb,),
        in_specs=[tok(D_MODEL), _const_spec((1, D_MODEL)), _const_spec((D_MODEL, IN_COLS)),
                  tab, tab, _const_spec((1, SGU_WIDTH)), _const_spec((1, SGU_WIDTH))],
        out_specs=[tok(Q_COLS), tok(KV_COLS), tok(KV_COLS), tok(SGU_WIDTH), tok(SGU_WIDTH)],
        out_shape=[jax.ShapeDtypeStruct((n, w), F32)
                   for w in (Q_COLS, KV_COLS, KV_COLS, SGU_WIDTH, SGU_WIDTH)],
        compiler_params=_params(48),
        name="in_proj",
    )(x, g_mix, w_in_bf, cos_t, sin_t, g_sgu, b_sgu)


def _attn_kernel(sinks_ref, q_ref, ka_ref, kb_ref, va_ref, vb_ref, g_ref, o_ref, acc_ref,
                 *, blocks_per_seq, first_block_has_no_past):
    q = q_ref[...]
    kk = jnp.concatenate([ka_ref[...], kb_ref[...]], axis=0).astype(BF16)
    vv = jnp.concatenate([va_ref[...], vb_ref[...]], axis=0).astype(BF16)
    r = lax.broadcasted_iota(jnp.int32, (WINDOW, 2 * WINDOW), 0)
    c = lax.broadcasted_iota(jnp.int32, (WINDOW, 2 * WINDOW), 1)
    lo = r
    if first_block_has_no_past:
        j = pl.program_id(0) % blocks_per_seq
        lo = jnp.where(j == 0, jnp.maximum(r, WINDOW - 1), r)
    valid = (c > lo) & (c <= r + WINDOW)
    for g in range(N_KV):
        kg = kk[:, g * HEAD_DIM:(g + 1) * HEAD_DIM]
        vg = vv[:, g * HEAD_DIM:(g + 1) * HEAD_DIM]
        for hh in range(Q_PER_KV):
            h = g * Q_PER_KV + hh
            qh = q[:, h * HEAD_DIM:(h + 1) * HEAD_DIM].astype(BF16)
            s = lax.dot_general(qh, kg, (((1,), (1,)), ((), ())),
                                preferred_element_type=F32) * (HEAD_DIM ** -0.5)
            s = jnp.where(valid, s, NEG_INF)
            sink = sinks_ref[h]
            m = jnp.maximum(jnp.max(s, axis=1, keepdims=True), sink)
            p = jnp.exp(s - m)
            denom = jnp.sum(p, axis=1, keepdims=True) + jnp.exp(sink - m)
            p = (p / denom).astype(BF16)
            acc_ref[:, h * HEAD_DIM:(h + 1) * HEAD_DIM] = jnp.dot(p, vg, preferred_element_type=F32)
    o_ref[...] = _rms(acc_ref[...], g_ref[...]).astype(o_ref.dtype)


def _attention(sinks, q, k_a, k_b, v_a, v_b, g_attn_out, a_map, b_map, blocks_per_seq,
               first_block_has_no_past):
    n = q.shape[0]
    kv = lambda m: pl.BlockSpec((WINDOW, KV_COLS), m)
    kern = functools.partial(_attn_kernel, blocks_per_seq=blocks_per_seq,
                             first_block_has_no_past=first_block_has_no_past)
    return pl.pallas_call(
        kern,
        grid=(n // WINDOW,),
        in_specs=[pl.BlockSpec(memory_space=pltpu.SMEM),
                  pl.BlockSpec((WINDOW, Q_COLS), lambda i: (i, 0)),
                  kv(a_map), kv(b_map), kv(a_map), kv(b_map),
                  _const_spec((1, ATTN_WIDTH))],
        out_specs=pl.BlockSpec((WINDOW, ATTN_WIDTH), lambda i: (i, 0)),
        out_shape=jax.ShapeDtypeStruct((n, ATTN_WIDTH), BF16),
        scratch_shapes=[pltpu.VMEM((WINDOW, ATTN_WIDTH), F32)],
        compiler_params=_params(32),
        name="attention",
    )(sinks, q, k_a, k_b, v_a, v_b, g_attn_out)


def _sgu_out_kernel(u_ref, vn_ref, wm_ref, bias_ref, g_ref, an_ref, wo_ref, x_ref, h_ref, s_ref):
    tb = u_ref.shape[0]
    for ch in range(tb // SGU_CHUNK):
        rows = slice(ch * SGU_CHUNK, (ch + 1) * SGU_CHUNK)
        for g in range(SGU_HEADS):
            cols = slice(g * SGU_GROUP, (g + 1) * SGU_GROUP)
            mix = jnp.dot(wm_ref[g], vn_ref[rows, cols].astype(BF16),
                          preferred_element_type=F32) + bias_ref[g]
            s_ref[rows, cols] = u_ref[rows, cols] * mix
    sn = _rms(s_ref[...], g_ref[...]).astype(BF16)
    mixed = jnp.dot(an_ref[...], wo_ref[:ATTN_WIDTH, :], preferred_element_type=F32)
    mixed = mixed + jnp.dot(sn, wo_ref[ATTN_WIDTH:, :], preferred_element_type=F32)
    h_ref[...] = x_ref[...] + mixed


def _sgu_out(u, vn, wm_bf, bias, g_sgu_out, an, w_out_bf, x, tb):
    n = x.shape[0]
    tok = lambda w: pl.BlockSpec((tb, w), lambda i: (i, 0))
    return pl.pallas_call(
        _sgu_out_kernel,
        grid=(n // tb,),
        in_specs=[tok(SGU_WIDTH), tok(SGU_WIDTH),
                  _const_spec((SGU_HEADS, SGU_CHUNK, SGU_CHUNK)),
                  _const_spec((SGU_HEADS, SGU_CHUNK, SGU_GROUP)),
                  _const_spec((1, SGU_WIDTH)), tok(ATTN_WIDTH),
                  _const_spec((D_MODEL, D_MODEL)), tok(D_MODEL)],
        out_specs=tok(D_MODEL),
        out_shape=jax.ShapeDtypeStruct((n, D_MODEL), F32),
        scratch_shapes=[pltpu.VMEM((tb, SGU_WIDTH), F32)],
        compiler_params=_params(48),
        name="sgu_out",
    )(u, vn, wm_bf, bias, g_sgu_out, an, w_out_bf, x)


def _top16_rows(s, payload=None):
    n_rows = s.shape[0]
    row = lax.broadcasted_iota(jnp.int32, s.shape, 0)
    vals, picks = [], []
    for _ in range(PEER_TOPK):
        m = jnp.max(s, axis=0, keepdims=True)
        i = jnp.min(jnp.where(s == m, row, n_rows), axis=0, keepdims=True)
        hit = row == i
        vals.append(m)
        picks.append(i if payload is None else jnp.max(jnp.where(hit, payload, -1), axis=0, keepdims=True))
        s = jnp.where(hit, -jnp.inf, s)
    return jnp.concatenate(vals, axis=0), jnp.concatenate(picks, axis=0)


_CAND_WIDTHS = (16, 8, 8, 8, 4, 4, 4, 4)


def _staircase(a, b, combine):
    rows = [combine(a[i:i + 1, :], b[:w, :]) for i, w in enumerate(_CAND_WIDTHS)]
    rows.append(combine(a[len(_CAND_WIDTHS):, :], b[0:1, :]))
    return jnp.concatenate(rows, axis=0)


def _peer_topk_kernel(h_ref, g_ref, wq_ref, k1_ref, k2_ref, xn_ref, idx_ref, gate_ref):
    xn = _rms(h_ref[...], g_ref[...])
    xn_ref[...] = xn
    qp = jnp.dot(xn.astype(BF16), wq_ref[...], preferred_element_type=F32).astype(BF16)
    idx_parts, gate_parts = [], []
    nt = (((1,), (1,)), ((), ()))
    for h in range(PEER_HEADS):
        base = h * 2 * PEER_HALF
        s1 = lax.dot_general(k1_ref[h], qp[:, base:base + PEER_HALF], nt,
                             preferred_element_type=F32)
        s2 = lax.dot_general(k2_ref[h], qp[:, base + PEER_HALF:base + 2 * PEER_HALF], nt,
                             preferred_element_type=F32)
        t1, i1 = _top16_rows(s1)
        t2, i2 = _top16_rows(s2)
        sc, expert = _top16_rows(_staircase(t1, t2, lambda a, b: a + b),
                                 _staircase(i1, i2, lambda a, b: a * PEER_NKEYS + b))
        idx_parts.append(expert)
        ex = jnp.exp(sc - sc[0:1, :])
        gate_parts.append(ex / jnp.sum(ex, axis=0, keepdims=True))
    idx_t = jnp.concatenate(idx_parts, axis=0)
    gate_t = jnp.concatenate(gate_parts, axis=0)
    tb = h_ref.shape[0]
    for c in range(tb // LANES):
        idx_ref[c * LANES:(c + 1) * LANES, :] = idx_t[:, c * LANES:(c + 1) * LANES].T
        gate_ref[c * LANES:(c + 1) * LANES, :] = gate_t[:, c * LANES:(c + 1) * LANES].T


def _peer_topk(h, g_ffn, w_pq_bf, k1_bf, k2_bf, tb):
    n = h.shape[0]
    tok = lambda w: pl.BlockSpec((tb, w), lambda i: (i, 0))
    return pl.pallas_call(
        _peer_topk_kernel,
        grid=(n // tb,),
        in_specs=[tok(D_MODEL), _const_spec((1, D_MODEL)),
                  _const_spec((D_MODEL, PEER_HEADS * 2 * PEER_HALF)),
                  _const_spec((PEER_HEADS, PEER_NKEYS, PEER_HALF)),
                  _const_spec((PEER_HEADS, PEER_NKEYS, PEER_HALF))],
        out_specs=[tok(D_MODEL), tok(PEER_PICKS), tok(PEER_PICKS)],
        out_shape=[jax.ShapeDtypeStruct((n, D_MODEL), F32),
                   jax.ShapeDtypeStruct((n, PEER_PICKS), jnp.int32),
                   jax.ShapeDtypeStruct((n, PEER_PICKS), F32)],
        compiler_params=_params(48),
        name="peer_topk",
    )(h, g_ffn, w_pq_bf, k1_bf, k2_bf)


SUBLANES = 8
PEER_RING = 8
PEER_CHUNKS = D_MODEL // LANES
PEER_EGROUPS = PEER_PICKS // SUBLANES
U_MASK = 0xFFFF0000


def _pack_uv(u_tab, v_tab):
    hi = lax.bitcast_convert_type(u_tab.astype(BF16), jnp.uint16).astype(jnp.uint32) << 16
    lo = lax.bitcast_convert_type(v_tab.astype(BF16), jnp.uint16).astype(jnp.uint32)
    return (hi | lo).reshape(u_tab.shape[0], PEER_CHUNKS, LANES)


def _peer_expert_kernel(idx_ref, xn_ref, gate_ref, h_ref, tab_hbm, o_ref, *scratch, tokens):
    bufs, (hk_rep, hk_scr, sem) = scratch[:PEER_RING], scratch[PEER_RING:]

    def row_copy(slot, e, r):
        return pltpu.make_async_copy(tab_hbm.at[r], bufs[slot].at[e], sem.at[slot])

    def issue(t, slot):
        for e in range(PEER_PICKS):
            row_copy(slot, e, idx_ref[t, e]).start(priority=e % 2)

    def wait(slot):
        for e in range(PEER_PICKS):
            row_copy(slot, e, 0).wait()

    sub = lax.broadcasted_iota(jnp.int32, (SUBLANES, LANES), 0)
    lane_minus_sub = lax.broadcasted_iota(jnp.int32, (SUBLANES, LANES), 1) - sub

    def fold_sublanes(parts):
        step = SUBLANES // 2
        while step >= 1:
            lower = (sub & step) == 0
            nxt = []
            for j in range(len(parts) // 2):
                a, b = parts[(j // step) * 2 * step + j % step], parts[(j // step) * 2 * step + j % step + step]
                nxt.append(jnp.where(lower, a, pltpu.roll(b, step, 0))
                           + jnp.where(lower, pltpu.roll(a, SUBLANES - step, 0), b))
            parts = nxt
            step //= 2
        return parts[0]

    def u_pass(t, slot):
        row = pl.ds(t, 1)
        x_row = xn_ref[row, :]
        x_lo = jnp.concatenate([x_row[:, c * LANES:(c + 1) * LANES] for c in range(SUBLANES)], axis=0)
        x_hi = jnp.concatenate([x_row[:, c * LANES:(c + 1) * LANES] for c in range(SUBLANES, PEER_CHUNKS)], axis=0)
        a_diag = jnp.zeros((SUBLANES, LANES), F32)
        for g in range(PEER_EGROUPS):
            parts = []
            for j in range(SUBLANES):
                w = bufs[slot][g * SUBLANES + j]
                parts.append(pltpu.bitcast(w[:SUBLANES] & jnp.uint32(U_MASK), F32) * x_lo
                             + pltpu.bitcast(w[SUBLANES:] & jnp.uint32(U_MASK), F32) * x_hi)
            a_g = jnp.sum(fold_sublanes(parts), axis=1, keepdims=True)
            a_diag = jnp.where(lane_minus_sub == g * SUBLANES, a_g, a_diag)
        a_row = jnp.sum(a_diag, axis=0, keepdims=True)
        hk_scr[slot:slot + 1, :] = _gelu(a_row) * gate_ref[row, :]

    def v_pass(t, slot):
        row = pl.ds(t, 1)
        hk_row = jnp.broadcast_to(hk_scr[slot:slot + 1, :], (SUBLANES, LANES))
        for g in range(PEER_EGROUPS):
            hk_rep[g * SUBLANES:(g + 1) * SUBLANES, :] = jnp.broadcast_to(
                jnp.sum(jnp.where(lane_minus_sub == g * SUBLANES, hk_row, 0.0), axis=1, keepdims=True),
                (SUBLANES, LANES))
        n_chains = 4
        lo, hi = [None] * n_chains, [None] * n_chains
        for e in range(PEER_PICKS):
            w = bufs[slot][e]
            hk_e = jnp.broadcast_to(hk_rep[e:e + 1, :], (SUBLANES, LANES))
            p_lo = pltpu.bitcast(w[:SUBLANES] << 16, F32) * hk_e
            p_hi = pltpu.bitcast(w[SUBLANES:] << 16, F32) * hk_e
            i = e % n_chains
            lo[i] = p_lo if lo[i] is None else lo[i] + p_lo
            hi[i] = p_hi if hi[i] is None else hi[i] + p_hi
        out_lo = (lo[0] + lo[1]) + (lo[2] + lo[3])
        out_hi = (hi[0] + hi[1]) + (hi[2] + hi[3])
        out_row = jnp.concatenate([out_lo[c:c + 1, :] for c in range(SUBLANES)]
                                  + [out_hi[c:c + 1, :] for c in range(SUBLANES)], axis=1)
        o_ref[row, :] = h_ref[row, :] + out_row

    def phase(k, s, do_v, do_issue):
        wait(s)
        if do_issue:
            issue(k - 2 + PEER_RING, (s - 2) % PEER_RING)
        u_pass(k, s)
        if do_v:
            v_pass(k - 1, (s - 1) % PEER_RING)

    def static_group(g):
        for s in range(PEER_RING):
            k = g * PEER_RING + s
            phase(k, s, k >= 1, k >= 2 and k - 2 + PEER_RING < tokens)

    def group(g, carry):
        for s in range(PEER_RING):
            phase(g * PEER_RING + s, s, True, True)
        return carry

    n_groups = tokens // PEER_RING
    for s in range(PEER_RING):
        issue(s, s)
    static_group(0)
    if n_groups > 1:
        lax.fori_loop(1, n_groups - 1, group, 0)
        static_group(n_groups - 1)
    v_pass(tokens - 1, (tokens - 1) % PEER_RING)


def _peer_expert(idx, xn, gate, h, tab, tokens):
    n = h.shape[0]
    assert tokens % PEER_RING == 0 and n % tokens == 0
    tok = lambda w: pl.BlockSpec((tokens, w), lambda i: (i, 0))
    kern = functools.partial(_peer_expert_kernel, tokens=tokens)
    return pl.pallas_call(
        kern,
        grid=(n // tokens,),
        in_specs=[pl.BlockSpec((tokens, PEER_PICKS), lambda i: (i, 0), memory_space=pltpu.SMEM),
                  tok(D_MODEL), tok(PEER_PICKS), tok(D_MODEL),
                  pl.BlockSpec(memory_space=pl.ANY)],
        out_specs=tok(D_MODEL),
        out_shape=jax.ShapeDtypeStruct((n, D_MODEL), F32),
        scratch_shapes=[pltpu.VMEM((PEER_PICKS, PEER_CHUNKS, LANES), jnp.uint32)
                        for _ in range(PEER_RING)] + [
                        pltpu.VMEM((PEER_PICKS, LANES), F32),
                        pltpu.VMEM((PEER_RING, PEER_PICKS), F32),
                        pltpu.SemaphoreType.DMA((PEER_RING,))],
        compiler_params=_params(40),
        name="peer_expert",
    )(idx, xn, gate, h, tab)


def _final_kernel(h_ref, p_ref, gg_ref, wg_ref, wp_ref, gp_ref, gf_ref, y_ref):
    h = h_ref[...]
    gate = jax.nn.sigmoid(jnp.dot(_rms(h, gg_ref[...]).astype(BF16), wg_ref[...],
                                  preferred_element_type=F32))
    ple = jnp.dot(p_ref[...].astype(BF16), wp_ref[...], preferred_element_type=F32)
    h = h + _rms(ple, gp_ref[...]) * gate
    y_ref[...] = _rms(h, gf_ref[...])


def _final(h, p, g_gate, w_gate_bf, w_ple_bf, g_ple, g_final, tb):
    n = h.shape[0]
    tok = lambda w: pl.BlockSpec((tb, w), lambda i: (i, 0))
    return pl.pallas_call(
        _final_kernel,
        grid=(n // tb,),
        in_specs=[tok(D_MODEL), tok(PLE_DIM), _const_spec((1, D_MODEL)),
                  _const_spec((D_MODEL, D_MODEL)), _const_spec((PLE_DIM, D_MODEL)),
                  _const_spec((1, D_MODEL)), _const_spec((1, D_MODEL))],
        out_specs=tok(D_MODEL),
        out_shape=jax.ShapeDtypeStruct((n, D_MODEL), F32),
        compiler_params=_params(48),
        name="final",
    )(h, p, g_gate, w_gate_bf, w_ple_bf, g_ple, g_final)


def _rope_tables(pos):
    half = HEAD_DIM // 2
    inv_freq = 1.0 / (ROPE_THETA ** (jnp.arange(0, HEAD_DIM, 2, dtype=F32) / HEAD_DIM))
    ang = pos.astype(F32)[:, None] * inv_freq[None, :]
    cos, sin = jnp.cos(ang), jnp.sin(ang)
    reps = LANES // HEAD_DIM
    return (jnp.tile(jnp.concatenate([cos, cos], axis=1), (1, reps)),
            jnp.tile(jnp.concatenate([-sin, sin], axis=1), (1, reps)))


def _row(v):
    return v.reshape(1, -1).astype(F32)


def _token_stages(x, p, cos_t, sin_t, attn_fn, wm_bf, bias, lw, tb):
    q, k, v, u, vn = _in_proj(x, lw["g_mix"], lw["w_in"], cos_t, sin_t, lw["g_sgu"], lw["b_sgu"], tb)
    an = attn_fn(q, k, v)
    h = _sgu_out(u, vn, wm_bf, bias, lw["g_sgu_out"], an, lw["w_out"], x, tb)
    xn, idx, gate = _peer_topk(h, lw["g_ffn"], lw["w_pq"], lw["sub_k1"], lw["sub_k2"], tb)
    h = _peer_expert(idx, xn, gate, h, lw["uv_tab"], min(256, x.shape[0]))
    return h, k, v, vn


def kernel(x_prompt, x_sample, cache_k, cache_v, p_prompt, p_sample, g_mix, w_in, sinks, g_sgu, b_sgu,
           w_s, b_s, g_attn_out, g_sgu_out, w_out, g_ffn, w_pq, sub_k1, sub_k2, u_tab, v_tab, g_gate,
           w_gate, w_ple, g_ple, g_final):
    depth = w_in.shape[0]
    bsz, seq, _ = x_prompt.shape
    dbsz, dseq, _ = x_sample.shape
    wc = cache_k.shape[2]
    assert depth == 1, "the final kernel fuses the layer tail with the final norm"
    assert seq % WINDOW == 0 and wc == WINDOW and WINDOW % dseq == 0 and dseq <= SGU_CHUNK
    n_p, n_s = bsz * seq, dbsz * dseq
    tb_p = 256 if seq % 256 == 0 else WINDOW
    tb_s = WINDOW
    assert n_s % tb_s == 0 and seq % tb_p == 0
    nb = seq // WINDOW

    cos_p, sin_p = _rope_tables(jnp.arange(seq, dtype=jnp.int32))
    cos_s, sin_s = _rope_tables(PAST_LEN + (jnp.arange(tb_s, dtype=jnp.int32) % dseq))
    causal = jnp.tril(jnp.ones((SGU_CHUNK, SGU_CHUNK), F32))
    eye = jnp.eye(SGU_CHUNK // dseq, dtype=F32)

    hp = x_prompt.reshape(n_p, D_MODEL)
    hs = x_sample.reshape(n_s, D_MODEL)
    kp_l, vp_l, ks_l, vs_l, gs_l = [], [], [], [], []
    for i in range(depth):
        lw = dict(
            g_mix=_row(g_mix[i]), w_in=w_in[i].astype(BF16), g_sgu=_row(g_sgu[i]), b_sgu=_row(b_sgu[i]),
            g_sgu_out=_row(g_sgu_out[i]), w_out=w_out[i].astype(BF16), g_ffn=_row(g_ffn[i]),
            w_pq=w_pq[i].astype(BF16), sub_k1=sub_k1[i].astype(BF16), sub_k2=sub_k2[i].astype(BF16),
            uv_tab=_pack_uv(u_tab[i], v_tab[i]))
        sinks_i = sinks[i].astype(F32)
        g_ao = _row(g_attn_out[i])
        ws = w_s[i].astype(F32) * causal
        wm_p = ws.astype(BF16)
        bias_p = jnp.broadcast_to(b_s[i].astype(F32)[:, :, None], (SGU_HEADS, SGU_CHUNK, SGU_GROUP))
        wm_s = jnp.einsum("ab,gts->gatbs", eye, ws[:, :dseq, :dseq]).reshape(
            SGU_HEADS, SGU_CHUNK, SGU_CHUNK).astype(BF16)
        bias_s = jnp.broadcast_to(jnp.tile(b_s[i].astype(F32)[:, :dseq], (1, SGU_CHUNK // dseq))[:, :, None],
                                  (SGU_HEADS, SGU_CHUNK, SGU_GROUP))

        def attn_prompt(q, k, v):
            prev = lambda j: (jnp.where(j % nb == 0, j, j - 1), 0)
            cur = lambda j: (j, 0)
            return _attention(sinks_i, q, k, k, v, v, g_ao, prev, cur, nb, True)

        def attn_sample(q, k, v, ck=cache_k[i], cv=cache_v[i]):
            pad = lambda a: jnp.pad(a.reshape(dbsz, dseq, -1), ((0, 0), (0, WINDOW - dseq), (0, 0))
                                    ).reshape(dbsz * WINDOW, -1)
            ident = lambda j: (j, 0)
            an = _attention(sinks_i, pad(q), ck.reshape(dbsz * wc, KV_COLS), pad(k),
                            cv.reshape(dbsz * wc, KV_COLS), pad(v), g_ao, ident, ident, 1, False)
            return an.reshape(dbsz, WINDOW, ATTN_WIDTH)[:, :dseq].reshape(n_s, ATTN_WIDTH)

        hp, kp, vp, _ = _token_stages(hp, None, cos_p, sin_p, attn_prompt, wm_p, bias_p, lw, tb_p)
        hs, ks, vs, vns = _token_stages(hs, None, cos_s, sin_s, attn_sample, wm_s, bias_s, lw, tb_s)
        fin = (_row(g_gate[i]), w_gate[i].astype(BF16), w_ple[i].astype(BF16), _row(g_ple[i]),
               _row(g_final))
        hp = _final(hp, p_prompt[i].reshape(n_p, PLE_DIM), *fin, tb_p)
        hs = _final(hs, p_sample[i].reshape(n_s, PLE_DIM), *fin, tb_s)

        kp_l.append(kp.reshape(bsz, seq, N_KV, HEAD_DIM)[:, seq - WINDOW:])
        vp_l.append(vp.reshape(bsz, seq, N_KV, HEAD_DIM)[:, seq - WINDOW:])
        ks_l.append(jnp.concatenate([cache_k[i][:, dseq:], ks.reshape(dbsz, dseq, N_KV, HEAD_DIM)], axis=1))
        vs_l.append(jnp.concatenate([cache_v[i][:, dseq:], vs.reshape(dbsz, dseq, N_KV, HEAD_DIM)], axis=1))
        gs_l.append(vns.reshape(dbsz, dseq, SGU_HEADS, SGU_GROUP))

    return (hp.reshape(bsz, seq, D_MODEL), hs.reshape(dbsz, dseq, D_MODEL),
            jnp.stack(kp_l), jnp.stack(vp_l), jnp.stack(ks_l), jnp.stack(vs_l), jnp.stack(gs_l))
```

```python
import functools

import jax
import jax.numpy as jnp
from jax import lax
from jax.experimental import pallas as pl
from jax.experimental.pallas import tpu as pltpu

F32 = jnp.float32
BF16 = jnp.bfloat16

D_MODEL = 2048
PAST_LEN = 16384
HEAD_DIM = 64
N_HEADS = 16
N_KV = 2
Q_PER_KV = N_HEADS // N_KV
WINDOW = 128
ROPE_THETA = 10000.0
NEG_INF = -1e30
ATTN_WIDTH = N_HEADS * HEAD_DIM
SGU_CHUNK = 128
SGU_GROUP = 128
SGU_HEADS = 8
SGU_WIDTH = SGU_HEADS * SGU_GROUP
Q_COLS = N_HEADS * HEAD_DIM
KV_COLS = N_KV * HEAD_DIM
IN_COLS = Q_COLS + 2 * KV_COLS + 2 * SGU_WIDTH
PEER_HEADS = 8
PEER_NKEYS = 128
PEER_HALF = 128
PEER_TOPK = 16
PEER_PICKS = PEER_HEADS * PEER_TOPK
PLE_DIM = 256
EPS = 1e-6

LANES = 128
MIB = 1024 * 1024


def _rms(x, g):
    return x * lax.rsqrt(jnp.mean(x * x, axis=-1, keepdims=True) + EPS) * g


def _gelu(x):
    return 0.5 * x * (1.0 + lax.erf(x * (0.5 ** 0.5)))


def _const_spec(shape):
    nd = len(shape)
    return pl.BlockSpec(shape, lambda *_: (0,) * nd, pipeline_mode=pl.Buffered(1))


def _params(vmem_mib, n_grid=1):
    return pltpu.CompilerParams(
        dimension_semantics=("arbitrary",) * n_grid, vmem_limit_bytes=vmem_mib * MIB)


def _in_proj_kernel(x_ref, g_ref, w_ref, cos_ref, sin_ref, gs_ref, bs_ref,
                    q_ref, k_ref, v_ref, u_ref, vn_ref):
    xn = _rms(x_ref[...], g_ref[...]).astype(BF16)
    z = jnp.dot(xn, w_ref[...], preferred_element_type=F32)
    cos = cos_ref[...]
    sin = sin_ref[...]
    lane = lax.broadcasted_iota(jnp.int32, cos.shape, 1)
    first_half = (lane % HEAD_DIM) < (HEAD_DIM // 2)

    def rope(zc):
        swapped = jnp.where(first_half,
                            pltpu.roll(zc, LANES - HEAD_DIM // 2, 1),
                            pltpu.roll(zc, HEAD_DIM // 2, 1))
        return zc * cos + swapped * sin

    for c in range(Q_COLS // LANES):
        q_ref[:, c * LANES:(c + 1) * LANES] = rope(z[:, c * LANES:(c + 1) * LANES])
    k_ref[...] = rope(z[:, Q_COLS:Q_COLS + KV_COLS])
    v_ref[...] = z[:, Q_COLS + KV_COLS:Q_COLS + 2 * KV_COLS]
    u0 = Q_COLS + 2 * KV_COLS
    u_ref[...] = _gelu(z[:, u0:u0 + SGU_WIDTH])
    sv = _gelu(z[:, u0 + SGU_WIDTH:u0 + 2 * SGU_WIDTH])
    mu = jnp.mean(sv, axis=-1, keepdims=True)
    svc = sv - mu
    var = jnp.mean(svc * svc, axis=-1, keepdims=True)
    vn_ref[...] = svc * lax.rsqrt(var + EPS) * gs_ref[...] + bs_ref[...]


def _in_proj(x, g_mix, w_in_bf, cos_t, sin_t, g_sgu, b_sgu, tb):
    n = x.shape[0]
    period_blocks = cos_t.shape[0] // tb
    tok = lambda w: pl.BlockSpec((tb, w), lambda i: (i, 0))
    tab = pl.BlockSpec((tb, LANES), lambda i: (i % period_blocks, 0))
    return pl.pallas_call(
        _in_proj_kernel,
        grid=(n // tb,),
        in_specs=[tok(D_MODEL), _const_spec((1, D_MODEL)), _const_spec((D_MODEL, IN_COLS)),
                  tab, tab, _const_spec((1, SGU_WIDTH)), _const_spec((1, SGU_WIDTH))],
        out_specs=[tok(Q_COLS), tok(KV_COLS), tok(KV_COLS), tok(SGU_WIDTH), tok(SGU_WIDTH)],
        out_shape=[jax.ShapeDtypeStruct((n, w), F32)
                   for w in (Q_COLS, KV_COLS, KV_COLS, SGU_WIDTH, SGU_WIDTH)],
        compiler_params=_params(48),
        name="in_proj",
    )(x, g_mix, w_in_bf, cos_t, sin_t, g_sgu, b_sgu)


def _attn_kernel(sinks_ref, q_ref, ka_ref, kb_ref, va_ref, vb_ref, g_ref, o_ref, acc_ref, *, blocks_per_seq):
    q = q_ref[...]
    kk = jnp.concatenate([ka_ref[...], kb_ref[...]], axis=0).astype(BF16)
    vv = jnp.concatenate([va_ref[...], vb_ref[...]], axis=0).astype(BF16)
    r = lax.broadcasted_iota(jnp.int32, (WINDOW, 2 * WINDOW), 0)
    c = lax.broadcasted_iota(jnp.int32, (WINDOW, 2 * WINDOW), 1)
    first = pl.program_id(0) % blocks_per_seq == 0
    valid = (c > jnp.where(first, jnp.maximum(r, WINDOW - 1), r)) & (c <= r + WINDOW)
    for g in range(N_KV):
        kg = kk[:, g * HEAD_DIM:(g + 1) * HEAD_DIM]
        vg = vv[:, g * HEAD_DIM:(g + 1) * HEAD_DIM]
        for hh in range(Q_PER_KV):
            h = g * Q_PER_KV + hh
            qh = q[:, h * HEAD_DIM:(h + 1) * HEAD_DIM].astype(BF16)
            s = lax.dot_general(qh, kg, (((1,), (1,)), ((), ())),
                                preferred_element_type=F32) * (HEAD_DIM ** -0.5)
            s = jnp.where(valid, s, NEG_INF)
            sink = sinks_ref[h]
            m = jnp.maximum(jnp.max(s, axis=1, keepdims=True), sink)
            p = jnp.exp(s - m)
            denom = jnp.sum(p, axis=1, keepdims=True) + jnp.exp(sink - m)
            p = (p / denom).astype(BF16)
            acc_ref[:, h * HEAD_DIM:(h + 1) * HEAD_DIM] = jnp.dot(p, vg, preferred_element_type=F32)
    o_ref[...] = _rms(acc_ref[...], g_ref[...]).astype(o_ref.dtype)


def _attention(sinks, q, k, v, g_attn_out, blocks_per_seq):
    n = q.shape[0]
    prev = pl.BlockSpec((WINDOW, KV_COLS), lambda j: (jnp.where(j % blocks_per_seq == 0, j, j - 1), 0))
    cur = pl.BlockSpec((WINDOW, KV_COLS), lambda j: (j, 0))
    kern = functools.partial(_attn_kernel, blocks_per_seq=blocks_per_seq)
    return pl.pallas_call(
        kern,
        grid=(n // WINDOW,),
        in_specs=[pl.BlockSpec(memory_space=pltpu.SMEM),
                  pl.BlockSpec((WINDOW, Q_COLS), lambda i: (i, 0)),
                  prev, cur, prev, cur,
                  _const_spec((1, ATTN_WIDTH))],
        out_specs=pl.BlockSpec((WINDOW, ATTN_WIDTH), lambda i: (i, 0)),
        out_shape=jax.ShapeDtypeStruct((n, ATTN_WIDTH), BF16),
        scratch_shapes=[pltpu.VMEM((WINDOW, ATTN_WIDTH), F32)],
        compiler_params=_params(32),
        name="attention",
    )(sinks, q, k, k, v, v, g_attn_out)


def _attn_sample_kernel(sinks_ref, q_ref, ck_ref, cv_ref, kn_ref, vn_ref, g_ref, o_ref, acc_ref, *, dseq, seqs):
    rows = Q_PER_KV * dseq
    r = lax.broadcasted_iota(jnp.int32, (rows, WINDOW), 0) % dseq
    c = lax.broadcasted_iota(jnp.int32, (rows, WINDOW), 1)
    valid_cache = c > r
    valid_new = c <= r
    head_of_row = lax.broadcasted_iota(jnp.int32, (rows, 1), 0) // dseq
    sink_cols = []
    for g in range(N_KV):
        col = jnp.zeros((rows, 1), F32)
        for hh in range(Q_PER_KV):
            col = jnp.where(head_of_row == hh, sinks_ref[g * Q_PER_KV + hh], col)
        sink_cols.append(col)
    pad = jnp.zeros((WINDOW - dseq, HEAD_DIM), F32)
    nt = (((1,), (1,)), ((), ()))
    for i in range(seqs):
        new = slice(i * dseq, (i + 1) * dseq)
        old = slice(i * WINDOW, (i + 1) * WINDOW)
        q = q_ref[new, :]
        for g in range(N_KV):
            cols = slice(g * HEAD_DIM, (g + 1) * HEAD_DIM)
            qg = jnp.concatenate([q[:, (g * Q_PER_KV + hh) * HEAD_DIM:(g * Q_PER_KV + hh + 1) * HEAD_DIM]
                                  for hh in range(Q_PER_KV)], axis=0).astype(BF16)
            k_new = jnp.concatenate([kn_ref[new, cols], pad], axis=0).astype(BF16)
            v_new = jnp.concatenate([vn_ref[new, cols], pad], axis=0).astype(BF16)
            s_old = lax.dot_general(qg, ck_ref[old, cols].astype(BF16), nt,
                                    preferred_element_type=F32) * (HEAD_DIM ** -0.5)
            s_new = lax.dot_general(qg, k_new, nt, preferred_element_type=F32) * (HEAD_DIM ** -0.5)
            s_old = jnp.where(valid_cache, s_old, NEG_INF)
            s_new = jnp.where(valid_new, s_new, NEG_INF)
            sink = sink_cols[g]
            m = jnp.maximum(jnp.maximum(jnp.max(s_old, axis=1, keepdims=True),
                                        jnp.max(s_new, axis=1, keepdims=True)), sink)
            p_old = jnp.exp(s_old - m)
            p_new = jnp.exp(s_new - m)
            denom = (jnp.sum(p_old, axis=1, keepdims=True) + jnp.sum(p_new, axis=1, keepdims=True)
                     + jnp.exp(sink - m))
            o = (jnp.dot((p_old / denom).astype(BF16), cv_ref[old, cols].astype(BF16),
                         preferred_element_type=F32)
                 + jnp.dot((p_new / denom).astype(BF16), v_new, preferred_element_type=F32))
            for hh in range(Q_PER_KV):
                h = g * Q_PER_KV + hh
                acc_ref[new, h * HEAD_DIM:(h + 1) * HEAD_DIM] = o[hh * dseq:(hh + 1) * dseq, :]
    o_ref[...] = _rms(acc_ref[...], g_ref[...]).astype(o_ref.dtype)


def _attention_sample(sinks, q, cache_k, cache_v, k_new, v_new, g_attn_out, dseq, seqs):
    n = q.shape[0]
    tok = lambda w: pl.BlockSpec((seqs * dseq, w), lambda i: (i, 0))
    cache = pl.BlockSpec((seqs * WINDOW, KV_COLS), lambda i: (i, 0))
    kern = functools.partial(_attn_sample_kernel, dseq=dseq, seqs=seqs)
    return pl.pallas_call(
        kern,
        grid=(n // (seqs * dseq),),
        in_specs=[pl.BlockSpec(memory_space=pltpu.SMEM), tok(Q_COLS), cache, cache, tok(KV_COLS), tok(KV_COLS),
                  _const_spec((1, ATTN_WIDTH))],
        out_specs=tok(ATTN_WIDTH),
        out_shape=jax.ShapeDtypeStruct((n, ATTN_WIDTH), BF16),
        scratch_shapes=[pltpu.VMEM((seqs * dseq, ATTN_WIDTH), F32)],
        compiler_params=_params(32),
        name="attention_sample",
    )(sinks, q, cache_k, cache_v, k_new, v_new, g_attn_out)


def _sgu_out_kernel(u_ref, vn_ref, wm_ref, bias_ref, g_ref, an_ref, wo_ref, x_ref, h_ref, s_ref):
    tb = u_ref.shape[0]
    for ch in range(tb // SGU_CHUNK):
        rows = slice(ch * SGU_CHUNK, (ch + 1) * SGU_CHUNK)
        for g in range(SGU_HEADS):
            cols = slice(g * SGU_GROUP, (g + 1) * SGU_GROUP)
            mix = jnp.dot(wm_ref[g], vn_ref[rows, cols].astype(BF16),
                          preferred_element_type=F32) + bias_ref[g]
            s_ref[rows, cols] = u_ref[rows, cols] * mix
    sn = _rms(s_ref[...], g_ref[...]).astype(BF16)
    mixed = jnp.dot(an_ref[...], wo_ref[:ATTN_WIDTH, :], preferred_element_type=F32)
    mixed = mixed + jnp.dot(sn, wo_ref[ATTN_WIDTH:, :], preferred_element_type=F32)
    h_ref[...] = x_ref[...] + mixed


def _sgu_out(u, vn, wm_bf, bias, g_sgu_out, an, w_out_bf, x, tb):
    n = x.shape[0]
    tok = lambda w: pl.BlockSpec((tb, w), lambda i: (i, 0))
    return pl.pallas_call(
        _sgu_out_kernel,
        grid=(n // tb,),
        in_specs=[tok(SGU_WIDTH), tok(SGU_WIDTH),
                  _const_spec((SGU_HEADS, SGU_CHUNK, SGU_CHUNK)),
                  _const_spec((SGU_HEADS, SGU_CHUNK, SGU_GROUP)),
                  _const_spec((1, SGU_WIDTH)), tok(ATTN_WIDTH),
                  _const_spec((D_MODEL, D_MODEL)), tok(D_MODEL)],
        out_specs=tok(D_MODEL),
        out_shape=jax.ShapeDtypeStruct((n, D_MODEL), F32),
        scratch_shapes=[pltpu.VMEM((tb, SGU_WIDTH), F32)],
        compiler_params=_params(48),
        name="sgu_out",
    )(u, vn, wm_bf, bias, g_sgu_out, an, w_out_bf, x)


def _top16_rows(s, payload=None):
    n_rows = s.shape[0]
    row = lax.broadcasted_iota(jnp.int32, s.shape, 0)
    vals, picks = [], []
    for _ in range(PEER_TOPK):
        m = jnp.max(s, axis=0, keepdims=True)
        i = jnp.min(jnp.where(s == m, row, n_rows), axis=0, keepdims=True)
        hit = row == i
        vals.append(m)
        picks.append(i if payload is None else jnp.max(jnp.where(hit, payload, -1), axis=0, keepdims=True))
        s = jnp.where(hit, -jnp.inf, s)
    return jnp.concatenate(vals, axis=0), jnp.concatenate(picks, axis=0)


_CAND_WIDTHS = (16, 8, 8, 8, 4, 4, 4, 4)


def _staircase(a, b, combine):
    rows = [combine(a[i:i + 1, :], b[:w, :]) for i, w in enumerate(_CAND_WIDTHS)]
    rows.append(combine(a[len(_CAND_WIDTHS):, :], b[0:1, :]))
    return jnp.concatenate(rows, axis=0)


def _peer_topk_kernel(h_ref, g_ref, wq_ref, k1_ref, k2_ref, xn_ref, idx_ref, gate_ref):
    xn = _rms(h_ref[...], g_ref[...])
    xn_ref[...] = xn
    qp = jnp.dot(xn.astype(BF16), wq_ref[...], preferred_element_type=F32).astype(BF16)
    idx_parts, gate_parts = [], []
    nt = (((1,), (1,)), ((), ()))
    for h in range(PEER_HEADS):
        base = h * 2 * PEER_HALF
        s1 = lax.dot_general(k1_ref[h], qp[:, base:base + PEER_HALF], nt,
                             preferred_element_type=F32)
        s2 = lax.dot_general(k2_ref[h], qp[:, base + PEER_HALF:base + 2 * PEER_HALF], nt,
                             preferred_element_type=F32)
        t1, i1 = _top16_rows(s1)
        t2, i2 = _top16_rows(s2)
        sc, expert = _top16_rows(_staircase(t1, t2, lambda a, b: a + b),
                                 _staircase(i1, i2, lambda a, b: a * PEER_NKEYS + b))
        idx_parts.append(expert)
        ex = jnp.exp(sc - sc[0:1, :])
        gate_parts.append(ex / jnp.sum(ex, axis=0, keepdims=True))
    idx_t = jnp.concatenate(idx_parts, axis=0)
    gate_t = jnp.concatenate(gate_parts, axis=0)
    tb = h_ref.shape[0]
    for c in range(tb // LANES):
        idx_ref[c * LANES:(c + 1) * LANES, :] = idx_t[:, c * LANES:(c + 1) * LANES].T
        gate_ref[c * LANES:(c + 1) * LANES, :] = gate_t[:, c * LANES:(c + 1) * LANES].T


def _peer_topk(h, g_ffn, w_pq_bf, k1_bf, k2_bf, tb):
    n = h.shape[0]
    tok = lambda w: pl.BlockSpec((tb, w), lambda i: (i, 0))
    return pl.pallas_call(
        _peer_topk_kernel,
        grid=(n // tb,),
        in_specs=[tok(D_MODEL), _const_spec((1, D_MODEL)),
                  _const_spec((D_MODEL, PEER_HEADS * 2 * PEER_HALF)),
                  _const_spec((PEER_HEADS, PEER_NKEYS, PEER_HALF)),
                  _const_spec((PEER_HEADS, PEER_NKEYS, PEER_HALF))],
        out_specs=[tok(D_MODEL), tok(PEER_PICKS), tok(PEER_PICKS)],
        out_shape=[jax.ShapeDtypeStruct((n, D_MODEL), F32),
                   jax.ShapeDtypeStruct((n, PEER_PICKS), jnp.int32),
                   jax.ShapeDtypeStruct((n, PEER_PICKS), F32)],
        compiler_params=_params(48),
        name="peer_topk",
    )(h, g_ffn, w_pq_bf, k1_bf, k2_bf)


SUBLANES = 8
PEER_RING = 8
PEER_CHUNKS = D_MODEL // LANES
PEER_EGROUPS = PEER_PICKS // SUBLANES
U_MASK = 0xFFFF0000


def _pack_uv(u_tab, v_tab):
    hi = lax.bitcast_convert_type(u_tab.astype(BF16), jnp.uint16).astype(jnp.uint32) << 16
    lo = lax.bitcast_convert_type(v_tab.astype(BF16), jnp.uint16).astype(jnp.uint32)
    return (hi | lo).reshape(u_tab.shape[0], PEER_CHUNKS, LANES)


def _peer_expert_kernel(idx_ref, nidx_ref, xn_ref, gate_ref, h_ref, tab_hbm, o_ref, *scratch, tokens):
    bufs, (hk_rep, hk_scr, sem) = scratch[:PEER_RING], scratch[PEER_RING:]

    def row_copy(slot, e, r):
        return pltpu.make_async_copy(tab_hbm.at[r], bufs[slot].at[e], sem.at[slot])

    def issue(src_idx_ref, t, slot):
        for e in range(PEER_PICKS):
            row_copy(slot, e, src_idx_ref[t, e]).start(priority=e % 2)

    def wait(slot):
        for e in range(PEER_PICKS):
            row_copy(slot, e, 0).wait()

    sub = lax.broadcasted_iota(jnp.int32, (SUBLANES, LANES), 0)
    lane_minus_sub = lax.broadcasted_iota(jnp.int32, (SUBLANES, LANES), 1) - sub

    def fold_sublanes(parts):
        step = SUBLANES // 2
        while step >= 1:
            lower = (sub & step) == 0
            nxt = []
            for j in range(len(parts) // 2):
                a, b = parts[(j // step) * 2 * step + j % step], parts[(j // step) * 2 * step + j % step + step]
                nxt.append(jnp.where(lower, a, pltpu.roll(b, step, 0))
                           + jnp.where(lower, pltpu.roll(a, SUBLANES - step, 0), b))
            parts = nxt
            step //= 2
        return parts[0]

    def u_pass(t, slot):
        row = pl.ds(t, 1)
        x_row = xn_ref[row, :]
        x_lo = jnp.concatenate([x_row[:, c * LANES:(c + 1) * LANES] for c in range(SUBLANES)], axis=0)
        x_hi = jnp.concatenate([x_row[:, c * LANES:(c + 1) * LANES] for c in range(SUBLANES, PEER_CHUNKS)], axis=0)
        a_diag = jnp.zeros((SUBLANES, LANES), F32)
        for g in range(PEER_EGROUPS):
            parts = []
            for j in range(SUBLANES):
                w = bufs[slot][g * SUBLANES + j]
                parts.append(pltpu.bitcast(w[:SUBLANES] & jnp.uint32(U_MASK), F32) * x_lo
                             + pltpu.bitcast(w[SUBLANES:] & jnp.uint32(U_MASK), F32) * x_hi)
            a_g = jnp.sum(fold_sublanes(parts), axis=1, keepdims=True)
            a_diag = jnp.where(lane_minus_sub == g * SUBLANES, a_g, a_diag)
        a_row = jnp.sum(a_diag, axis=0, keepdims=True)
        hk_scr[slot:slot + 1, :] = _gelu(a_row) * gate_ref[row, :]

    def v_pass(t, slot):
        row = pl.ds(t, 1)
        hk_row = jnp.broadcast_to(hk_scr[slot:slot + 1, :], (SUBLANES, LANES))
        for g in range(PEER_EGROUPS):
            hk_rep[g * SUBLANES:(g + 1) * SUBLANES, :] = jnp.broadcast_to(
                jnp.sum(jnp.where(lane_minus_sub == g * SUBLANES, hk_row, 0.0), axis=1, keepdims=True),
                (SUBLANES, LANES))
        n_chains = 4
        lo, hi = [None] * n_chains, [None] * n_chains
        for e in range(PEER_PICKS):
            w = bufs[slot][e]
            hk_e = jnp.broadcast_to(hk_rep[e:e + 1, :], (SUBLANES, LANES))
            p_lo = pltpu.bitcast(w[:SUBLANES] << 16, F32) * hk_e
            p_hi = pltpu.bitcast(w[SUBLANES:] << 16, F32) * hk_e
            i = e % n_chains
            lo[i] = p_lo if lo[i] is None else lo[i] + p_lo
            hi[i] = p_hi if hi[i] is None else hi[i] + p_hi
        out_lo = (lo[0] + lo[1]) + (lo[2] + lo[3])
        out_hi = (hi[0] + hi[1]) + (hi[2] + hi[3])
        out_row = jnp.concatenate([out_lo[c:c + 1, :] for c in range(SUBLANES)]
                                  + [out_hi[c:c + 1, :] for c in range(SUBLANES)], axis=1)
        o_ref[row, :] = h_ref[row, :] + out_row

    step, last_step = pl.program_id(0), pl.num_programs(0) - 1
    ahead = PEER_RING - 2

    def phase(k, s, do_v):
        wait(s)
        j, slot = k + ahead, (s + ahead) % PEER_RING
        if isinstance(j, int) and j >= tokens:
            @pl.when(step < last_step)
            def _():
                issue(nidx_ref, j - tokens, slot)
        else:
            issue(idx_ref, j, slot)
        u_pass(k, s)
        if do_v:
            v_pass(k - 1, (s - 1) % PEER_RING)

    def static_group(g):
        for s in range(PEER_RING):
            k = g * PEER_RING + s
            phase(k, s, k >= 1)

    def group(g, carry):
        for s in range(PEER_RING):
            phase(g * PEER_RING + s, s, True)
        return carry

    @pl.when(step == 0)
    def _():
        for s in range(ahead):
            issue(idx_ref, s, s)

    n_groups = tokens // PEER_RING
    static_group(0)
    if n_groups > 1:
        lax.fori_loop(1, n_groups - 1, group, 0)
        static_group(n_groups - 1)
    v_pass(tokens - 1, (tokens - 1) % PEER_RING)


def _peer_expert(idx, xn, gate, h, tab, tokens):
    n = h.shape[0]
    assert tokens % PEER_RING == 0 and n % tokens == 0
    tok = lambda w: pl.BlockSpec((tokens, w), lambda i: (i, 0))
    kern = functools.partial(_peer_expert_kernel, tokens=tokens)
    steps = n // tokens
    return pl.pallas_call(
        kern,
        grid=(steps,),
        in_specs=[pl.BlockSpec((tokens, PEER_PICKS), lambda i: (i, 0), memory_space=pltpu.SMEM),
                  pl.BlockSpec((tokens, PEER_PICKS), lambda i: (jnp.minimum(i + 1, steps - 1), 0),
                               memory_space=pltpu.SMEM),
                  tok(D_MODEL), tok(PEER_PICKS), tok(D_MODEL),
                  pl.BlockSpec(memory_space=pl.ANY)],
        out_specs=tok(D_MODEL),
        out_shape=jax.ShapeDtypeStruct((n, D_MODEL), F32),
        scratch_shapes=[pltpu.VMEM((PEER_PICKS, PEER_CHUNKS, LANES), jnp.uint32)
                        for _ in range(PEER_RING)] + [
                        pltpu.VMEM((PEER_PICKS, LANES), F32),
                        pltpu.VMEM((PEER_RING, PEER_PICKS), F32),
                        pltpu.SemaphoreType.DMA((PEER_RING,))],
        compiler_params=_params(40),
        name="peer_expert",
    )(idx, idx, xn, gate, h, tab)


def _final_kernel(h_ref, p_ref, gg_ref, wg_ref, wp_ref, gp_ref, gf_ref, y_ref):
    h = h_ref[...]
    gate = jax.nn.sigmoid(jnp.dot(_rms(h, gg_ref[...]).astype(BF16), wg_ref[...],
                                  preferred_element_type=F32))
    ple = jnp.dot(p_ref[...].astype(BF16), wp_ref[...], preferred_element_type=F32)
    h = h + _rms(ple, gp_ref[...]) * gate
    y_ref[...] = _rms(h, gf_ref[...])


def _final(h, p, g_gate, w_gate_bf, w_ple_bf, g_ple, g_final, tb):
    n = h.shape[0]
    tok = lambda w: pl.BlockSpec((tb, w), lambda i: (i, 0))
    return pl.pallas_call(
        _final_kernel,
        grid=(n // tb,),
        in_specs=[tok(D_MODEL), tok(PLE_DIM), _const_spec((1, D_MODEL)),
                  _const_spec((D_MODEL, D_MODEL)), _const_spec((PLE_DIM, D_MODEL)),
                  _const_spec((1, D_MODEL)), _const_spec((1, D_MODEL))],
        out_specs=tok(D_MODEL),
        out_shape=jax.ShapeDtypeStruct((n, D_MODEL), F32),
        compiler_params=_params(48),
        name="final",
    )(h, p, g_gate, w_gate_bf, w_ple_bf, g_ple, g_final)


def _rope_tables(pos):
    half = HEAD_DIM // 2
    inv_freq = 1.0 / (ROPE_THETA ** (jnp.arange(0, HEAD_DIM, 2, dtype=F32) / HEAD_DIM))
    ang = pos.astype(F32)[:, None] * inv_freq[None, :]
    cos, sin = jnp.cos(ang), jnp.sin(ang)
    reps = LANES // HEAD_DIM
    return (jnp.tile(jnp.concatenate([cos, cos], axis=1), (1, reps)),
            jnp.tile(jnp.concatenate([-sin, sin], axis=1), (1, reps)))


def _row(v):
    return v.reshape(1, -1).astype(F32)


def _token_stages(x, p, cos_t, sin_t, attn_fn, wm_bf, bias, lw, tb):
    q, k, v, u, vn = _in_proj(x, lw["g_mix"], lw["w_in"], cos_t, sin_t, lw["g_sgu"], lw["b_sgu"], tb)
    an = attn_fn(q, k, v)
    h = _sgu_out(u, vn, wm_bf, bias, lw["g_sgu_out"], an, lw["w_out"], x, tb)
    xn, idx, gate = _peer_topk(h, lw["g_ffn"], lw["w_pq"], lw["sub_k1"], lw["sub_k2"], tb)
    h = _peer_expert(idx, xn, gate, h, lw["uv_tab"], min(256, x.shape[0]))
    return h, k, v, vn


def kernel(x_prompt, x_sample, cache_k, cache_v, p_prompt, p_sample, g_mix, w_in, sinks, g_sgu, b_sgu,
           w_s, b_s, g_attn_out, g_sgu_out, w_out, g_ffn, w_pq, sub_k1, sub_k2, u_tab, v_tab, g_gate,
           w_gate, w_ple, g_ple, g_final):
    depth = w_in.shape[0]
    bsz, seq, _ = x_prompt.shape
    dbsz, dseq, _ = x_sample.shape
    wc = cache_k.shape[2]
    assert depth == 1, "the final kernel fuses the layer tail with the final norm"
    assert seq % WINDOW == 0 and wc == WINDOW and WINDOW % dseq == 0 and dseq <= SGU_CHUNK
    n_p, n_s = bsz * seq, dbsz * dseq
    tb_p = 256 if seq % 256 == 0 else WINDOW
    tb_s = WINDOW
    assert n_s % tb_s == 0 and seq % tb_p == 0
    nb = seq // WINDOW

    cos_p, sin_p = _rope_tables(jnp.arange(seq, dtype=jnp.int32))
    cos_s, sin_s = _rope_tables(PAST_LEN + (jnp.arange(tb_s, dtype=jnp.int32) % dseq))
    causal = jnp.tril(jnp.ones((SGU_CHUNK, SGU_CHUNK), F32))
    eye = jnp.eye(SGU_CHUNK // dseq, dtype=F32)

    hp = x_prompt.reshape(n_p, D_MODEL)
    hs = x_sample.reshape(n_s, D_MODEL)
    kp_l, vp_l, ks_l, vs_l, gs_l = [], [], [], [], []
    for i in range(depth):
        lw = dict(
            g_mix=_row(g_mix[i]), w_in=w_in[i].astype(BF16), g_sgu=_row(g_sgu[i]), b_sgu=_row(b_sgu[i]),
            g_sgu_out=_row(g_sgu_out[i]), w_out=w_out[i].astype(BF16), g_ffn=_row(g_ffn[i]),
            w_pq=w_pq[i].astype(BF16), sub_k1=sub_k1[i].astype(BF16), sub_k2=sub_k2[i].astype(BF16),
            uv_tab=_pack_uv(u_tab[i], v_tab[i]))
        sinks_i = sinks[i].astype(F32)
        g_ao = _row(g_attn_out[i])
        ws = w_s[i].astype(F32) * causal
        wm_p = ws.astype(BF16)
        bias_p = jnp.broadcast_to(b_s[i].astype(F32)[:, :, None], (SGU_HEADS, SGU_CHUNK, SGU_GROUP))
        wm_s = jnp.einsum("ab,gts->gatbs", eye, ws[:, :dseq, :dseq]).reshape(
            SGU_HEADS, SGU_CHUNK, SGU_CHUNK).astype(BF16)
        bias_s = jnp.broadcast_to(jnp.tile(b_s[i].astype(F32)[:, :dseq], (1, SGU_CHUNK // dseq))[:, :, None],
                                  (SGU_HEADS, SGU_CHUNK, SGU_GROUP))

        def attn_prompt(q, k, v):
            return _attention(sinks_i, q, k, v, g_ao, nb)

        def attn_sample(q, k, v, ck=cache_k[i], cv=cache_v[i]):
            return _attention_sample(sinks_i, q, ck.reshape(dbsz * wc, KV_COLS), cv.reshape(dbsz * wc, KV_COLS),
                                     k, v, g_ao, dseq, SGU_CHUNK // dseq)

        hp, kp, vp, _ = _token_stages(hp, None, cos_p, sin_p, attn_prompt, wm_p, bias_p, lw, tb_p)
        hs, ks, vs, vns = _token_stages(hs, None, cos_s, sin_s, attn_sample, wm_s, bias_s, lw, tb_s)
        fin = (_row(g_gate[i]), w_gate[i].astype(BF16), w_ple[i].astype(BF16), _row(g_ple[i]),
               _row(g_final))
        hp = _final(hp, p_prompt[i].reshape(n_p, PLE_DIM), *fin, tb_p)
        hs = _final(hs, p_sample[i].reshape(n_s, PLE_DIM), *fin, tb_s)

        kp_l.append(kp.reshape(bsz, seq, N_KV, HEAD_DIM)[:, seq - WINDOW:])
        vp_l.append(vp.reshape(bsz, seq, N_KV, HEAD_DIM)[:, seq - WINDOW:])
        ks_l.append(jnp.concatenate([cache_k[i][:, dseq:], ks.reshape(dbsz, dseq, N_KV, HEAD_DIM)], axis=1))
        vs_l.append(jnp.concatenate([cache_v[i][:, dseq:], vs.reshape(dbsz, dseq, N_KV, HEAD_DIM)], axis=1))
        gs_l.append(vns.reshape(dbsz, dseq, SGU_HEADS, SGU_GROUP))

    return (hp.reshape(bsz, seq, D_MODEL), hs.reshape(dbsz, dseq, D_MODEL),
            jnp.stack(kp_l), jnp.stack(vp_l), jnp.stack(ks_l), jnp.stack(vs_l), jnp.stack(gs_l))
```

```python
import functools

import jax
import jax.numpy as jnp
from jax import lax
from jax.experimental import pallas as pl
from jax.experimental.pallas import tpu as pltpu

F32 = jnp.float32
BF16 = jnp.bfloat16

D_MODEL = 2048
PAST_LEN = 16384
HEAD_DIM = 64
N_HEADS = 16
N_KV = 2
Q_PER_KV = N_HEADS // N_KV
WINDOW = 128
ROPE_THETA = 10000.0
NEG_INF = -1e30
ATTN_WIDTH = N_HEADS * HEAD_DIM
SGU_CHUNK = 128
SGU_GROUP = 128
SGU_HEADS = 8
SGU_WIDTH = SGU_HEADS * SGU_GROUP
Q_COLS = N_HEADS * HEAD_DIM
KV_COLS = N_KV * HEAD_DIM
IN_COLS = Q_COLS + 2 * KV_COLS + 2 * SGU_WIDTH
PEER_HEADS = 8
PEER_NKEYS = 128
PEER_HALF = 128
PEER_TOPK = 16
PEER_PICKS = PEER_HEADS * PEER_TOPK
PLE_DIM = 256
EPS = 1e-6

LANES = 128
MIB = 1024 * 1024


def _rms(x, g):
    return x * lax.rsqrt(jnp.mean(x * x, axis=-1, keepdims=True) + EPS) * g


def _gelu(x):
    return 0.5 * x * (1.0 + lax.erf(x * (0.5 ** 0.5)))


def _const_spec(shape):
    nd = len(shape)
    return pl.BlockSpec(shape, lambda *_: (0,) * nd, pipeline_mode=pl.Buffered(1))


def _params(vmem_mib, n_grid=1):
    return pltpu.CompilerParams(
        dimension_semantics=("arbitrary",) * n_grid, vmem_limit_bytes=vmem_mib * MIB)


def _in_proj_kernel(x_ref, g_ref, w_ref, cos_ref, sin_ref, gs_ref, bs_ref,
                    q_ref, k_ref, v_ref, u_ref, vn_ref):
    xn = _rms(x_ref[...], g_ref[...]).astype(BF16)
    z = jnp.dot(xn, w_ref[...], preferred_element_type=F32)
    cos = cos_ref[...]
    sin = sin_ref[...]
    lane = lax.broadcasted_iota(jnp.int32, cos.shape, 1)
    first_half = (lane % HEAD_DIM) < (HEAD_DIM // 2)

    def rope(zc):
        swapped = jnp.where(first_half,
                            pltpu.roll(zc, LANES - HEAD_DIM // 2, 1),
                            pltpu.roll(zc, HEAD_DIM // 2, 1))
        return zc * cos + swapped * sin

    for c in range(Q_COLS // LANES):
        q_ref[:, c * LANES:(c + 1) * LANES] = rope(z[:, c * LANES:(c + 1) * LANES])
    k_ref[...] = rope(z[:, Q_COLS:Q_COLS + KV_COLS])
    v_ref[...] = z[:, Q_COLS + KV_COLS:Q_COLS + 2 * KV_COLS]
    u0 = Q_COLS + 2 * KV_COLS
    u_ref[...] = _gelu(z[:, u0:u0 + SGU_WIDTH])
    sv = _gelu(z[:, u0 + SGU_WIDTH:u0 + 2 * SGU_WIDTH])
    mu = jnp.mean(sv, axis=-1, keepdims=True)
    svc = sv - mu
    var = jnp.mean(svc * svc, axis=-1, keepdims=True)
    vn_ref[...] = svc * lax.rsqrt(var + EPS) * gs_ref[...] + bs_ref[...]


def _in_proj(x, g_mix, w_in_bf, cos_t, sin_t, g_sgu, b_sgu, tb):
    n = x.shape[0]
    period_blocks = cos_t.shape[0] // tb
    tok = lambda w: pl.BlockSpec((tb, w), lambda i: (i, 0))
    tab = pl.BlockSpec((tb, LANES), lambda i: (i % period_blocks, 0))
    return pl.pallas_call(
        _in_proj_kernel,
        grid=(n // tb,),
        in_specs=[tok(D_MODEL), _const_spec((1, D_MODEL)), _const_spec((D_MODEL, IN_COLS)),
                  tab, tab, _const_spec((1, SGU_WIDTH)), _const_spec((1, SGU_WIDTH))],
        out_specs=[tok(Q_COLS), tok(KV_COLS), tok(KV_COLS), tok(SGU_WIDTH), tok(SGU_WIDTH)],
        out_shape=[jax.ShapeDtypeStruct((n, w), F32)
                   for w in (Q_COLS, KV_COLS, KV_COLS, SGU_WIDTH, SGU_WIDTH)],
        compiler_params=_params(48),
        name="in_proj",
    )(x, g_mix, w_in_bf, cos_t, sin_t, g_sgu, b_sgu)


def _attn_kernel(sinks_ref, q_ref, ka_ref, kb_ref, va_ref, vb_ref, g_ref, o_ref, acc_ref, *, blocks_per_seq):
    q = q_ref[...]
    kk = jnp.concatenate([ka_ref[...], kb_ref[...]], axis=0).astype(BF16)
    vv = jnp.concatenate([va_ref[...], vb_ref[...]], axis=0).astype(BF16)
    r = lax.broadcasted_iota(jnp.int32, (WINDOW, 2 * WINDOW), 0)
    c = lax.broadcasted_iota(jnp.int32, (WINDOW, 2 * WINDOW), 1)
    first = pl.program_id(0) % blocks_per_seq == 0
    valid = (c > jnp.where(first, jnp.maximum(r, WINDOW - 1), r)) & (c <= r + WINDOW)
    for g in range(N_KV):
        kg = kk[:, g * HEAD_DIM:(g + 1) * HEAD_DIM]
        vg = vv[:, g * HEAD_DIM:(g + 1) * HEAD_DIM]
        for hh in range(Q_PER_KV):
            h = g * Q_PER_KV + hh
            qh = q[:, h * HEAD_DIM:(h + 1) * HEAD_DIM].astype(BF16)
            s = lax.dot_general(qh, kg, (((1,), (1,)), ((), ())),
                                preferred_element_type=F32) * (HEAD_DIM ** -0.5)
            s = jnp.where(valid, s, NEG_INF)
            sink = sinks_ref[h]
            m = jnp.maximum(jnp.max(s, axis=1, keepdims=True), sink)
            p = jnp.exp(s - m)
            denom = jnp.sum(p, axis=1, keepdims=True) + jnp.exp(sink - m)
            p = (p / denom).astype(BF16)
            acc_ref[:, h * HEAD_DIM:(h + 1) * HEAD_DIM] = jnp.dot(p, vg, preferred_element_type=F32)
    o_ref[...] = _rms(acc_ref[...], g_ref[...]).astype(o_ref.dtype)


def _attention(sinks, q, k, v, g_attn_out, blocks_per_seq):
    n = q.shape[0]
    prev = pl.BlockSpec((WINDOW, KV_COLS), lambda j: (jnp.where(j % blocks_per_seq == 0, j, j - 1), 0))
    cur = pl.BlockSpec((WINDOW, KV_COLS), lambda j: (j, 0))
    kern = functools.partial(_attn_kernel, blocks_per_seq=blocks_per_seq)
    return pl.pallas_call(
        kern,
        grid=(n // WINDOW,),
        in_specs=[pl.BlockSpec(memory_space=pltpu.SMEM),
                  pl.BlockSpec((WINDOW, Q_COLS), lambda i: (i, 0)),
                  prev, cur, prev, cur,
                  _const_spec((1, ATTN_WIDTH))],
        out_specs=pl.BlockSpec((WINDOW, ATTN_WIDTH), lambda i: (i, 0)),
        out_shape=jax.ShapeDtypeStruct((n, ATTN_WIDTH), BF16),
        scratch_shapes=[pltpu.VMEM((WINDOW, ATTN_WIDTH), F32)],
        compiler_params=_params(32),
        name="attention",
    )(sinks, q, k, k, v, v, g_attn_out)


def _attn_sample_kernel(sinks_ref, q_ref, ck_ref, cv_ref, kn_ref, vn_ref, g_ref, o_ref, acc_ref, *, dseq, seqs):
    rows = Q_PER_KV * dseq
    r = lax.broadcasted_iota(jnp.int32, (rows, WINDOW), 0) % dseq
    c = lax.broadcasted_iota(jnp.int32, (rows, WINDOW), 1)
    valid_cache = c > r
    valid_new = c <= r
    head_of_row = lax.broadcasted_iota(jnp.int32, (rows, 1), 0) // dseq
    sink_cols = []
    for g in range(N_KV):
        col = jnp.zeros((rows, 1), F32)
        for hh in range(Q_PER_KV):
            col = jnp.where(head_of_row == hh, sinks_ref[g * Q_PER_KV + hh], col)
        sink_cols.append(col)
    pad = jnp.zeros((WINDOW - dseq, HEAD_DIM), F32)
    nt = (((1,), (1,)), ((), ()))
    for i in range(seqs):
        new = slice(i * dseq, (i + 1) * dseq)
        old = slice(i * WINDOW, (i + 1) * WINDOW)
        q = q_ref[new, :]
        for g in range(N_KV):
            cols = slice(g * HEAD_DIM, (g + 1) * HEAD_DIM)
            qg = jnp.concatenate([q[:, (g * Q_PER_KV + hh) * HEAD_DIM:(g * Q_PER_KV + hh + 1) * HEAD_DIM]
                                  for hh in range(Q_PER_KV)], axis=0).astype(BF16)
            k_new = jnp.concatenate([kn_ref[new, cols], pad], axis=0).astype(BF16)
            v_new = jnp.concatenate([vn_ref[new, cols], pad], axis=0).astype(BF16)
            s_old = lax.dot_general(qg, ck_ref[old, cols].astype(BF16), nt,
                                    preferred_element_type=F32) * (HEAD_DIM ** -0.5)
            s_new = lax.dot_general(qg, k_new, nt, preferred_element_type=F32) * (HEAD_DIM ** -0.5)
            s_old = jnp.where(valid_cache, s_old, NEG_INF)
            s_new = jnp.where(valid_new, s_new, NEG_INF)
            sink = sink_cols[g]
            m = jnp.maximum(jnp.maximum(jnp.max(s_old, axis=1, keepdims=True),
                                        jnp.max(s_new, axis=1, keepdims=True)), sink)
            p_old = jnp.exp(s_old - m)
            p_new = jnp.exp(s_new - m)
            denom = (jnp.sum(p_old, axis=1, keepdims=True) + jnp.sum(p_new, axis=1, keepdims=True)
                     + jnp.exp(sink - m))
            o = (jnp.dot((p_old / denom).astype(BF16), cv_ref[old, cols].astype(BF16),
                         preferred_element_type=F32)
                 + jnp.dot((p_new / denom).astype(BF16), v_new, preferred_element_type=F32))
            for hh in range(Q_PER_KV):
                h = g * Q_PER_KV + hh
                acc_ref[new, h * HEAD_DIM:(h + 1) * HEAD_DIM] = o[hh * dseq:(hh + 1) * dseq, :]
    o_ref[...] = _rms(acc_ref[...], g_ref[...]).astype(o_ref.dtype)


def _attention_sample(sinks, q, cache_k, cache_v, k_new, v_new, g_attn_out, dseq, seqs):
    n = q.shape[0]
    tok = lambda w: pl.BlockSpec((seqs * dseq, w), lambda i: (i, 0))
    cache = pl.BlockSpec((seqs * WINDOW, KV_COLS), lambda i: (i, 0))
    kern = functools.partial(_attn_sample_kernel, dseq=dseq, seqs=seqs)
    return pl.pallas_call(
        kern,
        grid=(n // (seqs * dseq),),
        in_specs=[pl.BlockSpec(memory_space=pltpu.SMEM), tok(Q_COLS), cache, cache, tok(KV_COLS), tok(KV_COLS),
                  _const_spec((1, ATTN_WIDTH))],
        out_specs=tok(ATTN_WIDTH),
        out_shape=jax.ShapeDtypeStruct((n, ATTN_WIDTH), BF16),
        scratch_shapes=[pltpu.VMEM((seqs * dseq, ATTN_WIDTH), F32)],
        compiler_params=_params(32),
        name="attention_sample",
    )(sinks, q, cache_k, cache_v, k_new, v_new, g_attn_out)


def _sgu_out_kernel(u_ref, vn_ref, wm_ref, bias_ref, g_ref, an_ref, wo_ref, x_ref, h_ref, s_ref):
    tb = u_ref.shape[0]
    for ch in range(tb // SGU_CHUNK):
        rows = slice(ch * SGU_CHUNK, (ch + 1) * SGU_CHUNK)
        for g in range(SGU_HEADS):
            cols = slice(g * SGU_GROUP, (g + 1) * SGU_GROUP)
            mix = jnp.dot(wm_ref[g], vn_ref[rows, cols].astype(BF16),
                          preferred_element_type=F32) + bias_ref[g]
            s_ref[rows, cols] = u_ref[rows, cols] * mix
    sn = _rms(s_ref[...], g_ref[...]).astype(BF16)
    mixed = jnp.dot(an_ref[...], wo_ref[:ATTN_WIDTH, :], preferred_element_type=F32)
    mixed = mixed + jnp.dot(sn, wo_ref[ATTN_WIDTH:, :], preferred_element_type=F32)
    h_ref[...] = x_ref[...] + mixed


def _sgu_out(u, vn, wm_bf, bias, g_sgu_out, an, w_out_bf, x, tb):
    n = x.shape[0]
    tok = lambda w: pl.BlockSpec((tb, w), lambda i: (i, 0))
    return pl.pallas_call(
        _sgu_out_kernel,
        grid=(n // tb,),
        in_specs=[tok(SGU_WIDTH), tok(SGU_WIDTH),
                  _const_spec((SGU_HEADS, SGU_CHUNK, SGU_CHUNK)),
                  _const_spec((SGU_HEADS, SGU_CHUNK, SGU_GROUP)),
                  _const_spec((1, SGU_WIDTH)), tok(ATTN_WIDTH),
                  _const_spec((D_MODEL, D_MODEL)), tok(D_MODEL)],
        out_specs=tok(D_MODEL),
        out_shape=jax.ShapeDtypeStruct((n, D_MODEL), F32),
        scratch_shapes=[pltpu.VMEM((tb, SGU_WIDTH), F32)],
        compiler_params=_params(48),
        name="sgu_out",
    )(u, vn, wm_bf, bias, g_sgu_out, an, w_out_bf, x)


def _odd_even_merge_sort_pairs(n):
    pairs = []

    def merge(lo, length, r):
        step = 2 * r
        if step < length:
            merge(lo, length, step)
            merge(lo + r, length, step)
            pairs.extend((i, i + r) for i in range(lo + r, lo + length - r, step))
        else:
            pairs.append((lo, lo + r))

    def sort(lo, length):
        if length > 1:
            half = length // 2
            sort(lo, half)
            sort(lo + half, half)
            merge(lo, length, 1)

    sort(0, n)
    return pairs


_BIG_ROW = 2 ** 30


def _top16_rows(s, tag=None, tag_bits=0):
    depth = s.shape[0] // SUBLANES
    row8 = lax.broadcasted_iota(jnp.int32, (SUBLANES, s.shape[1]), 0)
    vals = [s[d * SUBLANES:(d + 1) * SUBLANES, :] for d in range(depth)]
    pays = [(row8 + d * SUBLANES) << tag_bits for d in range(depth)]
    if tag is not None:
        pays = [p | tag[d * SUBLANES:(d + 1) * SUBLANES, :] for d, p in enumerate(pays)]
    for i, j in _odd_even_merge_sort_pairs(depth):
        a, b, pa, pb = vals[i], vals[j], pays[i], pays[j]
        swap = (b > a) | ((b == a) & (pb < pa))
        vals[i], vals[j] = jnp.where(swap, b, a), jnp.where(swap, a, b)
        pays[i], pays[j] = jnp.where(swap, pb, pa), jnp.where(swap, pa, pb)
    out_v, out_p = [], []
    for k in range(PEER_TOPK):
        head, head_p = vals[0], pays[0]
        m = jnp.max(head, axis=0, keepdims=True)
        p = jnp.min(jnp.where(head == m, head_p, _BIG_ROW), axis=0, keepdims=True)
        out_v.append(m)
        out_p.append(p)
        popped = head_p == p
        for d in range(min(depth, PEER_TOPK - 1 - k)):
            below_v, below_p = (vals[d + 1], pays[d + 1]) if d + 1 < depth else (-jnp.inf, _BIG_ROW)
            vals[d] = jnp.where(popped, below_v, vals[d])
            pays[d] = jnp.where(popped, below_p, pays[d])
    return jnp.concatenate(out_v, axis=0), jnp.concatenate(out_p, axis=0)


_CAND_WIDTHS = (16, 8, 8, 8, 4, 4, 4, 4)


def _staircase(a, b, combine):
    rows = [combine(a[i:i + 1, :], b[:w, :]) for i, w in enumerate(_CAND_WIDTHS)]
    rows.append(combine(a[len(_CAND_WIDTHS):, :], b[0:1, :]))
    return jnp.concatenate(rows, axis=0)


def _peer_topk_kernel(h_ref, g_ref, wq_ref, k1_ref, k2_ref, xn_ref, idx_ref, gate_ref):
    xn = _rms(h_ref[...], g_ref[...])
    xn_ref[...] = xn
    qp = jnp.dot(xn.astype(BF16), wq_ref[...], preferred_element_type=F32).astype(BF16)
    idx_parts, gate_parts = [], []
    nt = (((1,), (1,)), ((), ()))
    for h in range(PEER_HEADS):
        base = h * 2 * PEER_HALF
        s1 = lax.dot_general(k1_ref[h], qp[:, base:base + PEER_HALF], nt,
                             preferred_element_type=F32)
        s2 = lax.dot_general(k2_ref[h], qp[:, base + PEER_HALF:base + 2 * PEER_HALF], nt,
                             preferred_element_type=F32)
        t1, i1 = _top16_rows(s1)
        t2, i2 = _top16_rows(s2)
        cand = _staircase(t1, t2, lambda a, b: a + b)
        expert = _staircase(i1, i2, lambda a, b: a * PEER_NKEYS + b)
        expert_bits = (PEER_NKEYS * PEER_NKEYS - 1).bit_length()
        sc, code = _top16_rows(cand, expert, expert_bits)
        idx_parts.append(code & ((1 << expert_bits) - 1))
        ex = jnp.exp(sc - sc[0:1, :])
        gate_parts.append(ex / jnp.sum(ex, axis=0, keepdims=True))
    idx_t = jnp.concatenate(idx_parts, axis=0)
    gate_t = jnp.concatenate(gate_parts, axis=0)
    tb = h_ref.shape[0]
    for c in range(tb // LANES):
        idx_ref[c * LANES:(c + 1) * LANES, :] = idx_t[:, c * LANES:(c + 1) * LANES].T
        gate_ref[c * LANES:(c + 1) * LANES, :] = gate_t[:, c * LANES:(c + 1) * LANES].T


def _peer_topk(h, g_ffn, w_pq_bf, k1_bf, k2_bf, tb):
    n = h.shape[0]
    tok = lambda w: pl.BlockSpec((tb, w), lambda i: (i, 0))
    return pl.pallas_call(
        _peer_topk_kernel,
        grid=(n // tb,),
        in_specs=[tok(D_MODEL), _const_spec((1, D_MODEL)),
                  _const_spec((D_MODEL, PEER_HEADS * 2 * PEER_HALF)),
                  _const_spec((PEER_HEADS, PEER_NKEYS, PEER_HALF)),
                  _const_spec((PEER_HEADS, PEER_NKEYS, PEER_HALF))],
        out_specs=[tok(D_MODEL), tok(PEER_PICKS), tok(PEER_PICKS)],
        out_shape=[jax.ShapeDtypeStruct((n, D_MODEL), F32),
                   jax.ShapeDtypeStruct((n, PEER_PICKS), jnp.int32),
                   jax.ShapeDtypeStruct((n, PEER_PICKS), F32)],
        compiler_params=_params(48),
        name="peer_topk",
    )(h, g_ffn, w_pq_bf, k1_bf, k2_bf)


SUBLANES = 8
PEER_RING = 8
PEER_CHUNKS = D_MODEL // LANES
PEER_EGROUPS = PEER_PICKS // SUBLANES
U_MASK = 0xFFFF0000


def _pack_uv(u_tab, v_tab):
    hi = lax.bitcast_convert_type(u_tab.astype(BF16), jnp.uint16).astype(jnp.uint32) << 16
    lo = lax.bitcast_convert_type(v_tab.astype(BF16), jnp.uint16).astype(jnp.uint32)
    return (hi | lo).reshape(u_tab.shape[0], PEER_CHUNKS, LANES)


def _peer_expert_kernel(idx_ref, nidx_ref, xn_ref, gate_ref, h_ref, tab_hbm, o_ref, *scratch, tokens):
    bufs, (hk_rep, hk_scr, sem) = scratch[:PEER_RING], scratch[PEER_RING:]

    def row_copy(slot, e, r):
        return pltpu.make_async_copy(tab_hbm.at[r], bufs[slot].at[e], sem.at[slot])

    def issue(src_idx_ref, t, slot):
        for e in range(PEER_PICKS):
            row_copy(slot, e, src_idx_ref[t, e]).start(priority=e % 2)

    def wait(slot):
        for e in range(PEER_PICKS):
            row_copy(slot, e, 0).wait()

    sub = lax.broadcasted_iota(jnp.int32, (SUBLANES, LANES), 0)
    lane_minus_sub = lax.broadcasted_iota(jnp.int32, (SUBLANES, LANES), 1) - sub

    def fold_sublanes(parts):
        step = SUBLANES // 2
        while step >= 1:
            lower = (sub & step) == 0
            nxt = []
            for j in range(len(parts) // 2):
                a, b = parts[(j // step) * 2 * step + j % step], parts[(j // step) * 2 * step + j % step + step]
                nxt.append(jnp.where(lower, a, pltpu.roll(b, step, 0))
                           + jnp.where(lower, pltpu.roll(a, SUBLANES - step, 0), b))
            parts = nxt
            step //= 2
        return parts[0]

    def u_pass(t, slot):
        row = pl.ds(t, 1)
        x_row = xn_ref[row, :]
        x_lo = jnp.concatenate([x_row[:, c * LANES:(c + 1) * LANES] for c in range(SUBLANES)], axis=0)
        x_hi = jnp.concatenate([x_row[:, c * LANES:(c + 1) * LANES] for c in range(SUBLANES, PEER_CHUNKS)], axis=0)
        a_diag = jnp.zeros((SUBLANES, LANES), F32)
        for g in range(PEER_EGROUPS):
            parts = []
            for j in range(SUBLANES):
                w = bufs[slot][g * SUBLANES + j]
                parts.append(pltpu.bitcast(w[:SUBLANES] & jnp.uint32(U_MASK), F32) * x_lo
                             + pltpu.bitcast(w[SUBLANES:] & jnp.uint32(U_MASK), F32) * x_hi)
            a_g = jnp.sum(fold_sublanes(parts), axis=1, keepdims=True)
            a_diag = jnp.where(lane_minus_sub == g * SUBLANES, a_g, a_diag)
        a_row = jnp.sum(a_diag, axis=0, keepdims=True)
        hk_scr[slot:slot + 1, :] = _gelu(a_row) * gate_ref[row, :]

    def v_pass(t, slot):
        row = pl.ds(t, 1)
        hk_row = jnp.broadcast_to(hk_scr[slot:slot + 1, :], (SUBLANES, LANES))
        for g in range(PEER_EGROUPS):
            hk_rep[g * SUBLANES:(g + 1) * SUBLANES, :] = jnp.broadcast_to(
                jnp.sum(jnp.where(lane_minus_sub == g * SUBLANES, hk_row, 0.0), axis=1, keepdims=True),
                (SUBLANES, LANES))
        n_chains = 4
        lo, hi = [None] * n_chains, [None] * n_chains
        for e in range(PEER_PICKS):
            w = bufs[slot][e]
            hk_e = jnp.broadcast_to(hk_rep[e:e + 1, :], (SUBLANES, LANES))
            p_lo = pltpu.bitcast(w[:SUBLANES] << 16, F32) * hk_e
            p_hi = pltpu.bitcast(w[SUBLANES:] << 16, F32) * hk_e
            i = e % n_chains
            lo[i] = p_lo if lo[i] is None else lo[i] + p_lo
            hi[i] = p_hi if hi[i] is None else hi[i] + p_hi
        out_lo = (lo[0] + lo[1]) + (lo[2] + lo[3])
        out_hi = (hi[0] + hi[1]) + (hi[2] + hi[3])
        out_row = jnp.concatenate([out_lo[c:c + 1, :] for c in range(SUBLANES)]
                                  + [out_hi[c:c + 1, :] for c in range(SUBLANES)], axis=1)
        o_ref[row, :] = h_ref[row, :] + out_row

    step, last_step = pl.program_id(0), pl.num_programs(0) - 1
    ahead = PEER_RING - 2

    def phase(k, s, do_v):
        wait(s)
        j, slot = k + ahead, (s + ahead) % PEER_RING
        if isinstance(j, int) and j >= tokens:
            @pl.when(step < last_step)
            def _():
                issue(nidx_ref, j - tokens, slot)
        else:
            issue(idx_ref, j, slot)
        u_pass(k, s)
        if do_v:
            v_pass(k - 1, (s - 1) % PEER_RING)

    def static_group(g):
        for s in range(PEER_RING):
            k = g * PEER_RING + s
            phase(k, s, k >= 1)

    def group(g, carry):
        for s in range(PEER_RING):
            phase(g * PEER_RING + s, s, True)
        return carry

    @pl.when(step == 0)
    def _():
        for s in range(ahead):
            issue(idx_ref, s, s)

    n_groups = tokens // PEER_RING
    static_group(0)
    if n_groups > 1:
        lax.fori_loop(1, n_groups - 1, group, 0)
        static_group(n_groups - 1)
    v_pass(tokens - 1, (tokens - 1) % PEER_RING)


def _peer_expert(idx, xn, gate, h, tab, tokens):
    n = h.shape[0]
    assert tokens % PEER_RING == 0 and n % tokens == 0
    tok = lambda w: pl.BlockSpec((tokens, w), lambda i: (i, 0))
    kern = functools.partial(_peer_expert_kernel, tokens=tokens)
    steps = n // tokens
    return pl.pallas_call(
        kern,
        grid=(steps,),
        in_specs=[pl.BlockSpec((tokens, PEER_PICKS), lambda i: (i, 0), memory_space=pltpu.SMEM),
                  pl.BlockSpec((tokens, PEER_PICKS), lambda i: (jnp.minimum(i + 1, steps - 1), 0),
                               memory_space=pltpu.SMEM),
                  tok(D_MODEL), tok(PEER_PICKS), tok(D_MODEL),
                  pl.BlockSpec(memory_space=pl.ANY)],
        out_specs=tok(D_MODEL),
        out_shape=jax.ShapeDtypeStruct((n, D_MODEL), F32),
        scratch_shapes=[pltpu.VMEM((PEER_PICKS, PEER_CHUNKS, LANES), jnp.uint32)
                        for _ in range(PEER_RING)] + [
                        pltpu.VMEM((PEER_PICKS, LANES), F32),
                        pltpu.VMEM((PEER_RING, PEER_PICKS), F32),
                        pltpu.SemaphoreType.DMA((PEER_RING,))],
        compiler_params=_params(40),
        name="peer_expert",
    )(idx, idx, xn, gate, h, tab)


def _final_kernel(h_ref, p_ref, gg_ref, wg_ref, wp_ref, gp_ref, gf_ref, y_ref):
    h = h_ref[...]
    gate = jax.nn.sigmoid(jnp.dot(_rms(h, gg_ref[...]).astype(BF16), wg_ref[...],
                                  preferred_element_type=F32))
    ple = jnp.dot(p_ref[...].astype(BF16), wp_ref[...], preferred_element_type=F32)
    h = h + _rms(ple, gp_ref[...]) * gate
    y_ref[...] = _rms(h, gf_ref[...])


def _final(h, p, g_gate, w_gate_bf, w_ple_bf, g_ple, g_final, tb):
    n = h.shape[0]
    tok = lambda w: pl.BlockSpec((tb, w), lambda i: (i, 0))
    return pl.pallas_call(
        _final_kernel,
        grid=(n // tb,),
        in_specs=[tok(D_MODEL), tok(PLE_DIM), _const_spec((1, D_MODEL)),
                  _const_spec((D_MODEL, D_MODEL)), _const_spec((PLE_DIM, D_MODEL)),
                  _const_spec((1, D_MODEL)), _const_spec((1, D_MODEL))],
        out_specs=tok(D_MODEL),
        out_shape=jax.ShapeDtypeStruct((n, D_MODEL), F32),
        compiler_params=_params(48),
        name="final",
    )(h, p, g_gate, w_gate_bf, w_ple_bf, g_ple, g_final)


def _rope_tables(pos):
    half = HEAD_DIM // 2
    inv_freq = 1.0 / (ROPE_THETA ** (jnp.arange(0, HEAD_DIM, 2, dtype=F32) / HEAD_DIM))
    ang = pos.astype(F32)[:, None] * inv_freq[None, :]
    cos, sin = jnp.cos(ang), jnp.sin(ang)
    reps = LANES // HEAD_DIM
    return (jnp.tile(jnp.concatenate([cos, cos], axis=1), (1, reps)),
            jnp.tile(jnp.concatenate([-sin, sin], axis=1), (1, reps)))


def _row(v):
    return v.reshape(1, -1).astype(F32)


def _token_stages(x, p, cos_t, sin_t, attn_fn, wm_bf, bias, lw, tb):
    q, k, v, u, vn = _in_proj(x, lw["g_mix"], lw["w_in"], cos_t, sin_t, lw["g_sgu"], lw["b_sgu"], tb)
    an = attn_fn(q, k, v)
    h = _sgu_out(u, vn, wm_bf, bias, lw["g_sgu_out"], an, lw["w_out"], x, tb)
    xn, idx, gate = _peer_topk(h, lw["g_ffn"], lw["w_pq"], lw["sub_k1"], lw["sub_k2"], tb)
    h = _peer_expert(idx, xn, gate, h, lw["uv_tab"], min(256, x.shape[0]))
    return h, k, v, vn


def kernel(x_prompt, x_sample, cache_k, cache_v, p_prompt, p_sample, g_mix, w_in, sinks, g_sgu, b_sgu,
           w_s, b_s, g_attn_out, g_sgu_out, w_out, g_ffn, w_pq, sub_k1, sub_k2, u_tab, v_tab, g_gate,
           w_gate, w_ple, g_ple, g_final):
    depth = w_in.shape[0]
    bsz, seq, _ = x_prompt.shape
    dbsz, dseq, _ = x_sample.shape
    wc = cache_k.shape[2]
    assert depth == 1, "the final kernel fuses the layer tail with the final norm"
    assert seq % WINDOW == 0 and wc == WINDOW and WINDOW % dseq == 0 and dseq <= SGU_CHUNK
    n_p, n_s = bsz * seq, dbsz * dseq
    tb_p = 256 if seq % 256 == 0 else WINDOW
    tb_s = WINDOW
    assert n_s % tb_s == 0 and seq % tb_p == 0
    nb = seq // WINDOW

    cos_p, sin_p = _rope_tables(jnp.arange(seq, dtype=jnp.int32))
    cos_s, sin_s = _rope_tables(PAST_LEN + (jnp.arange(tb_s, dtype=jnp.int32) % dseq))
    causal = jnp.tril(jnp.ones((SGU_CHUNK, SGU_CHUNK), F32))
    eye = jnp.eye(SGU_CHUNK // dseq, dtype=F32)

    hp = x_prompt.reshape(n_p, D_MODEL)
    hs = x_sample.reshape(n_s, D_MODEL)
    kp_l, vp_l, ks_l, vs_l, gs_l = [], [], [], [], []
    for i in range(depth):
        lw = dict(
            g_mix=_row(g_mix[i]), w_in=w_in[i].astype(BF16), g_sgu=_row(g_sgu[i]), b_sgu=_row(b_sgu[i]),
            g_sgu_out=_row(g_sgu_out[i]), w_out=w_out[i].astype(BF16), g_ffn=_row(g_ffn[i]),
            w_pq=w_pq[i].astype(BF16), sub_k1=sub_k1[i].astype(BF16), sub_k2=sub_k2[i].astype(BF16),
            uv_tab=_pack_uv(u_tab[i], v_tab[i]))
        sinks_i = sinks[i].astype(F32)
        g_ao = _row(g_attn_out[i])
        ws = w_s[i].astype(F32) * causal
        wm_p = ws.astype(BF16)
        bias_p = jnp.broadcast_to(b_s[i].astype(F32)[:, :, None], (SGU_HEADS, SGU_CHUNK, SGU_GROUP))
        wm_s = jnp.einsum("ab,gts->gatbs", eye, ws[:, :dseq, :dseq]).reshape(
            SGU_HEADS, SGU_CHUNK, SGU_CHUNK).astype(BF16)
        bias_s = jnp.broadcast_to(jnp.tile(b_s[i].astype(F32)[:, :dseq], (1, SGU_CHUNK // dseq))[:, :, None],
                                  (SGU_HEADS, SGU_CHUNK, SGU_GROUP))

        def attn_prompt(q, k, v):
            return _attention(sinks_i, q, k, v, g_ao, nb)

        def attn_sample(q, k, v, ck=cache_k[i], cv=cache_v[i]):
            return _attention_sample(sinks_i, q, ck.reshape(dbsz * wc, KV_COLS), cv.reshape(dbsz * wc, KV_COLS),
                                     k, v, g_ao, dseq, SGU_CHUNK // dseq)

        hp, kp, vp, _ = _token_stages(hp, None, cos_p, sin_p, attn_prompt, wm_p, bias_p, lw, tb_p)
        hs, ks, vs, vns = _token_stages(hs, None, cos_s, sin_s, attn_sample, wm_s, bias_s, lw, tb_s)
        fin = (_row(g_gate[i]), w_gate[i].astype(BF16), w_ple[i].astype(BF16), _row(g_ple[i]),
               _row(g_final))
        hp = _final(hp, p_prompt[i].reshape(n_p, PLE_DIM), *fin, tb_p)
        hs = _final(hs, p_sample[i].reshape(n_s, PLE_DIM), *fin, tb_s)

        kp_l.append(kp.reshape(bsz, seq, N_KV, HEAD_DIM)[:, seq - WINDOW:])
        vp_l.append(vp.reshape(bsz, seq, N_KV, HEAD_DIM)[:, seq - WINDOW:])
        ks_l.append(jnp.concatenate([cache_k[i][:, dseq:], ks.reshape(dbsz, dseq, N_KV, HEAD_DIM)], axis=1))
        vs_l.append(jnp.concatenate([cache_v[i][:, dseq:], vs.reshape(dbsz, dseq, N_KV, HEAD_DIM)], axis=1))
        gs_l.append(vns.reshape(dbsz, dseq, SGU_HEADS, SGU_GROUP))

    return (hp.reshape(bsz, seq, D_MODEL), hs.reshape(dbsz, dseq, D_MODEL),
            jnp.stack(kp_l), jnp.stack(vp_l), jnp.stack(ks_l), jnp.stack(vs_l), jnp.stack(gs_l))
```

```python
import functools

import jax
import jax.numpy as jnp
from jax import lax
from jax.experimental import pallas as pl
from jax.experimental.pallas import tpu as pltpu

F32 = jnp.float32
BF16 = jnp.bfloat16

D_MODEL = 2048
PAST_LEN = 16384
HEAD_DIM = 64
N_HEADS = 16
N_KV = 2
Q_PER_KV = N_HEADS // N_KV
WINDOW = 128
ROPE_THETA = 10000.0
NEG_INF = -1e30
ATTN_WIDTH = N_HEADS * HEAD_DIM
SGU_CHUNK = 128
SGU_GROUP = 128
SGU_HEADS = 8
SGU_WIDTH = SGU_HEADS * SGU_GROUP
Q_COLS = N_HEADS * HEAD_DIM
KV_COLS = N_KV * HEAD_DIM
IN_COLS = Q_COLS + 2 * KV_COLS + 2 * SGU_WIDTH
PEER_HEADS = 8
PEER_NKEYS = 128
PEER_HALF = 128
PEER_TOPK = 16
PEER_PICKS = PEER_HEADS * PEER_TOPK
PLE_DIM = 256
EPS = 1e-6

LANES = 128
SUBLANES = 8
MIB = 1024 * 1024
VMEM_DENSE_MIB = 48
VMEM_SMALL_MIB = 32


def _rms(x, g):
    return x * lax.rsqrt(jnp.mean(x * x, axis=-1, keepdims=True) + EPS) * g


def _gelu(x):
    return 0.5 * x * (1.0 + lax.erf(x * (0.5 ** 0.5)))


def _const_spec(shape):
    nd = len(shape)
    return pl.BlockSpec(shape, lambda *_: (0,) * nd, pipeline_mode=pl.Buffered(1))


def _params(vmem_mib):
    return pltpu.CompilerParams(dimension_semantics=("arbitrary",), vmem_limit_bytes=vmem_mib * MIB)


def _in_proj_kernel(x_ref, g_ref, w_ref, cos_ref, sin_ref, gs_ref, bs_ref,
                    q_ref, k_ref, v_ref, u_ref, vn_ref):
    xn = _rms(x_ref[...], g_ref[...]).astype(BF16)
    z = jnp.dot(xn, w_ref[...], preferred_element_type=F32)
    cos = cos_ref[...]
    sin = sin_ref[...]
    lane = lax.broadcasted_iota(jnp.int32, cos.shape, 1)
    first_half = (lane % HEAD_DIM) < (HEAD_DIM // 2)

    def rope(zc):
        swapped = jnp.where(first_half,
                            pltpu.roll(zc, LANES - HEAD_DIM // 2, 1),
                            pltpu.roll(zc, HEAD_DIM // 2, 1))
        return zc * cos + swapped * sin

    for c in range(Q_COLS // LANES):
        q_ref[:, c * LANES:(c + 1) * LANES] = rope(z[:, c * LANES:(c + 1) * LANES])
    k_ref[...] = rope(z[:, Q_COLS:Q_COLS + KV_COLS])
    v_ref[...] = z[:, Q_COLS + KV_COLS:Q_COLS + 2 * KV_COLS]
    u0 = Q_COLS + 2 * KV_COLS
    u_ref[...] = _gelu(z[:, u0:u0 + SGU_WIDTH])
    sv = _gelu(z[:, u0 + SGU_WIDTH:u0 + 2 * SGU_WIDTH])
    mu = jnp.mean(sv, axis=-1, keepdims=True)
    svc = sv - mu
    var = jnp.mean(svc * svc, axis=-1, keepdims=True)
    vn_ref[...] = svc * lax.rsqrt(var + EPS) * gs_ref[...] + bs_ref[...]


def _in_proj(x, g_mix, w_in_bf, cos_t, sin_t, g_sgu, b_sgu, tb):
    n = x.shape[0]
    period_blocks = cos_t.shape[0] // tb
    tok = lambda w: pl.BlockSpec((tb, w), lambda i: (i, 0))
    tab = pl.BlockSpec((tb, LANES), lambda i: (i % period_blocks, 0))
    return pl.pallas_call(
        _in_proj_kernel,
        grid=(n // tb,),
        in_specs=[tok(D_MODEL), _const_spec((1, D_MODEL)), _const_spec((D_MODEL, IN_COLS)),
                  tab, tab, _const_spec((1, SGU_WIDTH)), _const_spec((1, SGU_WIDTH))],
        out_specs=[tok(Q_COLS), tok(KV_COLS), tok(KV_COLS), tok(SGU_WIDTH), tok(SGU_WIDTH)],
        out_shape=[jax.ShapeDtypeStruct((n, w), F32)
                   for w in (Q_COLS, KV_COLS, KV_COLS, SGU_WIDTH, SGU_WIDTH)],
        compiler_params=_params(VMEM_DENSE_MIB),
        name="in_proj",
    )(x, g_mix, w_in_bf, cos_t, sin_t, g_sgu, b_sgu)


def _attn_kernel(sinks_ref, q_ref, ka_ref, kb_ref, va_ref, vb_ref, g_ref, o_ref, acc_ref, *, blocks_per_seq):
    q = q_ref[...]
    kk_t_f32 = jnp.concatenate([ka_ref[...].T, kb_ref[...].T], axis=1)
    vv = jnp.concatenate([va_ref[...], vb_ref[...]], axis=0).astype(BF16)
    r = lax.broadcasted_iota(jnp.int32, (WINDOW, 2 * WINDOW), 0)
    c = lax.broadcasted_iota(jnp.int32, (WINDOW, 2 * WINDOW), 1)
    first = pl.program_id(0) % blocks_per_seq == 0
    valid = (c > jnp.where(first, jnp.maximum(r, WINDOW - 1), r)) & (c <= r + WINDOW)
    kk_t = kk_t_f32.astype(BF16)
    for g in range(N_KV):
        kg_t = kk_t[g * HEAD_DIM:(g + 1) * HEAD_DIM, :]
        vg = vv[:, g * HEAD_DIM:(g + 1) * HEAD_DIM]
        for hh in range(Q_PER_KV):
            h = g * Q_PER_KV + hh
            qh = q[:, h * HEAD_DIM:(h + 1) * HEAD_DIM].astype(BF16)
            s = jnp.dot(qh, kg_t, preferred_element_type=F32) * (HEAD_DIM ** -0.5)
            s = jnp.where(valid, s, NEG_INF)
            sink = sinks_ref[h]
            m = jnp.maximum(jnp.max(s, axis=1, keepdims=True), sink)
            p = jnp.exp(s - m)
            denom = jnp.sum(p, axis=1, keepdims=True) + jnp.exp(sink - m)
            p = (p / denom).astype(BF16)
            acc_ref[:, h * HEAD_DIM:(h + 1) * HEAD_DIM] = jnp.dot(p, vg, preferred_element_type=F32)
    o_ref[...] = _rms(acc_ref[...], g_ref[...]).astype(o_ref.dtype)


def _attention(sinks, q, k, v, g_attn_out, blocks_per_seq):
    n = q.shape[0]
    prev = pl.BlockSpec((WINDOW, KV_COLS), lambda j: (jnp.where(j % blocks_per_seq == 0, j, j - 1), 0))
    cur = pl.BlockSpec((WINDOW, KV_COLS), lambda j: (j, 0))
    kern = functools.partial(_attn_kernel, blocks_per_seq=blocks_per_seq)
    return pl.pallas_call(
        kern,
        grid=(n // WINDOW,),
        in_specs=[pl.BlockSpec(memory_space=pltpu.SMEM),
                  pl.BlockSpec((WINDOW, Q_COLS), lambda i: (i, 0)),
                  prev, cur, prev, cur,
                  _const_spec((1, ATTN_WIDTH))],
        out_specs=pl.BlockSpec((WINDOW, ATTN_WIDTH), lambda i: (i, 0)),
        out_shape=jax.ShapeDtypeStruct((n, ATTN_WIDTH), BF16),
        scratch_shapes=[pltpu.VMEM((WINDOW, ATTN_WIDTH), F32)],
        compiler_params=_params(VMEM_SMALL_MIB),
        name="attention",
    )(sinks, q, k, k, v, v, g_attn_out)


def _attn_sample_kernel(sinks_ref, q_ref, ck_ref, cv_ref, kn_ref, vn_ref, g_ref, o_ref, acc_ref, *, dseq, seqs):
    rows = Q_PER_KV * dseq
    r = lax.broadcasted_iota(jnp.int32, (rows, WINDOW), 0) % dseq
    c = lax.broadcasted_iota(jnp.int32, (rows, WINDOW), 1)
    valid_cache = c > r
    valid_new = c <= r
    head_of_row = lax.broadcasted_iota(jnp.int32, (rows, 1), 0) // dseq
    sink_cols = []
    for g in range(N_KV):
        col = jnp.zeros((rows, 1), F32)
        for hh in range(Q_PER_KV):
            col = jnp.where(head_of_row == hh, sinks_ref[g * Q_PER_KV + hh], col)
        sink_cols.append(col)
    pad = jnp.zeros((WINDOW - dseq, HEAD_DIM), F32)
    nt = (((1,), (1,)), ((), ()))
    for i in range(seqs):
        new = slice(i * dseq, (i + 1) * dseq)
        old = slice(i * WINDOW, (i + 1) * WINDOW)
        q = q_ref[new, :]
        for g in range(N_KV):
            cols = slice(g * HEAD_DIM, (g + 1) * HEAD_DIM)
            qg = jnp.concatenate([q[:, (g * Q_PER_KV + hh) * HEAD_DIM:(g * Q_PER_KV + hh + 1) * HEAD_DIM]
                                  for hh in range(Q_PER_KV)], axis=0).astype(BF16)
            k_new = jnp.concatenate([kn_ref[new, cols], pad], axis=0).astype(BF16)
            v_new = jnp.concatenate([vn_ref[new, cols], pad], axis=0).astype(BF16)
            s_old = lax.dot_general(qg, ck_ref[old, cols].astype(BF16), nt,
                                    preferred_element_type=F32) * (HEAD_DIM ** -0.5)
            s_new = lax.dot_general(qg, k_new, nt, preferred_element_type=F32) * (HEAD_DIM ** -0.5)
            s_old = jnp.where(valid_cache, s_old, NEG_INF)
            s_new = jnp.where(valid_new, s_new, NEG_INF)
            sink = sink_cols[g]
            m = jnp.maximum(jnp.maximum(jnp.max(s_old, axis=1, keepdims=True),
                                        jnp.max(s_new, axis=1, keepdims=True)), sink)
            p_old = jnp.exp(s_old - m)
            p_new = jnp.exp(s_new - m)
            denom = (jnp.sum(p_old, axis=1, keepdims=True) + jnp.sum(p_new, axis=1, keepdims=True)
                     + jnp.exp(sink - m))
            o = (jnp.dot((p_old / denom).astype(BF16), cv_ref[old, cols].astype(BF16),
                         preferred_element_type=F32)
                 + jnp.dot((p_new / denom).astype(BF16), v_new, preferred_element_type=F32))
            for hh in range(Q_PER_KV):
                h = g * Q_PER_KV + hh
                acc_ref[new, h * HEAD_DIM:(h + 1) * HEAD_DIM] = o[hh * dseq:(hh + 1) * dseq, :]
    o_ref[...] = _rms(acc_ref[...], g_ref[...]).astype(o_ref.dtype)


def _attention_sample(sinks, q, cache_k, cache_v, k_new, v_new, g_attn_out, dseq, seqs):
    n = q.shape[0]
    tok = lambda w: pl.BlockSpec((seqs * dseq, w), lambda i: (i, 0))
    cache = pl.BlockSpec((seqs * WINDOW, KV_COLS), lambda i: (i, 0))
    kern = functools.partial(_attn_sample_kernel, dseq=dseq, seqs=seqs)
    return pl.pallas_call(
        kern,
        grid=(n // (seqs * dseq),),
        in_specs=[pl.BlockSpec(memory_space=pltpu.SMEM), tok(Q_COLS), cache, cache, tok(KV_COLS), tok(KV_COLS),
                  _const_spec((1, ATTN_WIDTH))],
        out_specs=tok(ATTN_WIDTH),
        out_shape=jax.ShapeDtypeStruct((n, ATTN_WIDTH), BF16),
        scratch_shapes=[pltpu.VMEM((seqs * dseq, ATTN_WIDTH), F32)],
        compiler_params=_params(VMEM_SMALL_MIB),
        name="attention_sample",
    )(sinks, q, cache_k, cache_v, k_new, v_new, g_attn_out)


def _sgu_out_kernel(u_ref, vn_ref, wm_ref, bias_ref, g_ref, an_ref, wo_ref, x_ref, h_ref, s_ref):
    tb = u_ref.shape[0]
    for ch in range(tb // SGU_CHUNK):
        rows = slice(ch * SGU_CHUNK, (ch + 1) * SGU_CHUNK)
        for g in range(SGU_HEADS):
            cols = slice(g * SGU_GROUP, (g + 1) * SGU_GROUP)
            mix = jnp.dot(wm_ref[g], vn_ref[rows, cols].astype(BF16),
                          preferred_element_type=F32) + bias_ref[g]
            s_ref[rows, cols] = u_ref[rows, cols] * mix
    sn = _rms(s_ref[...], g_ref[...]).astype(BF16)
    mixed = jnp.dot(an_ref[...], wo_ref[:ATTN_WIDTH, :], preferred_element_type=F32)
    mixed = mixed + jnp.dot(sn, wo_ref[ATTN_WIDTH:, :], preferred_element_type=F32)
    h_ref[...] = x_ref[...] + mixed


def _sgu_out(u, vn, wm_bf, bias, g_sgu_out, an, w_out_bf, x, tb):
    n = x.shape[0]
    tok = lambda w: pl.BlockSpec((tb, w), lambda i: (i, 0))
    return pl.pallas_call(
        _sgu_out_kernel,
        grid=(n // tb,),
        in_specs=[tok(SGU_WIDTH), tok(SGU_WIDTH),
                  _const_spec((SGU_HEADS, SGU_CHUNK, SGU_CHUNK)),
                  _const_spec((SGU_HEADS, SGU_CHUNK, SGU_GROUP)),
                  _const_spec((1, SGU_WIDTH)), tok(ATTN_WIDTH),
                  _const_spec((D_MODEL, D_MODEL)), tok(D_MODEL)],
        out_specs=tok(D_MODEL),
        out_shape=jax.ShapeDtypeStruct((n, D_MODEL), F32),
        scratch_shapes=[pltpu.VMEM((tb, SGU_WIDTH), F32)],
        compiler_params=_params(VMEM_DENSE_MIB),
        name="sgu_out",
    )(u, vn, wm_bf, bias, g_sgu_out, an, w_out_bf, x)


def _odd_even_merge_sort_pairs(n):
    pairs = []

    def merge(lo, length, r):
        step = 2 * r
        if step < length:
            merge(lo, length, step)
            merge(lo + r, length, step)
            pairs.extend((i, i + r) for i in range(lo + r, lo + length - r, step))
        else:
            pairs.append((lo, lo + r))

    def sort(lo, length):
        if length > 1:
            half = length // 2
            sort(lo, half)
            sort(lo + half, half)
            merge(lo, length, 1)

    sort(0, n)
    return pairs


_BIG_ROW = 2 ** 30


def _top16_rows(s, tag=None, tag_bits=0):
    depth = s.shape[0] // SUBLANES
    row8 = lax.broadcasted_iota(jnp.int32, (SUBLANES, s.shape[1]), 0)
    vals = [s[d * SUBLANES:(d + 1) * SUBLANES, :] for d in range(depth)]
    pays = [(row8 + d * SUBLANES) << tag_bits for d in range(depth)]
    if tag is not None:
        pays = [p | tag[d * SUBLANES:(d + 1) * SUBLANES, :] for d, p in enumerate(pays)]
    for i, j in _odd_even_merge_sort_pairs(depth):
        a, b, pa, pb = vals[i], vals[j], pays[i], pays[j]
        swap = (b > a) | ((b == a) & (pb < pa))
        vals[i], vals[j] = jnp.where(swap, b, a), jnp.where(swap, a, b)
        pays[i], pays[j] = jnp.where(swap, pb, pa), jnp.where(swap, pa, pb)
    out_v, out_p = [], []
    for k in range(PEER_TOPK):
        head, head_p = vals[0], pays[0]
        m = jnp.max(head, axis=0, keepdims=True)
        p = jnp.min(jnp.where(head == m, head_p, _BIG_ROW), axis=0, keepdims=True)
        out_v.append(m)
        out_p.append(p)
        popped = head_p == p
        for d in range(min(depth, PEER_TOPK - 1 - k)):
            below_v, below_p = (vals[d + 1], pays[d + 1]) if d + 1 < depth else (-jnp.inf, _BIG_ROW)
            vals[d] = jnp.where(popped, below_v, vals[d])
            pays[d] = jnp.where(popped, below_p, pays[d])
    return jnp.concatenate(out_v, axis=0), jnp.concatenate(out_p, axis=0)


_CAND_WIDTHS = (16, 8, 8, 8, 4, 4, 4, 4)


def _staircase(a, b, combine):
    rows = [combine(a[i:i + 1, :], b[:w, :]) for i, w in enumerate(_CAND_WIDTHS)]
    rows.append(combine(a[len(_CAND_WIDTHS):, :], b[0:1, :]))
    return jnp.concatenate(rows, axis=0)


def _peer_topk_kernel(h_ref, g_ref, wq_ref, k1_ref, k2_ref, xn_ref, idx_ref, gate_ref):
    xn = _rms(h_ref[...], g_ref[...])
    xn_ref[...] = xn
    qp = jnp.dot(xn.astype(BF16), wq_ref[...], preferred_element_type=F32).astype(BF16)
    idx_parts, gate_parts = [], []
    nt = (((1,), (1,)), ((), ()))
    for h in range(PEER_HEADS):
        base = h * 2 * PEER_HALF
        s1 = lax.dot_general(k1_ref[h], qp[:, base:base + PEER_HALF], nt,
                             preferred_element_type=F32)
        s2 = lax.dot_general(k2_ref[h], qp[:, base + PEER_HALF:base + 2 * PEER_HALF], nt,
                             preferred_element_type=F32)
        t1, i1 = _top16_rows(s1)
        t2, i2 = _top16_rows(s2)
        cand = _staircase(t1, t2, lambda a, b: a + b)
        expert = _staircase(i1, i2, lambda a, b: a * PEER_NKEYS + b)
        expert_bits = (PEER_NKEYS * PEER_NKEYS - 1).bit_length()
        sc, code = _top16_rows(cand, expert, expert_bits)
        idx_parts.append(code & ((1 << expert_bits) - 1))
        ex = jnp.exp(sc - sc[0:1, :])
        gate_parts.append(ex / jnp.sum(ex, axis=0, keepdims=True))
    idx_t = jnp.concatenate(idx_parts, axis=0)
    gate_t = jnp.concatenate(gate_parts, axis=0)
    tb = h_ref.shape[0]
    for c in range(tb // LANES):
        idx_ref[c * LANES:(c + 1) * LANES, :] = idx_t[:, c * LANES:(c + 1) * LANES].T
        gate_ref[c * LANES:(c + 1) * LANES, :] = gate_t[:, c * LANES:(c + 1) * LANES].T


def _peer_topk(h, g_ffn, w_pq_bf, k1_bf, k2_bf, tb):
    n = h.shape[0]
    tok = lambda w: pl.BlockSpec((tb, w), lambda i: (i, 0))
    return pl.pallas_call(
        _peer_topk_kernel,
        grid=(n // tb,),
        in_specs=[tok(D_MODEL), _const_spec((1, D_MODEL)),
                  _const_spec((D_MODEL, PEER_HEADS * 2 * PEER_HALF)),
                  _const_spec((PEER_HEADS, PEER_NKEYS, PEER_HALF)),
                  _const_spec((PEER_HEADS, PEER_NKEYS, PEER_HALF))],
        out_specs=[tok(D_MODEL), tok(PEER_PICKS), tok(PEER_PICKS)],
        out_shape=[jax.ShapeDtypeStruct((n, D_MODEL), F32),
                   jax.ShapeDtypeStruct((n, PEER_PICKS), jnp.int32),
                   jax.ShapeDtypeStruct((n, PEER_PICKS), F32)],
        compiler_params=_params(VMEM_DENSE_MIB),
        name="peer_topk",
    )(h, g_ffn, w_pq_bf, k1_bf, k2_bf)


PEER_RING = 8
PEER_CHUNKS = D_MODEL // LANES
PEER_EGROUPS = PEER_PICKS // SUBLANES
U_MASK = 0xFFFF0000


def _pack_uv(u_tab, v_tab):
    hi = lax.bitcast_convert_type(u_tab.astype(BF16), jnp.uint16).astype(jnp.uint32) << 16
    lo = lax.bitcast_convert_type(v_tab.astype(BF16), jnp.uint16).astype(jnp.uint32)
    return (hi | lo).reshape(u_tab.shape[0], PEER_CHUNKS, LANES)


def _peer_expert_kernel(idx_ref, nidx_ref, xn_ref, gate_ref, h_ref, tab_hbm, o_ref, *scratch, tokens):
    bufs, (hk_rep, hk_scr, sem) = scratch[:PEER_RING], scratch[PEER_RING:]

    def row_copy(slot, e, r):
        return pltpu.make_async_copy(tab_hbm.at[r], bufs[slot].at[e], sem.at[slot])

    def issue(src_idx_ref, t, slot):
        for e in range(PEER_PICKS):
            row_copy(slot, e, src_idx_ref[t, e]).start(priority=e % 2)

    def wait(slot):
        for e in range(PEER_PICKS):
            row_copy(slot, e, 0).wait()

    sub = lax.broadcasted_iota(jnp.int32, (SUBLANES, LANES), 0)
    lane_minus_sub = lax.broadcasted_iota(jnp.int32, (SUBLANES, LANES), 1) - sub

    def fold_sublanes(parts):
        step = SUBLANES // 2
        while step >= 1:
            lower = (sub & step) == 0
            nxt = []
            for j in range(len(parts) // 2):
                a, b = parts[(j // step) * 2 * step + j % step], parts[(j // step) * 2 * step + j % step + step]
                nxt.append(jnp.where(lower, a, pltpu.roll(b, step, 0))
                           + jnp.where(lower, pltpu.roll(a, SUBLANES - step, 0), b))
            parts = nxt
            step //= 2
        return parts[0]

    def u_pass(t, slot):
        row = pl.ds(t, 1)
        x_row = xn_ref[row, :]
        x_lo = jnp.concatenate([x_row[:, c * LANES:(c + 1) * LANES] for c in range(SUBLANES)], axis=0)
        x_hi = jnp.concatenate([x_row[:, c * LANES:(c + 1) * LANES] for c in range(SUBLANES, PEER_CHUNKS)], axis=0)
        a_diag = jnp.zeros((SUBLANES, LANES), F32)
        for g in range(PEER_EGROUPS):
            parts = []
            for j in range(SUBLANES):
                w = bufs[slot][g * SUBLANES + j]
                parts.append(pltpu.bitcast(w[:SUBLANES] & jnp.uint32(U_MASK), F32) * x_lo
                             + pltpu.bitcast(w[SUBLANES:] & jnp.uint32(U_MASK), F32) * x_hi)
            a_g = jnp.sum(fold_sublanes(parts), axis=1, keepdims=True)
            a_diag = jnp.where(lane_minus_sub == g * SUBLANES, a_g, a_diag)
        a_row = jnp.sum(a_diag, axis=0, keepdims=True)
        hk_scr[slot:slot + 1, :] = _gelu(a_row) * gate_ref[row, :]

    def v_pass(t, slot):
        row = pl.ds(t, 1)
        hk_row = jnp.broadcast_to(hk_scr[slot:slot + 1, :], (SUBLANES, LANES))
        for g in range(PEER_EGROUPS):
            hk_rep[g * SUBLANES:(g + 1) * SUBLANES, :] = jnp.broadcast_to(
                jnp.sum(jnp.where(lane_minus_sub == g * SUBLANES, hk_row, 0.0), axis=1, keepdims=True),
                (SUBLANES, LANES))
        n_chains = 4
        lo, hi = [None] * n_chains, [None] * n_chains
        for e in range(PEER_PICKS):
            w = bufs[slot][e]
            hk_e = jnp.broadcast_to(hk_rep[e:e + 1, :], (SUBLANES, LANES))
            p_lo = pltpu.bitcast(w[:SUBLANES] << 16, F32) * hk_e
            p_hi = pltpu.bitcast(w[SUBLANES:] << 16, F32) * hk_e
            i = e % n_chains
            lo[i] = p_lo if lo[i] is None else lo[i] + p_lo
            hi[i] = p_hi if hi[i] is None else hi[i] + p_hi
        out_lo = (lo[0] + lo[1]) + (lo[2] + lo[3])
        out_hi = (hi[0] + hi[1]) + (hi[2] + hi[3])
        out_row = jnp.concatenate([out_lo[c:c + 1, :] for c in range(SUBLANES)]
                                  + [out_hi[c:c + 1, :] for c in range(SUBLANES)], axis=1)
        o_ref[row, :] = h_ref[row, :] + out_row

    step, last_step = pl.program_id(0), pl.num_programs(0) - 1
    ahead = PEER_RING - 2

    def phase(k, s, do_v):
        wait(s)
        j, slot = k + ahead, (s + ahead) % PEER_RING
        if isinstance(j, int) and j >= tokens:
            @pl.when(step < last_step)
            def _():
                issue(nidx_ref, j - tokens, slot)
        else:
            issue(idx_ref, j, slot)
        u_pass(k, s)
        if do_v:
            v_pass(k - 1, (s - 1) % PEER_RING)

    def static_group(g):
        for s in range(PEER_RING):
            k = g * PEER_RING + s
            phase(k, s, k >= 1)

    def group(g, carry):
        for s in range(PEER_RING):
            phase(g * PEER_RING + s, s, True)
        return carry

    @pl.when(step == 0)
    def _():
        for s in range(ahead):
            issue(idx_ref, s, s)

    n_groups = tokens // PEER_RING
    static_group(0)
    if n_groups > 1:
        lax.fori_loop(1, n_groups - 1, group, 0)
        static_group(n_groups - 1)
    v_pass(tokens - 1, (tokens - 1) % PEER_RING)


def _peer_expert(idx, xn, gate, h, tab, tokens):
    n = h.shape[0]
    assert tokens % PEER_RING == 0 and n % tokens == 0
    tok = lambda w: pl.BlockSpec((tokens, w), lambda i: (i, 0))
    kern = functools.partial(_peer_expert_kernel, tokens=tokens)
    steps = n // tokens
    return pl.pallas_call(
        kern,
        grid=(steps,),
        in_specs=[pl.BlockSpec((tokens, PEER_PICKS), lambda i: (i, 0), memory_space=pltpu.SMEM),
                  pl.BlockSpec((tokens, PEER_PICKS), lambda i: (jnp.minimum(i + 1, steps - 1), 0),
                               memory_space=pltpu.SMEM),
                  tok(D_MODEL), tok(PEER_PICKS), tok(D_MODEL),
                  pl.BlockSpec(memory_space=pl.ANY)],
        out_specs=tok(D_MODEL),
        out_shape=jax.ShapeDtypeStruct((n, D_MODEL), F32),
        scratch_shapes=[pltpu.VMEM((PEER_PICKS, PEER_CHUNKS, LANES), jnp.uint32)
                        for _ in range(PEER_RING)] + [
                        pltpu.VMEM((PEER_PICKS, LANES), F32),
                        pltpu.VMEM((PEER_RING, PEER_PICKS), F32),
                        pltpu.SemaphoreType.DMA((PEER_RING,))],
        compiler_params=_params(VMEM_SMALL_MIB),
        name="peer_expert",
    )(idx, idx, xn, gate, h, tab)


def _final_kernel(h_ref, p_ref, gg_ref, wg_ref, wp_ref, gp_ref, gf_ref, y_ref):
    h = h_ref[...]
    gate = jax.nn.sigmoid(jnp.dot(_rms(h, gg_ref[...]).astype(BF16), wg_ref[...],
                                  preferred_element_type=F32))
    ple = jnp.dot(p_ref[...].astype(BF16), wp_ref[...], preferred_element_type=F32)
    h = h + _rms(ple, gp_ref[...]) * gate
    y_ref[...] = _rms(h, gf_ref[...])


def _final(h, p, g_gate, w_gate_bf, w_ple_bf, g_ple, g_final, tb):
    n = h.shape[0]
    tok = lambda w: pl.BlockSpec((tb, w), lambda i: (i, 0))
    return pl.pallas_call(
        _final_kernel,
        grid=(n // tb,),
        in_specs=[tok(D_MODEL), tok(PLE_DIM), _const_spec((1, D_MODEL)),
                  _const_spec((D_MODEL, D_MODEL)), _const_spec((PLE_DIM, D_MODEL)),
                  _const_spec((1, D_MODEL)), _const_spec((1, D_MODEL))],
        out_specs=tok(D_MODEL),
        out_shape=jax.ShapeDtypeStruct((n, D_MODEL), F32),
        compiler_params=_params(VMEM_DENSE_MIB),
        name="final",
    )(h, p, g_gate, w_gate_bf, w_ple_bf, g_ple, g_final)


def _rope_tables(pos):
    half = HEAD_DIM // 2
    inv_freq = 1.0 / (ROPE_THETA ** (jnp.arange(0, HEAD_DIM, 2, dtype=F32) / HEAD_DIM))
    ang = pos.astype(F32)[:, None] * inv_freq[None, :]
    cos, sin = jnp.cos(ang), jnp.sin(ang)
    reps = LANES // HEAD_DIM
    return (jnp.tile(jnp.concatenate([cos, cos], axis=1), (1, reps)),
            jnp.tile(jnp.concatenate([-sin, sin], axis=1), (1, reps)))


def _row(v):
    return v.reshape(1, -1).astype(F32)


def _token_stages(x, p, cos_t, sin_t, attn_fn, wm_bf, bias, lw, tb):
    q, k, v, u, vn = _in_proj(x, lw["g_mix"], lw["w_in"], cos_t, sin_t, lw["g_sgu"], lw["b_sgu"], tb)
    an = attn_fn(q, k, v)
    h = _sgu_out(u, vn, wm_bf, bias, lw["g_sgu_out"], an, lw["w_out"], x, tb)
    xn, idx, gate = _peer_topk(h, lw["g_ffn"], lw["w_pq"], lw["sub_k1"], lw["sub_k2"], tb)
    h = _peer_expert(idx, xn, gate, h, lw["uv_tab"], min(256, x.shape[0]))
    return h, k, v, vn


def kernel(x_prompt, x_sample, cache_k, cache_v, p_prompt, p_sample, g_mix, w_in, sinks, g_sgu, b_sgu,
           w_s, b_s, g_attn_out, g_sgu_out, w_out, g_ffn, w_pq, sub_k1, sub_k2, u_tab, v_tab, g_gate,
           w_gate, w_ple, g_ple, g_final):
    depth = w_in.shape[0]
    bsz, seq, _ = x_prompt.shape
    dbsz, dseq, _ = x_sample.shape
    wc = cache_k.shape[2]
    assert depth == 1, "the final kernel fuses the layer tail with the final norm"
    assert seq % WINDOW == 0 and wc == WINDOW and WINDOW % dseq == 0 and dseq <= SGU_CHUNK
    n_p, n_s = bsz * seq, dbsz * dseq
    tb_p = 256 if seq % 256 == 0 else WINDOW
    tb_s = WINDOW
    assert n_s % tb_s == 0 and seq % tb_p == 0
    nb = seq // WINDOW

    cos_p, sin_p = _rope_tables(jnp.arange(seq, dtype=jnp.int32))
    cos_s, sin_s = _rope_tables(PAST_LEN + (jnp.arange(tb_s, dtype=jnp.int32) % dseq))
    causal = jnp.tril(jnp.ones((SGU_CHUNK, SGU_CHUNK), F32))
    eye = jnp.eye(SGU_CHUNK // dseq, dtype=F32)

    hp = x_prompt.reshape(n_p, D_MODEL)
    hs = x_sample.reshape(n_s, D_MODEL)
    kp_l, vp_l, ks_l, vs_l, gs_l = [], [], [], [], []
    for i in range(depth):
        lw = dict(
            g_mix=_row(g_mix[i]), w_in=w_in[i].astype(BF16), g_sgu=_row(g_sgu[i]), b_sgu=_row(b_sgu[i]),
            g_sgu_out=_row(g_sgu_out[i]), w_out=w_out[i].astype(BF16), g_ffn=_row(g_ffn[i]),
            w_pq=w_pq[i].astype(BF16), sub_k1=sub_k1[i].astype(BF16), sub_k2=sub_k2[i].astype(BF16),
            uv_tab=_pack_uv(u_tab[i], v_tab[i]))
        sinks_i = sinks[i].astype(F32)
        g_ao = _row(g_attn_out[i])
        ws = w_s[i].astype(F32) * causal
        wm_p = ws.astype(BF16)
        bias_p = jnp.broadcast_to(b_s[i].astype(F32)[:, :, None], (SGU_HEADS, SGU_CHUNK, SGU_GROUP))
        wm_s = jnp.einsum("ab,gts->gatbs", eye, ws[:, :dseq, :dseq]).reshape(
            SGU_HEADS, SGU_CHUNK, SGU_CHUNK).astype(BF16)
        bias_s = jnp.broadcast_to(jnp.tile(b_s[i].astype(F32)[:, :dseq], (1, SGU_CHUNK // dseq))[:, :, None],
                                  (SGU_HEADS, SGU_CHUNK, SGU_GROUP))

        def attn_prompt(q, k, v):
            return _attention(sinks_i, q, k, v, g_ao, nb)

        def attn_sample(q, k, v, ck=cache_k[i], cv=cache_v[i]):
            return _attention_sample(sinks_i, q, ck.reshape(dbsz * wc, KV_COLS), cv.reshape(dbsz * wc, KV_COLS),
                                     k, v, g_ao, dseq, SGU_CHUNK // dseq)

        hp, kp, vp, _ = _token_stages(hp, None, cos_p, sin_p, attn_prompt, wm_p, bias_p, lw, tb_p)
        hs, ks, vs, vns = _token_stages(hs, None, cos_s, sin_s, attn_sample, wm_s, bias_s, lw, tb_s)
        fin = (_row(g_gate[i]), w_gate[i].astype(BF16), w_ple[i].astype(BF16), _row(g_ple[i]),
               _row(g_final))
        hp = _final(hp, p_prompt[i].reshape(n_p, PLE_DIM), *fin, tb_p)
        hs = _final(hs, p_sample[i].reshape(n_s, PLE_DIM), *fin, tb_s)

        kp_l.append(kp.reshape(bsz, seq, N_KV, HEAD_DIM)[:, seq - WINDOW:])
        vp_l.append(vp.reshape(bsz, seq, N_KV, HEAD_DIM)[:, seq - WINDOW:])
        ks_l.append(jnp.concatenate([cache_k[i][:, dseq:], ks.reshape(dbsz, dseq, N_KV, HEAD_DIM)], axis=1))
        vs_l.append(jnp.concatenate([cache_v[i][:, dseq:], vs.reshape(dbsz, dseq, N_KV, HEAD_DIM)], axis=1))
        gs_l.append(vns.reshape(dbsz, dseq, SGU_HEADS, SGU_GROUP))

    return (hp.reshape(bsz, seq, D_MODEL), hs.reshape(dbsz, dseq, D_MODEL),
            jnp.stack(kp_l), jnp.stack(vp_l), jnp.stack(ks_l), jnp.stack(vs_l), jnp.stack(gs_l))
```

```python
import functools

import jax
import jax.numpy as jnp
from jax import lax
from jax.experimental import pallas as pl
from jax.experimental.pallas import tpu as pltpu

F32 = jnp.float32
BF16 = jnp.bfloat16

D_MODEL = 2048
PAST_LEN = 16384
HEAD_DIM = 64
N_HEADS = 16
N_KV = 2
Q_PER_KV = N_HEADS // N_KV
WINDOW = 128
ROPE_THETA = 10000.0
NEG_INF = -1e30
ATTN_WIDTH = N_HEADS * HEAD_DIM
SGU_CHUNK = 128
SGU_GROUP = 128
SGU_HEADS = 8
SGU_WIDTH = SGU_HEADS * SGU_GROUP
Q_COLS = N_HEADS * HEAD_DIM
KV_COLS = N_KV * HEAD_DIM
IN_COLS = Q_COLS + 2 * KV_COLS + 2 * SGU_WIDTH
PEER_HEADS = 8
PEER_NKEYS = 128
PEER_HALF = 128
PEER_TOPK = 16
PEER_PICKS = PEER_HEADS * PEER_TOPK
PLE_DIM = 256
EPS = 1e-6

LANES = 128
SUBLANES = 8
MIB = 1024 * 1024
VMEM_DENSE_MIB = 48
VMEM_SMALL_MIB = 32


def _rms(x, g):
    return x * lax.rsqrt(jnp.mean(x * x, axis=-1, keepdims=True) + EPS) * g


def _gelu(x):
    return 0.5 * x * (1.0 + lax.erf(x * (0.5 ** 0.5)))


def _const_spec(shape):
    nd = len(shape)
    return pl.BlockSpec(shape, lambda *_: (0,) * nd, pipeline_mode=pl.Buffered(1))


def _params(vmem_mib):
    return pltpu.CompilerParams(dimension_semantics=("arbitrary",), vmem_limit_bytes=vmem_mib * MIB)


def _in_proj_kernel(x_ref, g_ref, w_ref, cos_ref, sin_ref, gs_ref, bs_ref,
                    q_ref, k_ref, v_ref, u_ref, vn_ref):
    xn = _rms(x_ref[...], g_ref[...]).astype(BF16)
    z = jnp.dot(xn, w_ref[...], preferred_element_type=F32)
    cos = cos_ref[...]
    sin = sin_ref[...]
    lane = lax.broadcasted_iota(jnp.int32, cos.shape, 1)
    first_half = (lane % HEAD_DIM) < (HEAD_DIM // 2)

    def rope(zc):
        swapped = jnp.where(first_half,
                            pltpu.roll(zc, LANES - HEAD_DIM // 2, 1),
                            pltpu.roll(zc, HEAD_DIM // 2, 1))
        return zc * cos + swapped * sin

    for c in range(Q_COLS // LANES):
        q_ref[:, c * LANES:(c + 1) * LANES] = rope(z[:, c * LANES:(c + 1) * LANES])
    k_ref[...] = rope(z[:, Q_COLS:Q_COLS + KV_COLS])
    v_ref[...] = z[:, Q_COLS + KV_COLS:Q_COLS + 2 * KV_COLS]
    u0 = Q_COLS + 2 * KV_COLS
    u_ref[...] = _gelu(z[:, u0:u0 + SGU_WIDTH])
    sv = _gelu(z[:, u0 + SGU_WIDTH:u0 + 2 * SGU_WIDTH])
    mu = jnp.mean(sv, axis=-1, keepdims=True)
    svc = sv - mu
    var = jnp.mean(svc * svc, axis=-1, keepdims=True)
    vn_ref[...] = svc * lax.rsqrt(var + EPS) * gs_ref[...] + bs_ref[...]


def _in_proj(x, g_mix, w_in_bf, cos_t, sin_t, g_sgu, b_sgu, tb):
    n = x.shape[0]
    period_blocks = cos_t.shape[0] // tb
    tok = lambda w: pl.BlockSpec((tb, w), lambda i: (i, 0))
    tab = pl.BlockSpec((tb, LANES), lambda i: (i % period_blocks, 0))
    return pl.pallas_call(
        _in_proj_kernel,
        grid=(n // tb,),
        in_specs=[tok(D_MODEL), _const_spec((1, D_MODEL)), _const_spec((D_MODEL, IN_COLS)),
                  tab, tab, _const_spec((1, SGU_WIDTH)), _const_spec((1, SGU_WIDTH))],
        out_specs=[tok(Q_COLS), tok(KV_COLS), tok(KV_COLS), tok(SGU_WIDTH), tok(SGU_WIDTH)],
        out_shape=[jax.ShapeDtypeStruct((n, w), F32)
                   for w in (Q_COLS, KV_COLS, KV_COLS, SGU_WIDTH, SGU_WIDTH)],
        compiler_params=_params(VMEM_DENSE_MIB),
        name="in_proj",
    )(x, g_mix, w_in_bf, cos_t, sin_t, g_sgu, b_sgu)


def _attn_kernel(sinks_ref, q_ref, ka_ref, kb_ref, va_ref, vb_ref, g_ref, o_ref, acc_ref, *, blocks_per_seq):
    q = q_ref[...]
    kk_t_f32 = jnp.concatenate([ka_ref[...].T, kb_ref[...].T], axis=1)
    vv = jnp.concatenate([va_ref[...], vb_ref[...]], axis=0).astype(BF16)
    r = lax.broadcasted_iota(jnp.int32, (WINDOW, 2 * WINDOW), 0)
    c = lax.broadcasted_iota(jnp.int32, (WINDOW, 2 * WINDOW), 1)
    first = pl.program_id(0) % blocks_per_seq == 0
    valid = (c > jnp.where(first, jnp.maximum(r, WINDOW - 1), r)) & (c <= r + WINDOW)
    kk_t = kk_t_f32.astype(BF16)
    for g in range(N_KV):
        kg_t = kk_t[g * HEAD_DIM:(g + 1) * HEAD_DIM, :]
        vg = vv[:, g * HEAD_DIM:(g + 1) * HEAD_DIM]
        for hh in range(Q_PER_KV):
            h = g * Q_PER_KV + hh
            qh = q[:, h * HEAD_DIM:(h + 1) * HEAD_DIM].astype(BF16)
            s = jnp.dot(qh, kg_t, preferred_element_type=F32) * (HEAD_DIM ** -0.5)
            s = jnp.where(valid, s, NEG_INF)
            sink = sinks_ref[h]
            m = jnp.maximum(jnp.max(s, axis=1, keepdims=True), sink)
            p = jnp.exp(s - m)
            denom = jnp.sum(p, axis=1, keepdims=True) + jnp.exp(sink - m)
            p = (p / denom).astype(BF16)
            acc_ref[:, h * HEAD_DIM:(h + 1) * HEAD_DIM] = jnp.dot(p, vg, preferred_element_type=F32)
    o_ref[...] = _rms(acc_ref[...], g_ref[...]).astype(o_ref.dtype)


def _attention(sinks, q, k, v, g_attn_out, blocks_per_seq):
    n = q.shape[0]
    prev = pl.BlockSpec((WINDOW, KV_COLS), lambda j: (jnp.where(j % blocks_per_seq == 0, j, j - 1), 0))
    cur = pl.BlockSpec((WINDOW, KV_COLS), lambda j: (j, 0))
    kern = functools.partial(_attn_kernel, blocks_per_seq=blocks_per_seq)
    return pl.pallas_call(
        kern,
        grid=(n // WINDOW,),
        in_specs=[pl.BlockSpec(memory_space=pltpu.SMEM),
                  pl.BlockSpec((WINDOW, Q_COLS), lambda i: (i, 0)),
                  prev, cur, prev, cur,
                  _const_spec((1, ATTN_WIDTH))],
        out_specs=pl.BlockSpec((WINDOW, ATTN_WIDTH), lambda i: (i, 0)),
        out_shape=jax.ShapeDtypeStruct((n, ATTN_WIDTH), BF16),
        scratch_shapes=[pltpu.VMEM((WINDOW, ATTN_WIDTH), F32)],
        compiler_params=_params(VMEM_SMALL_MIB),
        name="attention",
    )(sinks, q, k, k, v, v, g_attn_out)


def _attn_sample_kernel(sinks_ref, q_ref, ck_ref, cv_ref, kn_ref, vn_ref, g_ref, o_ref, acc_ref, *, dseq, seqs):
    rows = Q_PER_KV * dseq
    r = lax.broadcasted_iota(jnp.int32, (rows, WINDOW), 0) % dseq
    c = lax.broadcasted_iota(jnp.int32, (rows, WINDOW), 1)
    valid_cache = c > r
    valid_new = c <= r
    head_of_row = lax.broadcasted_iota(jnp.int32, (rows, 1), 0) // dseq
    sink_cols = []
    for g in range(N_KV):
        col = jnp.zeros((rows, 1), F32)
        for hh in range(Q_PER_KV):
            col = jnp.where(head_of_row == hh, sinks_ref[g * Q_PER_KV + hh], col)
        sink_cols.append(col)
    pad = jnp.zeros((WINDOW - dseq, HEAD_DIM), F32)
    nt = (((1,), (1,)), ((), ()))
    for i in range(seqs):
        new = slice(i * dseq, (i + 1) * dseq)
        old = slice(i * WINDOW, (i + 1) * WINDOW)
        q = q_ref[new, :]
        for g in range(N_KV):
            cols = slice(g * HEAD_DIM, (g + 1) * HEAD_DIM)
            qg = jnp.concatenate([q[:, (g * Q_PER_KV + hh) * HEAD_DIM:(g * Q_PER_KV + hh + 1) * HEAD_DIM]
                                  for hh in range(Q_PER_KV)], axis=0).astype(BF16)
            k_new = jnp.concatenate([kn_ref[new, cols], pad], axis=0).astype(BF16)
            v_new = jnp.concatenate([vn_ref[new, cols], pad], axis=0).astype(BF16)
            s_old = lax.dot_general(qg, ck_ref[old, cols].astype(BF16), nt,
                                    preferred_element_type=F32) * (HEAD_DIM ** -0.5)
            s_new = lax.dot_general(qg, k_new, nt, preferred_element_type=F32) * (HEAD_DIM ** -0.5)
            s_old = jnp.where(valid_cache, s_old, NEG_INF)
            s_new = jnp.where(valid_new, s_new, NEG_INF)
            sink = sink_cols[g]
            m = jnp.maximum(jnp.maximum(jnp.max(s_old, axis=1, keepdims=True),
                                        jnp.max(s_new, axis=1, keepdims=True)), sink)
            p_old = jnp.exp(s_old - m)
            p_new = jnp.exp(s_new - m)
            denom = (jnp.sum(p_old, axis=1, keepdims=True) + jnp.sum(p_new, axis=1, keepdims=True)
                     + jnp.exp(sink - m))
            o = (jnp.dot((p_old / denom).astype(BF16), cv_ref[old, cols].astype(BF16),
                         preferred_element_type=F32)
                 + jnp.dot((p_new / denom).astype(BF16), v_new, preferred_element_type=F32))
            for hh in range(Q_PER_KV):
                h = g * Q_PER_KV + hh
                acc_ref[new, h * HEAD_DIM:(h + 1) * HEAD_DIM] = o[hh * dseq:(hh + 1) * dseq, :]
    o_ref[...] = _rms(acc_ref[...], g_ref[...]).astype(o_ref.dtype)


def _attention_sample(sinks, q, cache_k, cache_v, k_new, v_new, g_attn_out, dseq, seqs):
    n = q.shape[0]
    tok = lambda w: pl.BlockSpec((seqs * dseq, w), lambda i: (i, 0))
    cache = pl.BlockSpec((seqs * WINDOW, KV_COLS), lambda i: (i, 0))
    kern = functools.partial(_attn_sample_kernel, dseq=dseq, seqs=seqs)
    return pl.pallas_call(
        kern,
        grid=(n // (seqs * dseq),),
        in_specs=[pl.BlockSpec(memory_space=pltpu.SMEM), tok(Q_COLS), cache, cache, tok(KV_COLS), tok(KV_COLS),
                  _const_spec((1, ATTN_WIDTH))],
        out_specs=tok(ATTN_WIDTH),
        out_shape=jax.ShapeDtypeStruct((n, ATTN_WIDTH), BF16),
        scratch_shapes=[pltpu.VMEM((seqs * dseq, ATTN_WIDTH), F32)],
        compiler_params=_params(VMEM_SMALL_MIB),
        name="attention_sample",
    )(sinks, q, cache_k, cache_v, k_new, v_new, g_attn_out)


def _sgu_out_kernel(u_ref, vn_ref, wm_ref, bias_ref, g_ref, an_ref, wo_ref, x_ref, h_ref, s_ref):
    tb = u_ref.shape[0]
    for ch in range(tb // SGU_CHUNK):
        rows = slice(ch * SGU_CHUNK, (ch + 1) * SGU_CHUNK)
        for g in range(SGU_HEADS):
            cols = slice(g * SGU_GROUP, (g + 1) * SGU_GROUP)
            mix = jnp.dot(wm_ref[g], vn_ref[rows, cols].astype(BF16),
                          preferred_element_type=F32) + bias_ref[g]
            s_ref[rows, cols] = u_ref[rows, cols] * mix
    sn = _rms(s_ref[...], g_ref[...]).astype(BF16)
    mixed = jnp.dot(an_ref[...], wo_ref[:ATTN_WIDTH, :], preferred_element_type=F32)
    mixed = mixed + jnp.dot(sn, wo_ref[ATTN_WIDTH:, :], preferred_element_type=F32)
    h_ref[...] = x_ref[...] + mixed


def _sgu_out(u, vn, wm_bf, bias, g_sgu_out, an, w_out_bf, x, tb):
    n = x.shape[0]
    tok = lambda w: pl.BlockSpec((tb, w), lambda i: (i, 0))
    return pl.pallas_call(
        _sgu_out_kernel,
        grid=(n // tb,),
        in_specs=[tok(SGU_WIDTH), tok(SGU_WIDTH),
                  _const_spec((SGU_HEADS, SGU_CHUNK, SGU_CHUNK)),
                  _const_spec((SGU_HEADS, SGU_CHUNK, SGU_GROUP)),
                  _const_spec((1, SGU_WIDTH)), tok(ATTN_WIDTH),
                  _const_spec((D_MODEL, D_MODEL)), tok(D_MODEL)],
        out_specs=tok(D_MODEL),
        out_shape=jax.ShapeDtypeStruct((n, D_MODEL), F32),
        scratch_shapes=[pltpu.VMEM((tb, SGU_WIDTH), F32)],
        compiler_params=_params(VMEM_DENSE_MIB),
        name="sgu_out",
    )(u, vn, wm_bf, bias, g_sgu_out, an, w_out_bf, x)


def _odd_even_merge_sort_pairs(n):
    pairs = []

    def merge(lo, length, r):
        step = 2 * r
        if step < length:
            merge(lo, length, step)
            merge(lo + r, length, step)
            pairs.extend((i, i + r) for i in range(lo + r, lo + length - r, step))
        else:
            pairs.append((lo, lo + r))

    def sort(lo, length):
        if length > 1:
            half = length // 2
            sort(lo, half)
            sort(lo + half, half)
            merge(lo, length, 1)

    sort(0, n)
    return pairs


_BIG_ROW = 2 ** 30


def _top16_rows(s, tag=None, tag_bits=0):
    depth = s.shape[0] // SUBLANES
    row8 = lax.broadcasted_iota(jnp.int32, (SUBLANES, s.shape[1]), 0)
    vals = [s[d * SUBLANES:(d + 1) * SUBLANES, :] for d in range(depth)]
    pays = [(row8 + d * SUBLANES) << tag_bits for d in range(depth)]
    if tag is not None:
        pays = [p | tag[d * SUBLANES:(d + 1) * SUBLANES, :] for d, p in enumerate(pays)]
    for i, j in _odd_even_merge_sort_pairs(depth):
        a, b, pa, pb = vals[i], vals[j], pays[i], pays[j]
        swap = (b > a) | ((b == a) & (pb < pa))
        vals[i], vals[j] = jnp.where(swap, b, a), jnp.where(swap, a, b)
        pays[i], pays[j] = jnp.where(swap, pb, pa), jnp.where(swap, pa, pb)
    out_v, out_p = [], []
    for k in range(PEER_TOPK):
        head, head_p = vals[0], pays[0]
        m = jnp.max(head, axis=0, keepdims=True)
        p = jnp.min(jnp.where(head == m, head_p, _BIG_ROW), axis=0, keepdims=True)
        out_v.append(m)
        out_p.append(p)
        popped = head_p == p
        for d in range(min(depth, PEER_TOPK - 1 - k)):
            below_v, below_p = (vals[d + 1], pays[d + 1]) if d + 1 < depth else (-jnp.inf, _BIG_ROW)
            vals[d] = jnp.where(popped, below_v, vals[d])
            pays[d] = jnp.where(popped, below_p, pays[d])
    return jnp.concatenate(out_v, axis=0), jnp.concatenate(out_p, axis=0)


_CAND_WIDTHS = (16, 8, 8, 8, 4, 4, 4, 4)


def _staircase(a, b, combine):
    rows = [combine(a[i:i + 1, :], b[:w, :]) for i, w in enumerate(_CAND_WIDTHS)]
    rows.append(combine(a[len(_CAND_WIDTHS):, :], b[0:1, :]))
    return jnp.concatenate(rows, axis=0)


def _peer_topk_kernel(h_ref, g_ref, wq_ref, k1_ref, k2_ref, xn_ref, idx_ref, gate_ref):
    xn = _rms(h_ref[...], g_ref[...])
    xn_ref[...] = xn
    qp = jnp.dot(xn.astype(BF16), wq_ref[...], preferred_element_type=F32).astype(BF16)
    idx_parts, gate_parts = [], []
    nt = (((1,), (1,)), ((), ()))
    for h in range(PEER_HEADS):
        base = h * 2 * PEER_HALF
        s1 = lax.dot_general(k1_ref[h], qp[:, base:base + PEER_HALF], nt,
                             preferred_element_type=F32)
        s2 = lax.dot_general(k2_ref[h], qp[:, base + PEER_HALF:base + 2 * PEER_HALF], nt,
                             preferred_element_type=F32)
        t1, i1 = _top16_rows(s1)
        t2, i2 = _top16_rows(s2)
        cand = _staircase(t1, t2, lambda a, b: a + b)
        expert = _staircase(i1, i2, lambda a, b: a * PEER_NKEYS + b)
        expert_bits = (PEER_NKEYS * PEER_NKEYS - 1).bit_length()
        sc, code = _top16_rows(cand, expert, expert_bits)
        idx_parts.append(code & ((1 << expert_bits) - 1))
        ex = jnp.exp(sc - sc[0:1, :])
        gate_parts.append(ex / jnp.sum(ex, axis=0, keepdims=True))
    idx_t = jnp.concatenate(idx_parts, axis=0)
    gate_t = jnp.concatenate(gate_parts, axis=0)
    tb = h_ref.shape[0]
    for c in range(tb // LANES):
        idx_ref[c * LANES:(c + 1) * LANES, :] = idx_t[:, c * LANES:(c + 1) * LANES].T
        gate_ref[c * LANES:(c + 1) * LANES, :] = gate_t[:, c * LANES:(c + 1) * LANES].T


def _peer_topk(h, g_ffn, w_pq_bf, k1_bf, k2_bf, tb):
    n = h.shape[0]
    tok = lambda w: pl.BlockSpec((tb, w), lambda i: (i, 0))
    return pl.pallas_call(
        _peer_topk_kernel,
        grid=(n // tb,),
        in_specs=[tok(D_MODEL), _const_spec((1, D_MODEL)),
                  _const_spec((D_MODEL, PEER_HEADS * 2 * PEER_HALF)),
                  _const_spec((PEER_HEADS, PEER_NKEYS, PEER_HALF)),
                  _const_spec((PEER_HEADS, PEER_NKEYS, PEER_HALF))],
        out_specs=[tok(D_MODEL), tok(PEER_PICKS), tok(PEER_PICKS)],
        out_shape=[jax.ShapeDtypeStruct((n, D_MODEL), F32),
                   jax.ShapeDtypeStruct((n, PEER_PICKS), jnp.int32),
                   jax.ShapeDtypeStruct((n, PEER_PICKS), F32)],
        compiler_params=_params(VMEM_DENSE_MIB),
        name="peer_topk",
    )(h, g_ffn, w_pq_bf, k1_bf, k2_bf)


PEER_RING = 8
PEER_CHUNKS = D_MODEL // LANES
PEER_EGROUPS = PEER_PICKS // SUBLANES


def _pack_uv(u_tab, v_tab):
    n = u_tab.shape[0]
    return jnp.concatenate([u_tab.astype(BF16).reshape(n, PEER_CHUNKS, LANES),
                            v_tab.astype(BF16).reshape(n, PEER_CHUNKS, LANES)], axis=1)


def _peer_expert_kernel(idx_ref, nidx_ref, xn_ref, gate_ref, h_ref, tab_hbm, o_ref, *scratch, tokens):
    bufs, (hk_rep, hk_scr, sem) = scratch[:PEER_RING], scratch[PEER_RING:]

    def row_copy(slot, e, r):
        return pltpu.make_async_copy(tab_hbm.at[r], bufs[slot].at[e], sem.at[slot])

    def issue(src_idx_ref, t, slot):
        for e in range(PEER_PICKS):
            row_copy(slot, e, src_idx_ref[t, e]).start(priority=e % 2)

    def wait(slot):
        for e in range(PEER_PICKS):
            row_copy(slot, e, 0).wait()

    sub = lax.broadcasted_iota(jnp.int32, (SUBLANES, LANES), 0)
    lane_minus_sub = lax.broadcasted_iota(jnp.int32, (SUBLANES, LANES), 1) - sub

    def fold_sublanes(parts):
        step = SUBLANES // 2
        while step >= 1:
            lower = (sub & step) == 0
            nxt = []
            for j in range(len(parts) // 2):
                a, b = parts[(j // step) * 2 * step + j % step], parts[(j // step) * 2 * step + j % step + step]
                nxt.append(jnp.where(lower, a, pltpu.roll(b, step, 0))
                           + jnp.where(lower, pltpu.roll(a, SUBLANES - step, 0), b))
            parts = nxt
            step //= 2
        return parts[0]

    def u_pass(t, slot):
        row = pl.ds(t, 1)
        x_row = xn_ref[row, :]
        x_lo = jnp.concatenate([x_row[:, c * LANES:(c + 1) * LANES] for c in range(SUBLANES)], axis=0)
        x_hi = jnp.concatenate([x_row[:, c * LANES:(c + 1) * LANES] for c in range(SUBLANES, PEER_CHUNKS)], axis=0)
        a_diag = jnp.zeros((SUBLANES, LANES), F32)
        for g in range(PEER_EGROUPS):
            parts = []
            for j in range(SUBLANES):
                u = bufs[slot][g * SUBLANES + j, :PEER_CHUNKS, :].astype(F32)
                parts.append(u[:SUBLANES] * x_lo + u[SUBLANES:] * x_hi)
            a_g = jnp.sum(fold_sublanes(parts), axis=1, keepdims=True)
            a_diag = jnp.where(lane_minus_sub == g * SUBLANES, a_g, a_diag)
        a_row = jnp.sum(a_diag, axis=0, keepdims=True)
        hk_scr[slot:slot + 1, :] = _gelu(a_row) * gate_ref[row, :]

    def v_pass(t, slot):
        row = pl.ds(t, 1)
        hk_row = jnp.broadcast_to(hk_scr[slot:slot + 1, :], (SUBLANES, LANES))
        for g in range(PEER_EGROUPS):
            hk_rep[g * SUBLANES:(g + 1) * SUBLANES, :] = jnp.broadcast_to(
                jnp.sum(jnp.where(lane_minus_sub == g * SUBLANES, hk_row, 0.0), axis=1, keepdims=True),
                (SUBLANES, LANES))
        n_chains = 4
        lo, hi = [None] * n_chains, [None] * n_chains
        for e in range(PEER_PICKS):
            v = bufs[slot][e, PEER_CHUNKS:, :].astype(F32)
            hk_e = jnp.broadcast_to(hk_rep[e:e + 1, :], (SUBLANES, LANES))
            p_lo = v[:SUBLANES] * hk_e
            p_hi = v[SUBLANES:] * hk_e
            i = e % n_chains
            lo[i] = p_lo if lo[i] is None else lo[i] + p_lo
            hi[i] = p_hi if hi[i] is None else hi[i] + p_hi
        out_lo = (lo[0] + lo[1]) + (lo[2] + lo[3])
        out_hi = (hi[0] + hi[1]) + (hi[2] + hi[3])
        out_row = jnp.concatenate([out_lo[c:c + 1, :] for c in range(SUBLANES)]
                                  + [out_hi[c:c + 1, :] for c in range(SUBLANES)], axis=1)
        o_ref[row, :] = h_ref[row, :] + out_row

    step, last_step = pl.program_id(0), pl.num_programs(0) - 1
    ahead = PEER_RING - 2

    def phase(k, s, do_v):
        wait(s)
        j, slot = k + ahead, (s + ahead) % PEER_RING
        if isinstance(j, int) and j >= tokens:
            @pl.when(step < last_step)
            def _():
                issue(nidx_ref, j - tokens, slot)
        else:
            issue(idx_ref, j, slot)
        u_pass(k, s)
        if do_v:
            v_pass(k - 1, (s - 1) % PEER_RING)

    def static_group(g):
        for s in range(PEER_RING):
            k = g * PEER_RING + s
            phase(k, s, k >= 1)

    def group(g, carry):
        for s in range(PEER_RING):
            phase(g * PEER_RING + s, s, True)
        return carry

    @pl.when(step == 0)
    def _():
        for s in range(ahead):
            issue(idx_ref, s, s)

    n_groups = tokens // PEER_RING
    static_group(0)
    if n_groups > 1:
        lax.fori_loop(1, n_groups - 1, group, 0)
        static_group(n_groups - 1)
    v_pass(tokens - 1, (tokens - 1) % PEER_RING)


def _peer_expert(idx, xn, gate, h, tab, tokens):
    n = h.shape[0]
    assert tokens % PEER_RING == 0 and n % tokens == 0
    tok = lambda w: pl.BlockSpec((tokens, w), lambda i: (i, 0))
    kern = functools.partial(_peer_expert_kernel, tokens=tokens)
    steps = n // tokens
    return pl.pallas_call(
        kern,
        grid=(steps,),
        in_specs=[pl.BlockSpec((tokens, PEER_PICKS), lambda i: (i, 0), memory_space=pltpu.SMEM),
                  pl.BlockSpec((tokens, PEER_PICKS), lambda i: (jnp.minimum(i + 1, steps - 1), 0),
                               memory_space=pltpu.SMEM),
                  tok(D_MODEL), tok(PEER_PICKS), tok(D_MODEL),
                  pl.BlockSpec(memory_space=pl.ANY)],
        out_specs=tok(D_MODEL),
        out_shape=jax.ShapeDtypeStruct((n, D_MODEL), F32),
        scratch_shapes=[pltpu.VMEM((PEER_PICKS, 2 * PEER_CHUNKS, LANES), BF16)
                        for _ in range(PEER_RING)] + [
                        pltpu.VMEM((PEER_PICKS, LANES), F32),
                        pltpu.VMEM((PEER_RING, PEER_PICKS), F32),
                        pltpu.SemaphoreType.DMA((PEER_RING,))],
        compiler_params=_params(VMEM_SMALL_MIB),
        name="peer_expert",
    )(idx, idx, xn, gate, h, tab)


def _final_kernel(h_ref, p_ref, gg_ref, wg_ref, wp_ref, gp_ref, gf_ref, y_ref):
    h = h_ref[...]
    gate = jax.nn.sigmoid(jnp.dot(_rms(h, gg_ref[...]).astype(BF16), wg_ref[...],
                                  preferred_element_type=F32))
    ple = jnp.dot(p_ref[...].astype(BF16), wp_ref[...], preferred_element_type=F32)
    h = h + _rms(ple, gp_ref[...]) * gate
    y_ref[...] = _rms(h, gf_ref[...])


def _final(h, p, g_gate, w_gate_bf, w_ple_bf, g_ple, g_final, tb):
    n = h.shape[0]
    tok = lambda w: pl.BlockSpec((tb, w), lambda i: (i, 0))
    return pl.pallas_call(
        _final_kernel,
        grid=(n // tb,),
        in_specs=[tok(D_MODEL), tok(PLE_DIM), _const_spec((1, D_MODEL)),
                  _const_spec((D_MODEL, D_MODEL)), _const_spec((PLE_DIM, D_MODEL)),
                  _const_spec((1, D_MODEL)), _const_spec((1, D_MODEL))],
        out_specs=tok(D_MODEL),
        out_shape=jax.ShapeDtypeStruct((n, D_MODEL), F32),
        compiler_params=_params(VMEM_DENSE_MIB),
        name="final",
    )(h, p, g_gate, w_gate_bf, w_ple_bf, g_ple, g_final)


def _rope_tables(pos):
    half = HEAD_DIM // 2
    inv_freq = 1.0 / (ROPE_THETA ** (jnp.arange(0, HEAD_DIM, 2, dtype=F32) / HEAD_DIM))
    ang = pos.astype(F32)[:, None] * inv_freq[None, :]
    cos, sin = jnp.cos(ang), jnp.sin(ang)
    reps = LANES // HEAD_DIM
    return (jnp.tile(jnp.concatenate([cos, cos], axis=1), (1, reps)),
            jnp.tile(jnp.concatenate([-sin, sin], axis=1), (1, reps)))


def _row(v):
    return v.reshape(1, -1).astype(F32)


def _token_stages(x, p, cos_t, sin_t, attn_fn, wm_bf, bias, lw, tb):
    q, k, v, u, vn = _in_proj(x, lw["g_mix"], lw["w_in"], cos_t, sin_t, lw["g_sgu"], lw["b_sgu"], tb)
    an = attn_fn(q, k, v)
    h = _sgu_out(u, vn, wm_bf, bias, lw["g_sgu_out"], an, lw["w_out"], x, tb)
    xn, idx, gate = _peer_topk(h, lw["g_ffn"], lw["w_pq"], lw["sub_k1"], lw["sub_k2"], tb)
    h = _peer_expert(idx, xn, gate, h, lw["uv_tab"], min(256, x.shape[0]))
    return h, k, v, vn


def kernel(x_prompt, x_sample, cache_k, cache_v, p_prompt, p_sample, g_mix, w_in, sinks, g_sgu, b_sgu,
           w_s, b_s, g_attn_out, g_sgu_out, w_out, g_ffn, w_pq, sub_k1, sub_k2, u_tab, v_tab, g_gate,
           w_gate, w_ple, g_ple, g_final):
    depth = w_in.shape[0]
    bsz, seq, _ = x_prompt.shape
    dbsz, dseq, _ = x_sample.shape
    wc = cache_k.shape[2]
    assert depth == 1, "the final kernel fuses the layer tail with the final norm"
    assert seq % WINDOW == 0 and wc == WINDOW and WINDOW % dseq == 0 and dseq <= SGU_CHUNK
    n_p, n_s = bsz * seq, dbsz * dseq
    tb_p = 256 if seq % 256 == 0 else WINDOW
    tb_s = WINDOW
    assert n_s % tb_s == 0 and seq % tb_p == 0
    nb = seq // WINDOW

    cos_p, sin_p = _rope_tables(jnp.arange(seq, dtype=jnp.int32))
    cos_s, sin_s = _rope_tables(PAST_LEN + (jnp.arange(tb_s, dtype=jnp.int32) % dseq))
    causal = jnp.tril(jnp.ones((SGU_CHUNK, SGU_CHUNK), F32))
    eye = jnp.eye(SGU_CHUNK // dseq, dtype=F32)

    hp = x_prompt.reshape(n_p, D_MODEL)
    hs = x_sample.reshape(n_s, D_MODEL)
    kp_l, vp_l, ks_l, vs_l, gs_l = [], [], [], [], []
    for i in range(depth):
        lw = dict(
            g_mix=_row(g_mix[i]), w_in=w_in[i].astype(BF16), g_sgu=_row(g_sgu[i]), b_sgu=_row(b_sgu[i]),
            g_sgu_out=_row(g_sgu_out[i]), w_out=w_out[i].astype(BF16), g_ffn=_row(g_ffn[i]),
            w_pq=w_pq[i].astype(BF16), sub_k1=sub_k1[i].astype(BF16), sub_k2=sub_k2[i].astype(BF16),
            uv_tab=_pack_uv(u_tab[i], v_tab[i]))
        sinks_i = sinks[i].astype(F32)
        g_ao = _row(g_attn_out[i])
        ws = w_s[i].astype(F32) * causal
        wm_p = ws.astype(BF16)
        bias_p = jnp.broadcast_to(b_s[i].astype(F32)[:, :, None], (SGU_HEADS, SGU_CHUNK, SGU_GROUP))
        wm_s = jnp.einsum("ab,gts->gatbs", eye, ws[:, :dseq, :dseq]).reshape(
            SGU_HEADS, SGU_CHUNK, SGU_CHUNK).astype(BF16)
        bias_s = jnp.broadcast_to(jnp.tile(b_s[i].astype(F32)[:, :dseq], (1, SGU_CHUNK // dseq))[:, :, None],
                                  (SGU_HEADS, SGU_CHUNK, SGU_GROUP))

        def attn_prompt(q, k, v):
            return _attention(sinks_i, q, k, v, g_ao, nb)

        def attn_sample(q, k, v, ck=cache_k[i], cv=cache_v[i]):
            return _attention_sample(sinks_i, q, ck.reshape(dbsz * wc, KV_COLS), cv.reshape(dbsz * wc, KV_COLS),
                                     k, v, g_ao, dseq, SGU_CHUNK // dseq)

        hp, kp, vp, _ = _token_stages(hp, None, cos_p, sin_p, attn_prompt, wm_p, bias_p, lw, tb_p)
        hs, ks, vs, vns = _token_stages(hs, None, cos_s, sin_s, attn_sample, wm_s, bias_s, lw, tb_s)
        fin = (_row(g_gate[i]), w_gate[i].astype(BF16), w_ple[i].astype(BF16), _row(g_ple[i]),
               _row(g_final))
        hp = _final(hp, p_prompt[i].reshape(n_p, PLE_DIM), *fin, tb_p)
        hs = _final(hs, p_sample[i].reshape(n_s, PLE_DIM), *fin, tb_s)

        tail = lambda a: a.reshape(bsz, seq, KV_COLS)[:, seq - WINDOW:].reshape(bsz, WINDOW, N_KV, HEAD_DIM)
        kp_l.append(tail(kp))
        vp_l.append(tail(vp))
        ks_l.append(jnp.concatenate([cache_k[i][:, dseq:], ks.reshape(dbsz, dseq, N_KV, HEAD_DIM)], axis=1))
        vs_l.append(jnp.concatenate([cache_v[i][:, dseq:], vs.reshape(dbsz, dseq, N_KV, HEAD_DIM)], axis=1))
        gs_l.append(vns.reshape(dbsz, dseq, SGU_HEADS, SGU_GROUP))

    return (hp.reshape(bsz, seq, D_MODEL), hs.reshape(dbsz, dseq, D_MODEL),
            jnp.stack(kp_l), jnp.stack(vp_l), jnp.stack(ks_l), jnp.stack(vs_l), jnp.stack(gs_l))
```

```python
import functools

import jax
import jax.numpy as jnp
from jax import lax
from jax.experimental import pallas as pl
from jax.experimental.pallas import tpu as pltpu

F32 = jnp.float32
BF16 = jnp.bfloat16

D_MODEL = 2048
PAST_LEN = 16384
HEAD_DIM = 64
N_HEADS = 16
N_KV = 2
Q_PER_KV = N_HEADS // N_KV
WINDOW = 128
ROPE_THETA = 10000.0
NEG_INF = -1e30
ATTN_WIDTH = N_HEADS * HEAD_DIM
SGU_CHUNK = 128
SGU_GROUP = 128
SGU_HEADS = 8
SGU_WIDTH = SGU_HEADS * SGU_GROUP
Q_COLS = N_HEADS * HEAD_DIM
KV_COLS = N_KV * HEAD_DIM
IN_COLS = Q_COLS + 2 * KV_COLS + 2 * SGU_WIDTH
PEER_HEADS = 8
PEER_NKEYS = 128
PEER_HALF = 128
PEER_TOPK = 16
PEER_PICKS = PEER_HEADS * PEER_TOPK
PLE_DIM = 256
EPS = 1e-6

LANES = 128
SUBLANES = 8
MIB = 1024 * 1024
VMEM_DENSE_MIB = 48
VMEM_SMALL_MIB = 32


def _rms(x, g):
    return x * lax.rsqrt(jnp.mean(x * x, axis=-1, keepdims=True) + EPS) * g


def _gelu(x):
    return 0.5 * x * (1.0 + lax.erf(x * (0.5 ** 0.5)))


def _const_spec(shape):
    nd = len(shape)
    return pl.BlockSpec(shape, lambda *_: (0,) * nd, pipeline_mode=pl.Buffered(1))


def _params(vmem_mib):
    return pltpu.CompilerParams(dimension_semantics=("arbitrary",), vmem_limit_bytes=vmem_mib * MIB)


def _in_proj_kernel(x_ref, g_ref, w_ref, cos_ref, sin_ref, gs_ref, bs_ref,
                    q_ref, k_ref, v_ref, u_ref, vn_ref):
    xn = _rms(x_ref[...], g_ref[...]).astype(BF16)
    z = jnp.dot(xn, w_ref[...], preferred_element_type=F32)
    cos = cos_ref[...]
    sin = sin_ref[...]
    lane = lax.broadcasted_iota(jnp.int32, cos.shape, 1)
    first_half = (lane % HEAD_DIM) < (HEAD_DIM // 2)

    def rope(zc):
        swapped = jnp.where(first_half,
                            pltpu.roll(zc, LANES - HEAD_DIM // 2, 1),
                            pltpu.roll(zc, HEAD_DIM // 2, 1))
        return zc * cos + swapped * sin

    for c in range(Q_COLS // LANES):
        q_ref[:, c * LANES:(c + 1) * LANES] = rope(z[:, c * LANES:(c + 1) * LANES])
    k_ref[...] = rope(z[:, Q_COLS:Q_COLS + KV_COLS])
    v_ref[...] = z[:, Q_COLS + KV_COLS:Q_COLS + 2 * KV_COLS]
    u0 = Q_COLS + 2 * KV_COLS
    u_ref[...] = _gelu(z[:, u0:u0 + SGU_WIDTH])
    sv = _gelu(z[:, u0 + SGU_WIDTH:u0 + 2 * SGU_WIDTH])
    mu = jnp.mean(sv, axis=-1, keepdims=True)
    svc = sv - mu
    var = jnp.mean(svc * svc, axis=-1, keepdims=True)
    vn_ref[...] = svc * lax.rsqrt(var + EPS) * gs_ref[...] + bs_ref[...]


def _in_proj(x, g_mix, w_in_bf, cos_t, sin_t, g_sgu, b_sgu, tb):
    n = x.shape[0]
    period_blocks = cos_t.shape[0] // tb
    tok = lambda w: pl.BlockSpec((tb, w), lambda i: (i, 0))
    tab = pl.BlockSpec((tb, LANES), lambda i: (i % period_blocks, 0))
    return pl.pallas_call(
        _in_proj_kernel,
        grid=(n // tb,),
        in_specs=[tok(D_MODEL), _const_spec((1, D_MODEL)), _const_spec((D_MODEL, IN_COLS)),
                  tab, tab, _const_spec((1, SGU_WIDTH)), _const_spec((1, SGU_WIDTH))],
        out_specs=[tok(Q_COLS), tok(KV_COLS), tok(KV_COLS), tok(SGU_WIDTH), tok(SGU_WIDTH)],
        out_shape=[jax.ShapeDtypeStruct((n, w), F32)
                   for w in (Q_COLS, KV_COLS, KV_COLS, SGU_WIDTH, SGU_WIDTH)],
        compiler_params=_params(VMEM_DENSE_MIB),
        name="in_proj",
    )(x, g_mix, w_in_bf, cos_t, sin_t, g_sgu, b_sgu)


def _attn_kernel(sinks_ref, q_ref, ka_ref, kb_ref, va_ref, vb_ref, g_ref, o_ref, acc_ref, *, blocks_per_seq):
    q = q_ref[...]
    kk_t_f32 = jnp.concatenate([ka_ref[...].T, kb_ref[...].T], axis=1)
    vv = jnp.concatenate([va_ref[...], vb_ref[...]], axis=0).astype(BF16)
    r = lax.broadcasted_iota(jnp.int32, (WINDOW, 2 * WINDOW), 0)
    c = lax.broadcasted_iota(jnp.int32, (WINDOW, 2 * WINDOW), 1)
    first = pl.program_id(0) % blocks_per_seq == 0
    valid = (c > jnp.where(first, jnp.maximum(r, WINDOW - 1), r)) & (c <= r + WINDOW)
    kk_t = kk_t_f32.astype(BF16)
    for g in range(N_KV):
        kg_t = kk_t[g * HEAD_DIM:(g + 1) * HEAD_DIM, :]
        vg = vv[:, g * HEAD_DIM:(g + 1) * HEAD_DIM]
        for hh in range(Q_PER_KV):
            h = g * Q_PER_KV + hh
            qh = q[:, h * HEAD_DIM:(h + 1) * HEAD_DIM].astype(BF16)
            s = jnp.dot(qh, kg_t, preferred_element_type=F32) * (HEAD_DIM ** -0.5)
            s = jnp.where(valid, s, NEG_INF)
            sink = sinks_ref[h]
            m = jnp.maximum(jnp.max(s, axis=1, keepdims=True), sink)
            p = jnp.exp(s - m)
            denom = jnp.sum(p, axis=1, keepdims=True) + jnp.exp(sink - m)
            p = (p / denom).astype(BF16)
            acc_ref[:, h * HEAD_DIM:(h + 1) * HEAD_DIM] = jnp.dot(p, vg, preferred_element_type=F32)
    o_ref[...] = _rms(acc_ref[...], g_ref[...]).astype(o_ref.dtype)


def _attention(sinks, q, k, v, g_attn_out, blocks_per_seq):
    n = q.shape[0]
    prev = pl.BlockSpec((WINDOW, KV_COLS), lambda j: (jnp.where(j % blocks_per_seq == 0, j, j - 1), 0))
    cur = pl.BlockSpec((WINDOW, KV_COLS), lambda j: (j, 0))
    kern = functools.partial(_attn_kernel, blocks_per_seq=blocks_per_seq)
    return pl.pallas_call(
        kern,
        grid=(n // WINDOW,),
        in_specs=[pl.BlockSpec(memory_space=pltpu.SMEM),
                  pl.BlockSpec((WINDOW, Q_COLS), lambda i: (i, 0)),
                  prev, cur, prev, cur,
                  _const_spec((1, ATTN_WIDTH))],
        out_specs=pl.BlockSpec((WINDOW, ATTN_WIDTH), lambda i: (i, 0)),
        out_shape=jax.ShapeDtypeStruct((n, ATTN_WIDTH), BF16),
        scratch_shapes=[pltpu.VMEM((WINDOW, ATTN_WIDTH), F32)],
        compiler_params=_params(VMEM_SMALL_MIB),
        name="attention",
    )(sinks, q, k, k, v, v, g_attn_out)


def _attn_sample_kernel(sinks_ref, q_ref, ck_ref, cv_ref, kn_ref, vn_ref, g_ref, o_ref, acc_ref, *, dseq, seqs):
    rows = Q_PER_KV * dseq
    r = lax.broadcasted_iota(jnp.int32, (rows, WINDOW), 0) % dseq
    c = lax.broadcasted_iota(jnp.int32, (rows, WINDOW), 1)
    valid_cache = c > r
    valid_new = c <= r
    head_of_row = lax.broadcasted_iota(jnp.int32, (rows, 1), 0) // dseq
    sink_cols = []
    for g in range(N_KV):
        col = jnp.zeros((rows, 1), F32)
        for hh in range(Q_PER_KV):
            col = jnp.where(head_of_row == hh, sinks_ref[g * Q_PER_KV + hh], col)
        sink_cols.append(col)
    pad = jnp.zeros((WINDOW - dseq, HEAD_DIM), F32)
    nt = (((1,), (1,)), ((), ()))
    for i in range(seqs):
        new = slice(i * dseq, (i + 1) * dseq)
        old = slice(i * WINDOW, (i + 1) * WINDOW)
        q = q_ref[new, :]
        for g in range(N_KV):
            cols = slice(g * HEAD_DIM, (g + 1) * HEAD_DIM)
            qg = jnp.concatenate([q[:, (g * Q_PER_KV + hh) * HEAD_DIM:(g * Q_PER_KV + hh + 1) * HEAD_DIM]
                                  for hh in range(Q_PER_KV)], axis=0).astype(BF16)
            k_new = jnp.concatenate([kn_ref[new, cols], pad], axis=0).astype(BF16)
            v_new = jnp.concatenate([vn_ref[new, cols], pad], axis=0).astype(BF16)
            s_old = lax.dot_general(qg, ck_ref[old, cols].astype(BF16), nt,
                                    preferred_element_type=F32) * (HEAD_DIM ** -0.5)
            s_new = lax.dot_general(qg, k_new, nt, preferred_element_type=F32) * (HEAD_DIM ** -0.5)
            s_old = jnp.where(valid_cache, s_old, NEG_INF)
            s_new = jnp.where(valid_new, s_new, NEG_INF)
            sink = sink_cols[g]
            m = jnp.maximum(jnp.maximum(jnp.max(s_old, axis=1, keepdims=True),
                                        jnp.max(s_new, axis=1, keepdims=True)), sink)
            p_old = jnp.exp(s_old - m)
            p_new = jnp.exp(s_new - m)
            denom = (jnp.sum(p_old, axis=1, keepdims=True) + jnp.sum(p_new, axis=1, keepdims=True)
                     + jnp.exp(sink - m))
            o = (jnp.dot((p_old / denom).astype(BF16), cv_ref[old, cols].astype(BF16),
                         preferred_element_type=F32)
                 + jnp.dot((p_new / denom).astype(BF16), v_new, preferred_element_type=F32))
            for hh in range(Q_PER_KV):
                h = g * Q_PER_KV + hh
                acc_ref[new, h * HEAD_DIM:(h + 1) * HEAD_DIM] = o[hh * dseq:(hh + 1) * dseq, :]
    o_ref[...] = _rms(acc_ref[...], g_ref[...]).astype(o_ref.dtype)


def _attention_sample(sinks, q, cache_k, cache_v, k_new, v_new, g_attn_out, dseq, seqs):
    n = q.shape[0]
    tok = lambda w: pl.BlockSpec((seqs * dseq, w), lambda i: (i, 0))
    cache = pl.BlockSpec((seqs * WINDOW, KV_COLS), lambda i: (i, 0))
    kern = functools.partial(_attn_sample_kernel, dseq=dseq, seqs=seqs)
    return pl.pallas_call(
        kern,
        grid=(n // (seqs * dseq),),
        in_specs=[pl.BlockSpec(memory_space=pltpu.SMEM), tok(Q_COLS), cache, cache, tok(KV_COLS), tok(KV_COLS),
                  _const_spec((1, ATTN_WIDTH))],
        out_specs=tok(ATTN_WIDTH),
        out_shape=jax.ShapeDtypeStruct((n, ATTN_WIDTH), BF16),
        scratch_shapes=[pltpu.VMEM((seqs * dseq, ATTN_WIDTH), F32)],
        compiler_params=_params(VMEM_SMALL_MIB),
        name="attention_sample",
    )(sinks, q, cache_k, cache_v, k_new, v_new, g_attn_out)


def _sgu_out_kernel(u_ref, vn_ref, wm_ref, bias_ref, g_ref, an_ref, wo_ref, x_ref, h_ref, s_ref):
    tb = u_ref.shape[0]
    for ch in range(tb // SGU_CHUNK):
        rows = slice(ch * SGU_CHUNK, (ch + 1) * SGU_CHUNK)
        for g in range(SGU_HEADS):
            cols = slice(g * SGU_GROUP, (g + 1) * SGU_GROUP)
            mix = jnp.dot(wm_ref[g], vn_ref[rows, cols].astype(BF16),
                          preferred_element_type=F32) + bias_ref[g]
            s_ref[rows, cols] = u_ref[rows, cols] * mix
    sn = _rms(s_ref[...], g_ref[...]).astype(BF16)
    mixed = jnp.dot(an_ref[...], wo_ref[:ATTN_WIDTH, :], preferred_element_type=F32)
    mixed = mixed + jnp.dot(sn, wo_ref[ATTN_WIDTH:, :], preferred_element_type=F32)
    h_ref[...] = x_ref[...] + mixed


def _sgu_out(u, vn, wm_bf, bias, g_sgu_out, an, w_out_bf, x, tb):
    n = x.shape[0]
    tok = lambda w: pl.BlockSpec((tb, w), lambda i: (i, 0))
    return pl.pallas_call(
        _sgu_out_kernel,
        grid=(n // tb,),
        in_specs=[tok(SGU_WIDTH), tok(SGU_WIDTH),
                  _const_spec((SGU_HEADS, SGU_CHUNK, SGU_CHUNK)),
                  _const_spec((SGU_HEADS, SGU_CHUNK, SGU_GROUP)),
                  _const_spec((1, SGU_WIDTH)), tok(ATTN_WIDTH),
                  _const_spec((D_MODEL, D_MODEL)), tok(D_MODEL)],
        out_specs=tok(D_MODEL),
        out_shape=jax.ShapeDtypeStruct((n, D_MODEL), F32),
        scratch_shapes=[pltpu.VMEM((tb, SGU_WIDTH), F32)],
        compiler_params=_params(VMEM_DENSE_MIB),
        name="sgu_out",
    )(u, vn, wm_bf, bias, g_sgu_out, an, w_out_bf, x)


def _odd_even_merge_sort_pairs(n):
    pairs = []

    def merge(lo, length, r):
        step = 2 * r
        if step < length:
            merge(lo, length, step)
            merge(lo + r, length, step)
            pairs.extend((i, i + r) for i in range(lo + r, lo + length - r, step))
        else:
            pairs.append((lo, lo + r))

    def sort(lo, length):
        if length > 1:
            half = length // 2
            sort(lo, half)
            sort(lo + half, half)
            merge(lo, length, 1)

    sort(0, n)
    return pairs


_BIG_ROW = 2 ** 30


def _top16_rows(s, tag=None, tag_bits=0):
    depth = s.shape[0] // SUBLANES
    row8 = lax.broadcasted_iota(jnp.int32, (SUBLANES, s.shape[1]), 0)
    vals = [s[d * SUBLANES:(d + 1) * SUBLANES, :] for d in range(depth)]
    pays = [(row8 + d * SUBLANES) << tag_bits for d in range(depth)]
    if tag is not None:
        pays = [p | tag[d * SUBLANES:(d + 1) * SUBLANES, :] for d, p in enumerate(pays)]
    for i, j in _odd_even_merge_sort_pairs(depth):
        a, b, pa, pb = vals[i], vals[j], pays[i], pays[j]
        swap = (b > a) | ((b == a) & (pb < pa))
        vals[i], vals[j] = jnp.where(swap, b, a), jnp.where(swap, a, b)
        pays[i], pays[j] = jnp.where(swap, pb, pa), jnp.where(swap, pa, pb)
    out_v, out_p = [], []
    for k in range(PEER_TOPK):
        head, head_p = vals[0], pays[0]
        m = jnp.max(head, axis=0, keepdims=True)
        p = jnp.min(jnp.where(head == m, head_p, _BIG_ROW), axis=0, keepdims=True)
        out_v.append(m)
        out_p.append(p)
        popped = head_p == p
        for d in range(min(depth, PEER_TOPK - 1 - k)):
            below_v, below_p = (vals[d + 1], pays[d + 1]) if d + 1 < depth else (-jnp.inf, _BIG_ROW)
            vals[d] = jnp.where(popped, below_v, vals[d])
            pays[d] = jnp.where(popped, below_p, pays[d])
    return jnp.concatenate(out_v, axis=0), jnp.concatenate(out_p, axis=0)


_CAND_WIDTHS = (16, 8, 8, 8, 4, 4, 4, 4)


def _staircase(a, b, combine):
    rows = [combine(a[i:i + 1, :], b[:w, :]) for i, w in enumerate(_CAND_WIDTHS)]
    rows.append(combine(a[len(_CAND_WIDTHS):, :], b[0:1, :]))
    return jnp.concatenate(rows, axis=0)


def _peer_topk_kernel(h_ref, g_ref, wq_ref, k1_ref, k2_ref, xn_ref, idx_ref, gate_ref):
    xn = _rms(h_ref[...], g_ref[...])
    xn_ref[...] = xn
    qp = jnp.dot(xn.astype(BF16), wq_ref[...], preferred_element_type=F32).astype(BF16)
    idx_parts, gate_parts = [], []
    nt = (((1,), (1,)), ((), ()))
    for h in range(PEER_HEADS):
        base = h * 2 * PEER_HALF
        s1 = lax.dot_general(k1_ref[h], qp[:, base:base + PEER_HALF], nt,
                             preferred_element_type=F32)
        s2 = lax.dot_general(k2_ref[h], qp[:, base + PEER_HALF:base + 2 * PEER_HALF], nt,
                             preferred_element_type=F32)
        t1, i1 = _top16_rows(s1)
        t2, i2 = _top16_rows(s2)
        cand = _staircase(t1, t2, lambda a, b: a + b)
        expert = _staircase(i1, i2, lambda a, b: a * PEER_NKEYS + b)
        expert_bits = (PEER_NKEYS * PEER_NKEYS - 1).bit_length()
        sc, code = _top16_rows(cand, expert, expert_bits)
        idx_parts.append(code & ((1 << expert_bits) - 1))
        ex = jnp.exp(sc - sc[0:1, :])
        gate_parts.append(ex / jnp.sum(ex, axis=0, keepdims=True))
    idx_t = jnp.concatenate(idx_parts, axis=0)
    gate_t = jnp.concatenate(gate_parts, axis=0)
    tb = h_ref.shape[0]
    for c in range(tb // LANES):
        idx_ref[c * LANES:(c + 1) * LANES, :] = idx_t[:, c * LANES:(c + 1) * LANES].T
        gate_ref[c * LANES:(c + 1) * LANES, :] = gate_t[:, c * LANES:(c + 1) * LANES].T


def _peer_topk(h, g_ffn, w_pq_bf, k1_bf, k2_bf, tb):
    n = h.shape[0]
    tok = lambda w: pl.BlockSpec((tb, w), lambda i: (i, 0))
    return pl.pallas_call(
        _peer_topk_kernel,
        grid=(n // tb,),
        in_specs=[tok(D_MODEL), _const_spec((1, D_MODEL)),
                  _const_spec((D_MODEL, PEER_HEADS * 2 * PEER_HALF)),
                  _const_spec((PEER_HEADS, PEER_NKEYS, PEER_HALF)),
                  _const_spec((PEER_HEADS, PEER_NKEYS, PEER_HALF))],
        out_specs=[tok(D_MODEL), tok(PEER_PICKS), tok(PEER_PICKS)],
        out_shape=[jax.ShapeDtypeStruct((n, D_MODEL), F32),
                   jax.ShapeDtypeStruct((n, PEER_PICKS), jnp.int32),
                   jax.ShapeDtypeStruct((n, PEER_PICKS), F32)],
        compiler_params=_params(VMEM_DENSE_MIB),
        name="peer_topk",
    )(h, g_ffn, w_pq_bf, k1_bf, k2_bf)


PEER_RING = 16
PEER_CHUNKS = D_MODEL // LANES
PEER_EGROUPS = PEER_PICKS // SUBLANES


def _pack_uv(u_tab, v_tab):
    n = u_tab.shape[0]
    return jnp.concatenate([u_tab.astype(BF16).reshape(n, PEER_CHUNKS, LANES),
                            v_tab.astype(BF16).reshape(n, PEER_CHUNKS, LANES)], axis=1)


def _peer_expert_kernel(idx_ref, nidx_ref, xn_ref, gate_ref, h_ref, tab_hbm, o_ref, *scratch, tokens):
    bufs, (hk_rep, hk_scr, sem) = scratch[:PEER_RING], scratch[PEER_RING:]

    def row_copy(slot, e, r):
        return pltpu.make_async_copy(tab_hbm.at[r], bufs[slot].at[e], sem.at[slot])

    def issue(src_idx_ref, t, slot):
        for e in range(PEER_PICKS):
            row_copy(slot, e, src_idx_ref[t, e]).start(priority=e % 2)

    def wait(slot):
        for e in range(PEER_PICKS):
            row_copy(slot, e, 0).wait()

    sub = lax.broadcasted_iota(jnp.int32, (SUBLANES, LANES), 0)
    lane_minus_sub = lax.broadcasted_iota(jnp.int32, (SUBLANES, LANES), 1) - sub

    def fold_sublanes(parts):
        step = SUBLANES // 2
        while step >= 1:
            lower = (sub & step) == 0
            nxt = []
            for j in range(len(parts) // 2):
                a, b = parts[(j // step) * 2 * step + j % step], parts[(j // step) * 2 * step + j % step + step]
                nxt.append(jnp.where(lower, a, pltpu.roll(b, step, 0))
                           + jnp.where(lower, pltpu.roll(a, SUBLANES - step, 0), b))
            parts = nxt
            step //= 2
        return parts[0]

    def u_pass(t, slot):
        row = pl.ds(t, 1)
        x_row = xn_ref[row, :]
        x_lo = jnp.concatenate([x_row[:, c * LANES:(c + 1) * LANES] for c in range(SUBLANES)], axis=0)
        x_hi = jnp.concatenate([x_row[:, c * LANES:(c + 1) * LANES] for c in range(SUBLANES, PEER_CHUNKS)], axis=0)
        a_diag = jnp.zeros((SUBLANES, LANES), F32)
        for g in range(PEER_EGROUPS):
            parts = []
            for j in range(SUBLANES):
                u = bufs[slot][g * SUBLANES + j, :PEER_CHUNKS, :].astype(F32)
                parts.append(u[:SUBLANES] * x_lo + u[SUBLANES:] * x_hi)
            a_g = jnp.sum(fold_sublanes(parts), axis=1, keepdims=True)
            a_diag = jnp.where(lane_minus_sub == g * SUBLANES, a_g, a_diag)
        a_row = jnp.sum(a_diag, axis=0, keepdims=True)
        hk_scr[slot:slot + 1, :] = _gelu(a_row) * gate_ref[row, :]

    def v_pass(t, slot):
        row = pl.ds(t, 1)
        hk_row = jnp.broadcast_to(hk_scr[slot:slot + 1, :], (SUBLANES, LANES))
        for g in range(PEER_EGROUPS):
            hk_rep[g * SUBLANES:(g + 1) * SUBLANES, :] = jnp.broadcast_to(
                jnp.sum(jnp.where(lane_minus_sub == g * SUBLANES, hk_row, 0.0), axis=1, keepdims=True),
                (SUBLANES, LANES))
        n_chains = 4
        lo, hi = [None] * n_chains, [None] * n_chains
        for e in range(PEER_PICKS):
            v = bufs[slot][e, PEER_CHUNKS:, :].astype(F32)
            hk_e = jnp.broadcast_to(hk_rep[e:e + 1, :], (SUBLANES, LANES))
            p_lo = v[:SUBLANES] * hk_e
            p_hi = v[SUBLANES:] * hk_e
            i = e % n_chains
            lo[i] = p_lo if lo[i] is None else lo[i] + p_lo
            hi[i] = p_hi if hi[i] is None else hi[i] + p_hi
        out_lo = (lo[0] + lo[1]) + (lo[2] + lo[3])
        out_hi = (hi[0] + hi[1]) + (hi[2] + hi[3])
        out_row = jnp.concatenate([out_lo[c:c + 1, :] for c in range(SUBLANES)]
                                  + [out_hi[c:c + 1, :] for c in range(SUBLANES)], axis=1)
        o_ref[row, :] = h_ref[row, :] + out_row

    step, last_step = pl.program_id(0), pl.num_programs(0) - 1
    ahead = PEER_RING - 2

    def phase(k, s, do_v):
        wait(s)
        j, slot = k + ahead, (s + ahead) % PEER_RING
        if isinstance(j, int) and j >= tokens:
            @pl.when(step < last_step)
            def _():
                issue(nidx_ref, j - tokens, slot)
        else:
            issue(idx_ref, j, slot)
        u_pass(k, s)
        if do_v:
            v_pass(k - 1, (s - 1) % PEER_RING)

    def static_group(g):
        for s in range(PEER_RING):
            k = g * PEER_RING + s
            phase(k, s, k >= 1)

    def group(g, carry):
        for s in range(PEER_RING):
            phase(g * PEER_RING + s, s, True)
        return carry

    @pl.when(step == 0)
    def _():
        for s in range(ahead):
            issue(idx_ref, s, s)

    n_groups = tokens // PEER_RING
    static_group(0)
    if n_groups > 1:
        lax.fori_loop(1, n_groups - 1, group, 0)
        static_group(n_groups - 1)
    v_pass(tokens - 1, (tokens - 1) % PEER_RING)


def _peer_expert(idx, xn, gate, h, tab, tokens):
    n = h.shape[0]
    assert tokens % PEER_RING == 0 and n % tokens == 0
    tok = lambda w: pl.BlockSpec((tokens, w), lambda i: (i, 0))
    kern = functools.partial(_peer_expert_kernel, tokens=tokens)
    steps = n // tokens
    return pl.pallas_call(
        kern,
        grid=(steps,),
        in_specs=[pl.BlockSpec((tokens, PEER_PICKS), lambda i: (i, 0), memory_space=pltpu.SMEM),
                  pl.BlockSpec((tokens, PEER_PICKS), lambda i: (jnp.minimum(i + 1, steps - 1), 0),
                               memory_space=pltpu.SMEM),
                  tok(D_MODEL), tok(PEER_PICKS), tok(D_MODEL),
                  pl.BlockSpec(memory_space=pl.ANY)],
        out_specs=tok(D_MODEL),
        out_shape=jax.ShapeDtypeStruct((n, D_MODEL), F32),
        scratch_shapes=[pltpu.VMEM((PEER_PICKS, 2 * PEER_CHUNKS, LANES), BF16)
                        for _ in range(PEER_RING)] + [
                        pltpu.VMEM((PEER_PICKS, LANES), F32),
                        pltpu.VMEM((PEER_RING, PEER_PICKS), F32),
                        pltpu.SemaphoreType.DMA((PEER_RING,))],
        compiler_params=_params(VMEM_DENSE_MIB),
        name="peer_expert",
    )(idx, idx, xn, gate, h, tab)


def _final_kernel(h_ref, p_ref, gg_ref, wg_ref, wp_ref, gp_ref, gf_ref, y_ref):
    h = h_ref[...]
    gate = jax.nn.sigmoid(jnp.dot(_rms(h, gg_ref[...]).astype(BF16), wg_ref[...],
                                  preferred_element_type=F32))
    ple = jnp.dot(p_ref[...].astype(BF16), wp_ref[...], preferred_element_type=F32)
    h = h + _rms(ple, gp_ref[...]) * gate
    y_ref[...] = _rms(h, gf_ref[...])


def _final(h, p, g_gate, w_gate_bf, w_ple_bf, g_ple, g_final, tb):
    n = h.shape[0]
    tok = lambda w: pl.BlockSpec((tb, w), lambda i: (i, 0))
    return pl.pallas_call(
        _final_kernel,
        grid=(n // tb,),
        in_specs=[tok(D_MODEL), tok(PLE_DIM), _const_spec((1, D_MODEL)),
                  _const_spec((D_MODEL, D_MODEL)), _const_spec((PLE_DIM, D_MODEL)),
                  _const_spec((1, D_MODEL)), _const_spec((1, D_MODEL))],
        out_specs=tok(D_MODEL),
        out_shape=jax.ShapeDtypeStruct((n, D_MODEL), F32),
        compiler_params=_params(VMEM_DENSE_MIB),
        name="final",
    )(h, p, g_gate, w_gate_bf, w_ple_bf, g_ple, g_final)


def _rope_tables(pos):
    half = HEAD_DIM // 2
    inv_freq = 1.0 / (ROPE_THETA ** (jnp.arange(0, HEAD_DIM, 2, dtype=F32) / HEAD_DIM))
    ang = pos.astype(F32)[:, None] * inv_freq[None, :]
    cos, sin = jnp.cos(ang), jnp.sin(ang)
    reps = LANES // HEAD_DIM
    return (jnp.tile(jnp.concatenate([cos, cos], axis=1), (1, reps)),
            jnp.tile(jnp.concatenate([-sin, sin], axis=1), (1, reps)))


def _row(v):
    return v.reshape(1, -1).astype(F32)


def _token_stages(x, p, cos_t, sin_t, attn_fn, wm_bf, bias, lw, tb):
    q, k, v, u, vn = _in_proj(x, lw["g_mix"], lw["w_in"], cos_t, sin_t, lw["g_sgu"], lw["b_sgu"], tb)
    an = attn_fn(q, k, v)
    h = _sgu_out(u, vn, wm_bf, bias, lw["g_sgu_out"], an, lw["w_out"], x, tb)
    xn, idx, gate = _peer_topk(h, lw["g_ffn"], lw["w_pq"], lw["sub_k1"], lw["sub_k2"], tb)
    h = _peer_expert(idx, xn, gate, h, lw["uv_tab"], min(256, x.shape[0]))
    return h, k, v, vn


def kernel(x_prompt, x_sample, cache_k, cache_v, p_prompt, p_sample, g_mix, w_in, sinks, g_sgu, b_sgu,
           w_s, b_s, g_attn_out, g_sgu_out, w_out, g_ffn, w_pq, sub_k1, sub_k2, u_tab, v_tab, g_gate,
           w_gate, w_ple, g_ple, g_final):
    depth = w_in.shape[0]
    bsz, seq, _ = x_prompt.shape
    dbsz, dseq, _ = x_sample.shape
    wc = cache_k.shape[2]
    assert depth == 1, "the final kernel fuses the layer tail with the final norm"
    assert seq % WINDOW == 0 and wc == WINDOW and WINDOW % dseq == 0 and dseq <= SGU_CHUNK
    n_p, n_s = bsz * seq, dbsz * dseq
    tb_p = 256 if seq % 256 == 0 else WINDOW
    tb_s = WINDOW
    assert n_s % tb_s == 0 and seq % tb_p == 0
    nb = seq // WINDOW

    cos_p, sin_p = _rope_tables(jnp.arange(seq, dtype=jnp.int32))
    cos_s, sin_s = _rope_tables(PAST_LEN + (jnp.arange(tb_s, dtype=jnp.int32) % dseq))
    causal = jnp.tril(jnp.ones((SGU_CHUNK, SGU_CHUNK), F32))
    eye = jnp.eye(SGU_CHUNK // dseq, dtype=F32)

    hp = x_prompt.reshape(n_p, D_MODEL)
    hs = x_sample.reshape(n_s, D_MODEL)
    kp_l, vp_l, ks_l, vs_l, gs_l = [], [], [], [], []
    for i in range(depth):
        lw = dict(
            g_mix=_row(g_mix[i]), w_in=w_in[i].astype(BF16), g_sgu=_row(g_sgu[i]), b_sgu=_row(b_sgu[i]),
            g_sgu_out=_row(g_sgu_out[i]), w_out=w_out[i].astype(BF16), g_ffn=_row(g_ffn[i]),
            w_pq=w_pq[i].astype(BF16), sub_k1=sub_k1[i].astype(BF16), sub_k2=sub_k2[i].astype(BF16),
            uv_tab=_pack_uv(u_tab[i], v_tab[i]))
        sinks_i = sinks[i].astype(F32)
        g_ao = _row(g_attn_out[i])
        ws = w_s[i].astype(F32) * causal
        wm_p = ws.astype(BF16)
        bias_p = jnp.broadcast_to(b_s[i].astype(F32)[:, :, None], (SGU_HEADS, SGU_CHUNK, SGU_GROUP))
        wm_s = jnp.einsum("ab,gts->gatbs", eye, ws[:, :dseq, :dseq]).reshape(
            SGU_HEADS, SGU_CHUNK, SGU_CHUNK).astype(BF16)
        bias_s = jnp.broadcast_to(jnp.tile(b_s[i].astype(F32)[:, :dseq], (1, SGU_CHUNK // dseq))[:, :, None],
                                  (SGU_HEADS, SGU_CHUNK, SGU_GROUP))

        def attn_prompt(q, k, v):
            return _attention(sinks_i, q, k, v, g_ao, nb)

        def attn_sample(q, k, v, ck=cache_k[i], cv=cache_v[i]):
            return _attention_sample(sinks_i, q, ck.reshape(dbsz * wc, KV_COLS), cv.reshape(dbsz * wc, KV_COLS),
                                     k, v, g_ao, dseq, SGU_CHUNK // dseq)

        hp, kp, vp, _ = _token_stages(hp, None, cos_p, sin_p, attn_prompt, wm_p, bias_p, lw, tb_p)
        hs, ks, vs, vns = _token_stages(hs, None, cos_s, sin_s, attn_sample, wm_s, bias_s, lw, tb_s)
        fin = (_row(g_gate[i]), w_gate[i].astype(BF16), w_ple[i].astype(BF16), _row(g_ple[i]),
               _row(g_final))
        hp = _final(hp, p_prompt[i].reshape(n_p, PLE_DIM), *fin, tb_p)
        hs = _final(hs, p_sample[i].reshape(n_s, PLE_DIM), *fin, tb_s)

        tail = lambda a: a.reshape(bsz, seq, KV_COLS)[:, seq - WINDOW:].reshape(bsz, WINDOW, N_KV, HEAD_DIM)
        kp_l.append(tail(kp))
        vp_l.append(tail(vp))
        ks_l.append(jnp.concatenate([cache_k[i][:, dseq:], ks.reshape(dbsz, dseq, N_KV, HEAD_DIM)], axis=1))
        vs_l.append(jnp.concatenate([cache_v[i][:, dseq:], vs.reshape(dbsz, dseq, N_KV, HEAD_DIM)], axis=1))
        gs_l.append(vns.reshape(dbsz, dseq, SGU_HEADS, SGU_GROUP))

    return (hp.reshape(bsz, seq, D_MODEL), hs.reshape(dbsz, dseq, D_MODEL),
            jnp.stack(kp_l), jnp.stack(vp_l), jnp.stack(ks_l), jnp.stack(vs_l), jnp.stack(gs_l))
```

```python
import functools

import jax
import jax.numpy as jnp
from jax import lax
from jax.experimental import pallas as pl
from jax.experimental.pallas import tpu as pltpu

F32 = jnp.float32
BF16 = jnp.bfloat16

D_MODEL = 2048
PAST_LEN = 16384
HEAD_DIM = 64
N_HEADS = 16
N_KV = 2
Q_PER_KV = N_HEADS // N_KV
WINDOW = 128
ROPE_THETA = 10000.0
NEG_INF = -1e30
ATTN_WIDTH = N_HEADS * HEAD_DIM
SGU_CHUNK = 128
SGU_GROUP = 128
SGU_HEADS = 8
SGU_WIDTH = SGU_HEADS * SGU_GROUP
Q_COLS = N_HEADS * HEAD_DIM
KV_COLS = N_KV * HEAD_DIM
IN_COLS = Q_COLS + 2 * KV_COLS + 2 * SGU_WIDTH
PEER_HEADS = 8
PEER_NKEYS = 128
PEER_HALF = 128
PEER_TOPK = 16
PEER_PICKS = PEER_HEADS * PEER_TOPK
PLE_DIM = 256
EPS = 1e-6

LANES = 128
SUBLANES = 8
MIB = 1024 * 1024
VMEM_DENSE_MIB = 48
VMEM_SMALL_MIB = 32


def _rms(x, g):
    return x * lax.rsqrt(jnp.mean(x * x, axis=-1, keepdims=True) + EPS) * g


def _gelu(x):
    return 0.5 * x * (1.0 + lax.erf(x * (0.5 ** 0.5)))


def _const_spec(shape):
    nd = len(shape)
    return pl.BlockSpec(shape, lambda *_: (0,) * nd, pipeline_mode=pl.Buffered(1))


def _params(vmem_mib):
    return pltpu.CompilerParams(dimension_semantics=("arbitrary",), vmem_limit_bytes=vmem_mib * MIB)


def _in_proj_kernel(x_ref, g_ref, w_ref, cos_ref, sin_ref, gs_ref, bs_ref,
                    q_ref, k_ref, v_ref, u_ref, vn_ref):
    xn = _rms(x_ref[...], g_ref[...]).astype(BF16)
    z = jnp.dot(xn, w_ref[...], preferred_element_type=F32)
    cos = cos_ref[...]
    sin = sin_ref[...]
    lane = lax.broadcasted_iota(jnp.int32, cos.shape, 1)
    first_half = (lane % HEAD_DIM) < (HEAD_DIM // 2)

    def rope(zc):
        swapped = jnp.where(first_half,
                            pltpu.roll(zc, LANES - HEAD_DIM // 2, 1),
                            pltpu.roll(zc, HEAD_DIM // 2, 1))
        return zc * cos + swapped * sin

    for c in range(Q_COLS // LANES):
        q_ref[:, c * LANES:(c + 1) * LANES] = rope(z[:, c * LANES:(c + 1) * LANES])
    k_ref[...] = rope(z[:, Q_COLS:Q_COLS + KV_COLS])
    v_ref[...] = z[:, Q_COLS + KV_COLS:Q_COLS + 2 * KV_COLS]
    u0 = Q_COLS + 2 * KV_COLS
    u_ref[...] = _gelu(z[:, u0:u0 + SGU_WIDTH])
    sv = _gelu(z[:, u0 + SGU_WIDTH:u0 + 2 * SGU_WIDTH])
    mu = jnp.mean(sv, axis=-1, keepdims=True)
    svc = sv - mu
    var = jnp.mean(svc * svc, axis=-1, keepdims=True)
    vn_ref[...] = svc * lax.rsqrt(var + EPS) * gs_ref[...] + bs_ref[...]


def _in_proj(x, g_mix, w_in_bf, cos_t, sin_t, g_sgu, b_sgu, tb):
    n = x.shape[0]
    period_blocks = cos_t.shape[0] // tb
    tok = lambda w: pl.BlockSpec((tb, w), lambda i: (i, 0))
    tab = pl.BlockSpec((tb, LANES), lambda i: (i % period_blocks, 0))
    return pl.pallas_call(
        _in_proj_kernel,
        grid=(n // tb,),
        in_specs=[tok(D_MODEL), _const_spec((1, D_MODEL)), _const_spec((D_MODEL, IN_COLS)),
                  tab, tab, _const_spec((1, SGU_WIDTH)), _const_spec((1, SGU_WIDTH))],
        out_specs=[tok(Q_COLS), tok(KV_COLS), tok(KV_COLS), tok(SGU_WIDTH), tok(SGU_WIDTH)],
        out_shape=[jax.ShapeDtypeStruct((n, w), F32)
                   for w in (Q_COLS, KV_COLS, KV_COLS, SGU_WIDTH, SGU_WIDTH)],
        compiler_params=_params(VMEM_DENSE_MIB),
        name="in_proj",
    )(x, g_mix, w_in_bf, cos_t, sin_t, g_sgu, b_sgu)


def _attn_kernel(sinks_ref, q_ref, ka_ref, kb_ref, va_ref, vb_ref, g_ref, o_ref, acc_ref, *, blocks_per_seq):
    q = q_ref[...]
    kk_t_f32 = jnp.concatenate([ka_ref[...].T, kb_ref[...].T], axis=1)
    vv = jnp.concatenate([va_ref[...], vb_ref[...]], axis=0).astype(BF16)
    r = lax.broadcasted_iota(jnp.int32, (WINDOW, 2 * WINDOW), 0)
    c = lax.broadcasted_iota(jnp.int32, (WINDOW, 2 * WINDOW), 1)
    first = pl.program_id(0) % blocks_per_seq == 0
    valid = (c > jnp.where(first, jnp.maximum(r, WINDOW - 1), r)) & (c <= r + WINDOW)
    kk_t = kk_t_f32.astype(BF16)
    for g in range(N_KV):
        kg_t = kk_t[g * HEAD_DIM:(g + 1) * HEAD_DIM, :]
        vg = vv[:, g * HEAD_DIM:(g + 1) * HEAD_DIM]
        for hh in range(Q_PER_KV):
            h = g * Q_PER_KV + hh
            qh = q[:, h * HEAD_DIM:(h + 1) * HEAD_DIM].astype(BF16)
            s = jnp.dot(qh, kg_t, preferred_element_type=F32) * (HEAD_DIM ** -0.5)
            s = jnp.where(valid, s, NEG_INF)
            sink = sinks_ref[h]
            m = jnp.maximum(jnp.max(s, axis=1, keepdims=True), sink)
            p = jnp.exp(s - m)
            denom = jnp.sum(p, axis=1, keepdims=True) + jnp.exp(sink - m)
            p = (p / denom).astype(BF16)
            acc_ref[:, h * HEAD_DIM:(h + 1) * HEAD_DIM] = jnp.dot(p, vg, preferred_element_type=F32)
    o_ref[...] = _rms(acc_ref[...], g_ref[...]).astype(o_ref.dtype)


def _attention(sinks, q, k, v, g_attn_out, blocks_per_seq):
    n = q.shape[0]
    prev = pl.BlockSpec((WINDOW, KV_COLS), lambda j: (jnp.where(j % blocks_per_seq == 0, j, j - 1), 0))
    cur = pl.BlockSpec((WINDOW, KV_COLS), lambda j: (j, 0))
    kern = functools.partial(_attn_kernel, blocks_per_seq=blocks_per_seq)
    return pl.pallas_call(
        kern,
        grid=(n // WINDOW,),
        in_specs=[pl.BlockSpec(memory_space=pltpu.SMEM),
                  pl.BlockSpec((WINDOW, Q_COLS), lambda i: (i, 0)),
                  prev, cur, prev, cur,
                  _const_spec((1, ATTN_WIDTH))],
        out_specs=pl.BlockSpec((WINDOW, ATTN_WIDTH), lambda i: (i, 0)),
        out_shape=jax.ShapeDtypeStruct((n, ATTN_WIDTH), BF16),
        scratch_shapes=[pltpu.VMEM((WINDOW, ATTN_WIDTH), F32)],
        compiler_params=_params(VMEM_SMALL_MIB),
        name="attention",
    )(sinks, q, k, k, v, v, g_attn_out)


def _attn_sample_kernel(sinks_ref, q_ref, ck_ref, cv_ref, kn_ref, vn_ref, g_ref, o_ref, acc_ref, *, dseq, seqs):
    rows = Q_PER_KV * dseq
    r = lax.broadcasted_iota(jnp.int32, (rows, WINDOW), 0) % dseq
    c = lax.broadcasted_iota(jnp.int32, (rows, WINDOW), 1)
    valid_cache = c > r
    valid_new = c <= r
    head_of_row = lax.broadcasted_iota(jnp.int32, (rows, 1), 0) // dseq
    sink_cols = []
    for g in range(N_KV):
        col = jnp.zeros((rows, 1), F32)
        for hh in range(Q_PER_KV):
            col = jnp.where(head_of_row == hh, sinks_ref[g * Q_PER_KV + hh], col)
        sink_cols.append(col)
    pad = jnp.zeros((WINDOW - dseq, HEAD_DIM), F32)
    nt = (((1,), (1,)), ((), ()))
    for i in range(seqs):
        new = slice(i * dseq, (i + 1) * dseq)
        old = slice(i * WINDOW, (i + 1) * WINDOW)
        q = q_ref[new, :]
        for g in range(N_KV):
            cols = slice(g * HEAD_DIM, (g + 1) * HEAD_DIM)
            qg = jnp.concatenate([q[:, (g * Q_PER_KV + hh) * HEAD_DIM:(g * Q_PER_KV + hh + 1) * HEAD_DIM]
                                  for hh in range(Q_PER_KV)], axis=0).astype(BF16)
            k_new = jnp.concatenate([kn_ref[new, cols], pad], axis=0).astype(BF16)
            v_new = jnp.concatenate([vn_ref[new, cols], pad], axis=0).astype(BF16)
            s_old = lax.dot_general(qg, ck_ref[old, cols].astype(BF16), nt,
                                    preferred_element_type=F32) * (HEAD_DIM ** -0.5)
            s_new = lax.dot_general(qg, k_new, nt, preferred_element_type=F32) * (HEAD_DIM ** -0.5)
            s_old = jnp.where(valid_cache, s_old, NEG_INF)
            s_new = jnp.where(valid_new, s_new, NEG_INF)
            sink = sink_cols[g]
            m = jnp.maximum(jnp.maximum(jnp.max(s_old, axis=1, keepdims=True),
                                        jnp.max(s_new, axis=1, keepdims=True)), sink)
            p_old = jnp.exp(s_old - m)
            p_new = jnp.exp(s_new - m)
            denom = (jnp.sum(p_old, axis=1, keepdims=True) + jnp.sum(p_new, axis=1, keepdims=True)
                     + jnp.exp(sink - m))
            o = (jnp.dot((p_old / denom).astype(BF16), cv_ref[old, cols].astype(BF16),
                         preferred_element_type=F32)
                 + jnp.dot((p_new / denom).astype(BF16), v_new, preferred_element_type=F32))
            for hh in range(Q_PER_KV):
                h = g * Q_PER_KV + hh
                acc_ref[new, h * HEAD_DIM:(h + 1) * HEAD_DIM] = o[hh * dseq:(hh + 1) * dseq, :]
    o_ref[...] = _rms(acc_ref[...], g_ref[...]).astype(o_ref.dtype)


def _attention_sample(sinks, q, cache_k, cache_v, k_new, v_new, g_attn_out, dseq, seqs):
    n = q.shape[0]
    tok = lambda w: pl.BlockSpec((seqs * dseq, w), lambda i: (i, 0))
    cache = pl.BlockSpec((seqs * WINDOW, KV_COLS), lambda i: (i, 0))
    kern = functools.partial(_attn_sample_kernel, dseq=dseq, seqs=seqs)
    return pl.pallas_call(
        kern,
        grid=(n // (seqs * dseq),),
        in_specs=[pl.BlockSpec(memory_space=pltpu.SMEM), tok(Q_COLS), cache, cache, tok(KV_COLS), tok(KV_COLS),
                  _const_spec((1, ATTN_WIDTH))],
        out_specs=tok(ATTN_WIDTH),
        out_shape=jax.ShapeDtypeStruct((n, ATTN_WIDTH), BF16),
        scratch_shapes=[pltpu.VMEM((seqs * dseq, ATTN_WIDTH), F32)],
        compiler_params=_params(VMEM_SMALL_MIB),
        name="attention_sample",
    )(sinks, q, cache_k, cache_v, k_new, v_new, g_attn_out)


def _sgu_out_kernel(u_ref, vn_ref, wm_ref, bias_ref, g_ref, an_ref, wo_ref, x_ref, h_ref, s_ref):
    tb = u_ref.shape[0]
    for ch in range(tb // SGU_CHUNK):
        rows = slice(ch * SGU_CHUNK, (ch + 1) * SGU_CHUNK)
        for g in range(SGU_HEADS):
            cols = slice(g * SGU_GROUP, (g + 1) * SGU_GROUP)
            mix = jnp.dot(wm_ref[g], vn_ref[rows, cols].astype(BF16),
                          preferred_element_type=F32) + bias_ref[g]
            s_ref[rows, cols] = u_ref[rows, cols] * mix
    sn = _rms(s_ref[...], g_ref[...]).astype(BF16)
    mixed = jnp.dot(an_ref[...], wo_ref[:ATTN_WIDTH, :], preferred_element_type=F32)
    mixed = mixed + jnp.dot(sn, wo_ref[ATTN_WIDTH:, :], preferred_element_type=F32)
    h_ref[...] = x_ref[...] + mixed


def _sgu_out(u, vn, wm_bf, bias, g_sgu_out, an, w_out_bf, x, tb):
    n = x.shape[0]
    tok = lambda w: pl.BlockSpec((tb, w), lambda i: (i, 0))
    return pl.pallas_call(
        _sgu_out_kernel,
        grid=(n // tb,),
        in_specs=[tok(SGU_WIDTH), tok(SGU_WIDTH),
                  _const_spec((SGU_HEADS, SGU_CHUNK, SGU_CHUNK)),
                  _const_spec((SGU_HEADS, SGU_CHUNK, SGU_GROUP)),
                  _const_spec((1, SGU_WIDTH)), tok(ATTN_WIDTH),
                  _const_spec((D_MODEL, D_MODEL)), tok(D_MODEL)],
        out_specs=tok(D_MODEL),
        out_shape=jax.ShapeDtypeStruct((n, D_MODEL), F32),
        scratch_shapes=[pltpu.VMEM((tb, SGU_WIDTH), F32)],
        compiler_params=_params(VMEM_DENSE_MIB),
        name="sgu_out",
    )(u, vn, wm_bf, bias, g_sgu_out, an, w_out_bf, x)


def _odd_even_merge_sort_pairs(n):
    pairs = []

    def merge(lo, length, r):
        step = 2 * r
        if step < length:
            merge(lo, length, step)
            merge(lo + r, length, step)
            pairs.extend((i, i + r) for i in range(lo + r, lo + length - r, step))
        else:
            pairs.append((lo, lo + r))

    def sort(lo, length):
        if length > 1:
            half = length // 2
            sort(lo, half)
            sort(lo + half, half)
            merge(lo, length, 1)

    sort(0, n)
    return pairs


_BIG_ROW = 2 ** 30


def _top16_rows(s, tag=None, tag_bits=0):
    depth = s.shape[0] // SUBLANES
    row8 = lax.broadcasted_iota(jnp.int32, (SUBLANES, s.shape[1]), 0)
    vals = [s[d * SUBLANES:(d + 1) * SUBLANES, :] for d in range(depth)]
    pays = [(row8 + d * SUBLANES) << tag_bits for d in range(depth)]
    if tag is not None:
        pays = [p | tag[d * SUBLANES:(d + 1) * SUBLANES, :] for d, p in enumerate(pays)]
    for i, j in _odd_even_merge_sort_pairs(depth):
        a, b, pa, pb = vals[i], vals[j], pays[i], pays[j]
        swap = (b > a) | ((b == a) & (pb < pa))
        vals[i], vals[j] = jnp.where(swap, b, a), jnp.where(swap, a, b)
        pays[i], pays[j] = jnp.where(swap, pb, pa), jnp.where(swap, pa, pb)
    out_v, out_p = [], []
    for k in range(PEER_TOPK):
        head, head_p = vals[0], pays[0]
        m = jnp.max(head, axis=0, keepdims=True)
        p = jnp.min(jnp.where(head == m, head_p, _BIG_ROW), axis=0, keepdims=True)
        out_v.append(m)
        out_p.append(p)
        popped = head_p == p
        for d in range(min(depth, PEER_TOPK - 1 - k)):
            below_v, below_p = (vals[d + 1], pays[d + 1]) if d + 1 < depth else (-jnp.inf, _BIG_ROW)
            vals[d] = jnp.where(popped, below_v, vals[d])
            pays[d] = jnp.where(popped, below_p, pays[d])
    return jnp.concatenate(out_v, axis=0), jnp.concatenate(out_p, axis=0)


_CAND_WIDTHS = (16, 8, 8, 8, 4, 4, 4, 4)


def _staircase(a, b, combine):
    rows = [combine(a[i:i + 1, :], b[:w, :]) for i, w in enumerate(_CAND_WIDTHS)]
    rows.append(combine(a[len(_CAND_WIDTHS):, :], b[0:1, :]))
    return jnp.concatenate(rows, axis=0)


def _peer_topk_kernel(h_ref, g_ref, wq_ref, k1_ref, k2_ref, xn_ref, idx_ref, gate_ref):
    xn = _rms(h_ref[...], g_ref[...])
    xn_ref[...] = xn
    qp = jnp.dot(xn.astype(BF16), wq_ref[...], preferred_element_type=F32).astype(BF16)
    idx_parts, gate_parts = [], []
    nt = (((1,), (1,)), ((), ()))
    for h in range(PEER_HEADS):
        base = h * 2 * PEER_HALF
        s1 = lax.dot_general(k1_ref[h], qp[:, base:base + PEER_HALF], nt,
                             preferred_element_type=F32)
        s2 = lax.dot_general(k2_ref[h], qp[:, base + PEER_HALF:base + 2 * PEER_HALF], nt,
                             preferred_element_type=F32)
        t1, i1 = _top16_rows(s1)
        t2, i2 = _top16_rows(s2)
        cand = _staircase(t1, t2, lambda a, b: a + b)
        expert = _staircase(i1, i2, lambda a, b: a * PEER_NKEYS + b)
        expert_bits = (PEER_NKEYS * PEER_NKEYS - 1).bit_length()
        sc, code = _top16_rows(cand, expert, expert_bits)
        idx_parts.append(code & ((1 << expert_bits) - 1))
        ex = jnp.exp(sc - sc[0:1, :])
        gate_parts.append(ex / jnp.sum(ex, axis=0, keepdims=True))
    idx_t = jnp.concatenate(idx_parts, axis=0)
    gate_t = jnp.concatenate(gate_parts, axis=0)
    tb = h_ref.shape[0]
    for c in range(tb // LANES):
        idx_ref[c * LANES:(c + 1) * LANES, :] = idx_t[:, c * LANES:(c + 1) * LANES].T
        gate_ref[c * LANES:(c + 1) * LANES, :] = gate_t[:, c * LANES:(c + 1) * LANES].T


def _peer_topk(h, g_ffn, w_pq_bf, k1_bf, k2_bf, tb):
    n = h.shape[0]
    tok = lambda w: pl.BlockSpec((tb, w), lambda i: (i, 0))
    return pl.pallas_call(
        _peer_topk_kernel,
        grid=(n // tb,),
        in_specs=[tok(D_MODEL), _const_spec((1, D_MODEL)),
                  _const_spec((D_MODEL, PEER_HEADS * 2 * PEER_HALF)),
                  _const_spec((PEER_HEADS, PEER_NKEYS, PEER_HALF)),
                  _const_spec((PEER_HEADS, PEER_NKEYS, PEER_HALF))],
        out_specs=[tok(D_MODEL), tok(PEER_PICKS), tok(PEER_PICKS)],
        out_shape=[jax.ShapeDtypeStruct((n, D_MODEL), F32),
                   jax.ShapeDtypeStruct((n, PEER_PICKS), jnp.int32),
                   jax.ShapeDtypeStruct((n, PEER_PICKS), F32)],
        compiler_params=_params(VMEM_DENSE_MIB),
        name="peer_topk",
    )(h, g_ffn, w_pq_bf, k1_bf, k2_bf)


PEER_RING = 16
PEER_CHUNKS = D_MODEL // LANES
PEER_EGROUPS = PEER_PICKS // SUBLANES


U_MASK = 0xFFFF0000


def _pack_uv(u_tab, v_tab):
    hi = lax.bitcast_convert_type(u_tab.astype(BF16), jnp.uint16).astype(jnp.uint32) << 16
    lo = lax.bitcast_convert_type(v_tab.astype(BF16), jnp.uint16).astype(jnp.uint32)
    return (hi | lo).reshape(u_tab.shape[0], PEER_CHUNKS, LANES)


def _peer_expert_kernel(idx_ref, nidx_ref, xn_ref, gate_ref, h_ref, tab_hbm, o_ref, *scratch, tokens):
    bufs, (hk_rep, hk_scr, sem) = scratch[:PEER_RING], scratch[PEER_RING:]

    def row_copy(slot, e, r):
        return pltpu.make_async_copy(tab_hbm.at[r], bufs[slot].at[e], sem.at[slot])

    def issue(src_idx_ref, t, slot):
        for e in range(PEER_PICKS):
            row_copy(slot, e, src_idx_ref[t, e]).start(priority=e % 2)

    def wait(slot):
        for e in range(PEER_PICKS):
            row_copy(slot, e, 0).wait()

    sub = lax.broadcasted_iota(jnp.int32, (SUBLANES, LANES), 0)
    lane_minus_sub = lax.broadcasted_iota(jnp.int32, (SUBLANES, LANES), 1) - sub

    def fold_sublanes(parts):
        step = SUBLANES // 2
        while step >= 1:
            lower = (sub & step) == 0
            nxt = []
            for j in range(len(parts) // 2):
                a, b = parts[(j // step) * 2 * step + j % step], parts[(j // step) * 2 * step + j % step + step]
                nxt.append(jnp.where(lower, a, pltpu.roll(b, step, 0))
                           + jnp.where(lower, pltpu.roll(a, SUBLANES - step, 0), b))
            parts = nxt
            step //= 2
        return parts[0]

    def u_pass(t, slot):
        row = pl.ds(t, 1)
        x_row = xn_ref[row, :]
        x_lo = jnp.concatenate([x_row[:, c * LANES:(c + 1) * LANES] for c in range(SUBLANES)], axis=0)
        x_hi = jnp.concatenate([x_row[:, c * LANES:(c + 1) * LANES] for c in range(SUBLANES, PEER_CHUNKS)], axis=0)
        a_diag = jnp.zeros((SUBLANES, LANES), F32)
        for g in range(PEER_EGROUPS):
            parts = []
            for j in range(SUBLANES):
                w = bufs[slot][g * SUBLANES + j]
                parts.append(pltpu.bitcast(w[:SUBLANES] & jnp.uint32(U_MASK), F32) * x_lo
                             + pltpu.bitcast(w[SUBLANES:] & jnp.uint32(U_MASK), F32) * x_hi)
            a_g = jnp.sum(fold_sublanes(parts), axis=1, keepdims=True)
            a_diag = jnp.where(lane_minus_sub == g * SUBLANES, a_g, a_diag)
        a_row = jnp.sum(a_diag, axis=0, keepdims=True)
        hk_scr[slot:slot + 1, :] = _gelu(a_row) * gate_ref[row, :]

    def v_pass(t, slot):
        row = pl.ds(t, 1)
        hk_row = jnp.broadcast_to(hk_scr[slot:slot + 1, :], (SUBLANES, LANES))
        for g in range(PEER_EGROUPS):
            hk_rep[g * SUBLANES:(g + 1) * SUBLANES, :] = jnp.broadcast_to(
                jnp.sum(jnp.where(lane_minus_sub == g * SUBLANES, hk_row, 0.0), axis=1, keepdims=True),
                (SUBLANES, LANES))
        n_chains = 4
        lo, hi = [None] * n_chains, [None] * n_chains
        for e in range(PEER_PICKS):
            w = bufs[slot][e]
            hk_e = jnp.broadcast_to(hk_rep[e:e + 1, :], (SUBLANES, LANES))
            p_lo = pltpu.bitcast(w[:SUBLANES] << 16, F32) * hk_e
            p_hi = pltpu.bitcast(w[SUBLANES:] << 16, F32) * hk_e
            i = e % n_chains
            lo[i] = p_lo if lo[i] is None else lo[i] + p_lo
            hi[i] = p_hi if hi[i] is None else hi[i] + p_hi
        out_lo = (lo[0] + lo[1]) + (lo[2] + lo[3])
        out_hi = (hi[0] + hi[1]) + (hi[2] + hi[3])
        out_row = jnp.concatenate([out_lo[c:c + 1, :] for c in range(SUBLANES)]
                                  + [out_hi[c:c + 1, :] for c in range(SUBLANES)], axis=1)
        o_ref[row, :] = h_ref[row, :] + out_row

    step, last_step = pl.program_id(0), pl.num_programs(0) - 1
    ahead = PEER_RING - 2

    def phase(k, s, do_v):
        wait(s)
        j, slot = k + ahead, (s + ahead) % PEER_RING
        if isinstance(j, int) and j >= tokens:
            @pl.when(step < last_step)
            def _():
                issue(nidx_ref, j - tokens, slot)
        else:
            issue(idx_ref, j, slot)
        u_pass(k, s)
        if do_v:
            v_pass(k - 1, (s - 1) % PEER_RING)

    def static_group(g):
        for s in range(PEER_RING):
            k = g * PEER_RING + s
            phase(k, s, k >= 1)

    def group(g, carry):
        for s in range(PEER_RING):
            phase(g * PEER_RING + s, s, True)
        return carry

    @pl.when(step == 0)
    def _():
        for s in range(ahead):
            issue(idx_ref, s, s)

    n_groups = tokens // PEER_RING
    static_group(0)
    if n_groups > 1:
        lax.fori_loop(1, n_groups - 1, group, 0)
        static_group(n_groups - 1)
    v_pass(tokens - 1, (tokens - 1) % PEER_RING)


def _peer_expert(idx, xn, gate, h, tab, tokens):
    n = h.shape[0]
    assert tokens % PEER_RING == 0 and n % tokens == 0
    tok = lambda w: pl.BlockSpec((tokens, w), lambda i: (i, 0))
    kern = functools.partial(_peer_expert_kernel, tokens=tokens)
    steps = n // tokens
    return pl.pallas_call(
        kern,
        grid=(steps,),
        in_specs=[pl.BlockSpec((tokens, PEER_PICKS), lambda i: (i, 0), memory_space=pltpu.SMEM),
                  pl.BlockSpec((tokens, PEER_PICKS), lambda i: (jnp.minimum(i + 1, steps - 1), 0),
                               memory_space=pltpu.SMEM),
                  tok(D_MODEL), tok(PEER_PICKS), tok(D_MODEL),
                  pl.BlockSpec(memory_space=pl.ANY)],
        out_specs=tok(D_MODEL),
        out_shape=jax.ShapeDtypeStruct((n, D_MODEL), F32),
        scratch_shapes=[pltpu.VMEM((PEER_PICKS, PEER_CHUNKS, LANES), jnp.uint32)
                        for _ in range(PEER_RING)] + [
                        pltpu.VMEM((PEER_PICKS, LANES), F32),
                        pltpu.VMEM((PEER_RING, PEER_PICKS), F32),
                        pltpu.SemaphoreType.DMA((PEER_RING,))],
        compiler_params=_params(VMEM_DENSE_MIB),
        name="peer_expert",
    )(idx, idx, xn, gate, h, tab)


def _final_kernel(h_ref, p_ref, gg_ref, wg_ref, wp_ref, gp_ref, gf_ref, y_ref):
    h = h_ref[...]
    gate = jax.nn.sigmoid(jnp.dot(_rms(h, gg_ref[...]).astype(BF16), wg_ref[...],
                                  preferred_element_type=F32))
    ple = jnp.dot(p_ref[...].astype(BF16), wp_ref[...], preferred_element_type=F32)
    h = h + _rms(ple, gp_ref[...]) * gate
    y_ref[...] = _rms(h, gf_ref[...])


def _final(h, p, g_gate, w_gate_bf, w_ple_bf, g_ple, g_final, tb):
    n = h.shape[0]
    tok = lambda w: pl.BlockSpec((tb, w), lambda i: (i, 0))
    return pl.pallas_call(
        _final_kernel,
        grid=(n // tb,),
        in_specs=[tok(D_MODEL), tok(PLE_DIM), _const_spec((1, D_MODEL)),
                  _const_spec((D_MODEL, D_MODEL)), _const_spec((PLE_DIM, D_MODEL)),
                  _const_spec((1, D_MODEL)), _const_spec((1, D_MODEL))],
        out_specs=tok(D_MODEL),
        out_shape=jax.ShapeDtypeStruct((n, D_MODEL), F32),
        compiler_params=_params(VMEM_DENSE_MIB),
        name="final",
    )(h, p, g_gate, w_gate_bf, w_ple_bf, g_ple, g_final)


def _rope_tables(pos):
    half = HEAD_DIM // 2
    inv_freq = 1.0 / (ROPE_THETA ** (jnp.arange(0, HEAD_DIM, 2, dtype=F32) / HEAD_DIM))
    ang = pos.astype(F32)[:, None] * inv_freq[None, :]
    cos, sin = jnp.cos(ang), jnp.sin(ang)
    reps = LANES // HEAD_DIM
    return (jnp.tile(jnp.concatenate([cos, cos], axis=1), (1, reps)),
            jnp.tile(jnp.concatenate([-sin, sin], axis=1), (1, reps)))


def _row(v):
    return v.reshape(1, -1).astype(F32)


def _token_stages(x, p, cos_t, sin_t, attn_fn, wm_bf, bias, lw, tb):
    q, k, v, u, vn = _in_proj(x, lw["g_mix"], lw["w_in"], cos_t, sin_t, lw["g_sgu"], lw["b_sgu"], tb)
    an = attn_fn(q, k, v)
    h = _sgu_out(u, vn, wm_bf, bias, lw["g_sgu_out"], an, lw["w_out"], x, tb)
    xn, idx, gate = _peer_topk(h, lw["g_ffn"], lw["w_pq"], lw["sub_k1"], lw["sub_k2"], tb)
    h = _peer_expert(idx, xn, gate, h, lw["uv_tab"], min(256, x.shape[0]))
    return h, k, v, vn


def kernel(x_prompt, x_sample, cache_k, cache_v, p_prompt, p_sample, g_mix, w_in, sinks, g_sgu, b_sgu,
           w_s, b_s, g_attn_out, g_sgu_out, w_out, g_ffn, w_pq, sub_k1, sub_k2, u_tab, v_tab, g_gate,
           w_gate, w_ple, g_ple, g_final):
    depth = w_in.shape[0]
    bsz, seq, _ = x_prompt.shape
    dbsz, dseq, _ = x_sample.shape
    wc = cache_k.shape[2]
    assert depth == 1, "the final kernel fuses the layer tail with the final norm"
    assert seq % WINDOW == 0 and wc == WINDOW and WINDOW % dseq == 0 and dseq <= SGU_CHUNK
    n_p, n_s = bsz * seq, dbsz * dseq
    tb_p = 256 if seq % 256 == 0 else WINDOW
    tb_s = WINDOW
    assert n_s % tb_s == 0 and seq % tb_p == 0
    nb = seq // WINDOW

    cos_p, sin_p = _rope_tables(jnp.arange(seq, dtype=jnp.int32))
    cos_s, sin_s = _rope_tables(PAST_LEN + (jnp.arange(tb_s, dtype=jnp.int32) % dseq))
    causal = jnp.tril(jnp.ones((SGU_CHUNK, SGU_CHUNK), F32))
    eye = jnp.eye(SGU_CHUNK // dseq, dtype=F32)

    hp = x_prompt.reshape(n_p, D_MODEL)
    hs = x_sample.reshape(n_s, D_MODEL)
    kp_l, vp_l, ks_l, vs_l, gs_l = [], [], [], [], []
    for i in range(depth):
        lw = dict(
            g_mix=_row(g_mix[i]), w_in=w_in[i].astype(BF16), g_sgu=_row(g_sgu[i]), b_sgu=_row(b_sgu[i]),
            g_sgu_out=_row(g_sgu_out[i]), w_out=w_out[i].astype(BF16), g_ffn=_row(g_ffn[i]),
            w_pq=w_pq[i].astype(BF16), sub_k1=sub_k1[i].astype(BF16), sub_k2=sub_k2[i].astype(BF16),
            uv_tab=_pack_uv(u_tab[i], v_tab[i]))
        sinks_i = sinks[i].astype(F32)
        g_ao = _row(g_attn_out[i])
        ws = w_s[i].astype(F32) * causal
        wm_p = ws.astype(BF16)
        bias_p = jnp.broadcast_to(b_s[i].astype(F32)[:, :, None], (SGU_HEADS, SGU_CHUNK, SGU_GROUP))
        wm_s = jnp.einsum("ab,gts->gatbs", eye, ws[:, :dseq, :dseq]).reshape(
            SGU_HEADS, SGU_CHUNK, SGU_CHUNK).astype(BF16)
        bias_s = jnp.broadcast_to(jnp.tile(b_s[i].astype(F32)[:, :dseq], (1, SGU_CHUNK // dseq))[:, :, None],
                                  (SGU_HEADS, SGU_CHUNK, SGU_GROUP))

        def attn_prompt(q, k, v):
            return _attention(sinks_i, q, k, v, g_ao, nb)

        def attn_sample(q, k, v, ck=cache_k[i], cv=cache_v[i]):
            return _attention_sample(sinks_i, q, ck.reshape(dbsz * wc, KV_COLS), cv.reshape(dbsz * wc, KV_COLS),
                                     k, v, g_ao, dseq, SGU_CHUNK // dseq)

        hp, kp, vp, _ = _token_stages(hp, None, cos_p, sin_p, attn_prompt, wm_p, bias_p, lw, tb_p)
        hs, ks, vs, vns = _token_stages(hs, None, cos_s, sin_s, attn_sample, wm_s, bias_s, lw, tb_s)
        fin = (_row(g_gate[i]), w_gate[i].astype(BF16), w_ple[i].astype(BF16), _row(g_ple[i]),
               _row(g_final))
        hp = _final(hp, p_prompt[i].reshape(n_p, PLE_DIM), *fin, tb_p)
        hs = _final(hs, p_sample[i].reshape(n_s, PLE_DIM), *fin, tb_s)

        tail = lambda a: a.reshape(bsz, seq, KV_COLS)[:, seq - WINDOW:].reshape(bsz, WINDOW, N_KV, HEAD_DIM)
        kp_l.append(tail(kp))
        vp_l.append(tail(vp))
        ks_l.append(jnp.concatenate([cache_k[i][:, dseq:], ks.reshape(dbsz, dseq, N_KV, HEAD_DIM)], axis=1))
        vs_l.append(jnp.concatenate([cache_v[i][:, dseq:], vs.reshape(dbsz, dseq, N_KV, HEAD_DIM)], axis=1))
        gs_l.append(vns.reshape(dbsz, dseq, SGU_HEADS, SGU_GROUP))

    return (hp.reshape(bsz, seq, D_MODEL), hs.reshape(dbsz, dseq, D_MODEL),
            jnp.stack(kp_l), jnp.stack(vp_l), jnp.stack(ks_l), jnp.stack(vs_l), jnp.stack(gs_l))
```

```python
import functools

import jax
import jax.numpy as jnp
from jax import lax
from jax.experimental import pallas as pl
from jax.experimental.pallas import tpu as pltpu

F32 = jnp.float32
BF16 = jnp.bfloat16

D_MODEL = 2048
PAST_LEN = 16384
HEAD_DIM = 64
N_HEADS = 16
N_KV = 2
Q_PER_KV = N_HEADS // N_KV
WINDOW = 128
ROPE_THETA = 10000.0
NEG_INF = -1e30
ATTN_WIDTH = N_HEADS * HEAD_DIM
SGU_CHUNK = 128
SGU_GROUP = 128
SGU_HEADS = 8
SGU_WIDTH = SGU_HEADS * SGU_GROUP
Q_COLS = N_HEADS * HEAD_DIM
KV_COLS = N_KV * HEAD_DIM
IN_COLS = Q_COLS + 2 * KV_COLS + 2 * SGU_WIDTH
PEER_HEADS = 8
PEER_NKEYS = 128
PEER_HALF = 128
PEER_TOPK = 16
PEER_PICKS = PEER_HEADS * PEER_TOPK
PLE_DIM = 256
EPS = 1e-6

LANES = 128
SUBLANES = 8
MIB = 1024 * 1024
VMEM_DENSE_MIB = 48
VMEM_SMALL_MIB = 32


def _rms(x, g):
    return x * lax.rsqrt(jnp.mean(x * x, axis=-1, keepdims=True) + EPS) * g


def _gelu(x):
    return 0.5 * x * (1.0 + lax.erf(x * (0.5 ** 0.5)))


def _const_spec(shape):
    nd = len(shape)
    return pl.BlockSpec(shape, lambda *_: (0,) * nd, pipeline_mode=pl.Buffered(1))


def _params(vmem_mib):
    return pltpu.CompilerParams(dimension_semantics=("arbitrary",), vmem_limit_bytes=vmem_mib * MIB)


def _in_proj_kernel(x_ref, g_ref, w_ref, cos_ref, sin_ref, gs_ref, bs_ref,
                    q_ref, k_ref, v_ref, u_ref, vn_ref):
    xn = _rms(x_ref[...], g_ref[...]).astype(BF16)
    z = jnp.dot(xn, w_ref[...], preferred_element_type=F32)
    cos = cos_ref[...]
    sin = sin_ref[...]
    lane = lax.broadcasted_iota(jnp.int32, cos.shape, 1)
    first_half = (lane % HEAD_DIM) < (HEAD_DIM // 2)

    def rope(zc):
        swapped = jnp.where(first_half,
                            pltpu.roll(zc, LANES - HEAD_DIM // 2, 1),
                            pltpu.roll(zc, HEAD_DIM // 2, 1))
        return zc * cos + swapped * sin

    for c in range(Q_COLS // LANES):
        q_ref[:, c * LANES:(c + 1) * LANES] = rope(z[:, c * LANES:(c + 1) * LANES])
    k_ref[...] = rope(z[:, Q_COLS:Q_COLS + KV_COLS])
    v_ref[...] = z[:, Q_COLS + KV_COLS:Q_COLS + 2 * KV_COLS]
    u0 = Q_COLS + 2 * KV_COLS
    u_ref[...] = _gelu(z[:, u0:u0 + SGU_WIDTH])
    sv = _gelu(z[:, u0 + SGU_WIDTH:u0 + 2 * SGU_WIDTH])
    mu = jnp.mean(sv, axis=-1, keepdims=True)
    svc = sv - mu
    var = jnp.mean(svc * svc, axis=-1, keepdims=True)
    vn_ref[...] = svc * lax.rsqrt(var + EPS) * gs_ref[...] + bs_ref[...]


def _in_proj(x, g_mix, w_in_bf, cos_t, sin_t, g_sgu, b_sgu, tb):
    n = x.shape[0]
    period_blocks = cos_t.shape[0] // tb
    tok = lambda w: pl.BlockSpec((tb, w), lambda i: (i, 0))
    tab = pl.BlockSpec((tb, LANES), lambda i: (i % period_blocks, 0))
    return pl.pallas_call(
        _in_proj_kernel,
        grid=(n // tb,),
        in_specs=[tok(D_MODEL), _const_spec((1, D_MODEL)), _const_spec((D_MODEL, IN_COLS)),
                  tab, tab, _const_spec((1, SGU_WIDTH)), _const_spec((1, SGU_WIDTH))],
        out_specs=[tok(Q_COLS), tok(KV_COLS), tok(KV_COLS), tok(SGU_WIDTH), tok(SGU_WIDTH)],
        out_shape=[jax.ShapeDtypeStruct((n, w), F32)
                   for w in (Q_COLS, KV_COLS, KV_COLS, SGU_WIDTH, SGU_WIDTH)],
        compiler_params=_params(VMEM_DENSE_MIB),
        name="in_proj",
    )(x, g_mix, w_in_bf, cos_t, sin_t, g_sgu, b_sgu)


def _attn_kernel(sinks_ref, q_ref, ka_ref, kb_ref, va_ref, vb_ref, g_ref, o_ref, acc_ref, *, blocks_per_seq):
    q = q_ref[...]
    kk_t_f32 = jnp.concatenate([ka_ref[...].T, kb_ref[...].T], axis=1)
    vv = jnp.concatenate([va_ref[...], vb_ref[...]], axis=0).astype(BF16)
    r = lax.broadcasted_iota(jnp.int32, (WINDOW, 2 * WINDOW), 0)
    c = lax.broadcasted_iota(jnp.int32, (WINDOW, 2 * WINDOW), 1)
    first = pl.program_id(0) % blocks_per_seq == 0
    valid = (c > jnp.where(first, jnp.maximum(r, WINDOW - 1), r)) & (c <= r + WINDOW)
    kk_t = kk_t_f32.astype(BF16)
    for g in range(N_KV):
        kg_t = kk_t[g * HEAD_DIM:(g + 1) * HEAD_DIM, :]
        vg = vv[:, g * HEAD_DIM:(g + 1) * HEAD_DIM]
        for hh in range(Q_PER_KV):
            h = g * Q_PER_KV + hh
            qh = q[:, h * HEAD_DIM:(h + 1) * HEAD_DIM].astype(BF16)
            s = jnp.dot(qh, kg_t, preferred_element_type=F32) * (HEAD_DIM ** -0.5)
            s = jnp.where(valid, s, NEG_INF)
            sink = sinks_ref[h]
            m = jnp.maximum(jnp.max(s, axis=1, keepdims=True), sink)
            p = jnp.exp(s - m)
            denom = jnp.sum(p, axis=1, keepdims=True) + jnp.exp(sink - m)
            p = (p / denom).astype(BF16)
            acc_ref[:, h * HEAD_DIM:(h + 1) * HEAD_DIM] = jnp.dot(p, vg, preferred_element_type=F32)
    o_ref[...] = _rms(acc_ref[...], g_ref[...]).astype(o_ref.dtype)


def _attention(sinks, q, k, v, g_attn_out, blocks_per_seq):
    n = q.shape[0]
    prev = pl.BlockSpec((WINDOW, KV_COLS), lambda j: (jnp.where(j % blocks_per_seq == 0, j, j - 1), 0))
    cur = pl.BlockSpec((WINDOW, KV_COLS), lambda j: (j, 0))
    kern = functools.partial(_attn_kernel, blocks_per_seq=blocks_per_seq)
    return pl.pallas_call(
        kern,
        grid=(n // WINDOW,),
        in_specs=[pl.BlockSpec(memory_space=pltpu.SMEM),
                  pl.BlockSpec((WINDOW, Q_COLS), lambda i: (i, 0)),
                  prev, cur, prev, cur,
                  _const_spec((1, ATTN_WIDTH))],
        out_specs=pl.BlockSpec((WINDOW, ATTN_WIDTH), lambda i: (i, 0)),
        out_shape=jax.ShapeDtypeStruct((n, ATTN_WIDTH), BF16),
        scratch_shapes=[pltpu.VMEM((WINDOW, ATTN_WIDTH), F32)],
        compiler_params=_params(VMEM_SMALL_MIB),
        name="attention",
    )(sinks, q, k, k, v, v, g_attn_out)


def _attn_sample_kernel(sinks_ref, q_ref, ck_ref, cv_ref, kn_ref, vn_ref, g_ref, o_ref, acc_ref, *, dseq, seqs):
    rows = Q_PER_KV * dseq
    r = lax.broadcasted_iota(jnp.int32, (rows, WINDOW), 0) % dseq
    c = lax.broadcasted_iota(jnp.int32, (rows, WINDOW), 1)
    valid_cache = c > r
    valid_new = c <= r
    head_of_row = lax.broadcasted_iota(jnp.int32, (rows, 1), 0) // dseq
    sink_cols = []
    for g in range(N_KV):
        col = jnp.zeros((rows, 1), F32)
        for hh in range(Q_PER_KV):
            col = jnp.where(head_of_row == hh, sinks_ref[g * Q_PER_KV + hh], col)
        sink_cols.append(col)
    pad = jnp.zeros((WINDOW - dseq, HEAD_DIM), F32)
    nt = (((1,), (1,)), ((), ()))
    for i in range(seqs):
        new = slice(i * dseq, (i + 1) * dseq)
        old = slice(i * WINDOW, (i + 1) * WINDOW)
        q = q_ref[new, :]
        for g in range(N_KV):
            cols = slice(g * HEAD_DIM, (g + 1) * HEAD_DIM)
            qg = jnp.concatenate([q[:, (g * Q_PER_KV + hh) * HEAD_DIM:(g * Q_PER_KV + hh + 1) * HEAD_DIM]
                                  for hh in range(Q_PER_KV)], axis=0).astype(BF16)
            k_new = jnp.concatenate([kn_ref[new, cols], pad], axis=0).astype(BF16)
            v_new = jnp.concatenate([vn_ref[new, cols], pad], axis=0).astype(BF16)
            s_old = lax.dot_general(qg, ck_ref[old, cols].astype(BF16), nt,
                                    preferred_element_type=F32) * (HEAD_DIM ** -0.5)
            s_new = lax.dot_general(qg, k_new, nt, preferred_element_type=F32) * (HEAD_DIM ** -0.5)
            s_old = jnp.where(valid_cache, s_old, NEG_INF)
            s_new = jnp.where(valid_new, s_new, NEG_INF)
            sink = sink_cols[g]
            m = jnp.maximum(jnp.maximum(jnp.max(s_old, axis=1, keepdims=True),
                                        jnp.max(s_new, axis=1, keepdims=True)), sink)
            p_old = jnp.exp(s_old - m)
            p_new = jnp.exp(s_new - m)
            denom = (jnp.sum(p_old, axis=1, keepdims=True) + jnp.sum(p_new, axis=1, keepdims=True)
                     + jnp.exp(sink - m))
            o = (jnp.dot((p_old / denom).astype(BF16), cv_ref[old, cols].astype(BF16),
                         preferred_element_type=F32)
                 + jnp.dot((p_new / denom).astype(BF16), v_new, preferred_element_type=F32))
            for hh in range(Q_PER_KV):
                h = g * Q_PER_KV + hh
                acc_ref[new, h * HEAD_DIM:(h + 1) * HEAD_DIM] = o[hh * dseq:(hh + 1) * dseq, :]
    o_ref[...] = _rms(acc_ref[...], g_ref[...]).astype(o_ref.dtype)


def _attention_sample(sinks, q, cache_k, cache_v, k_new, v_new, g_attn_out, dseq, seqs):
    n = q.shape[0]
    tok = lambda w: pl.BlockSpec((seqs * dseq, w), lambda i: (i, 0))
    cache = pl.BlockSpec((seqs * WINDOW, KV_COLS), lambda i: (i, 0))
    kern = functools.partial(_attn_sample_kernel, dseq=dseq, seqs=seqs)
    return pl.pallas_call(
        kern,
        grid=(n // (seqs * dseq),),
        in_specs=[pl.BlockSpec(memory_space=pltpu.SMEM), tok(Q_COLS), cache, cache, tok(KV_COLS), tok(KV_COLS),
                  _const_spec((1, ATTN_WIDTH))],
        out_specs=tok(ATTN_WIDTH),
        out_shape=jax.ShapeDtypeStruct((n, ATTN_WIDTH), BF16),
        scratch_shapes=[pltpu.VMEM((seqs * dseq, ATTN_WIDTH), F32)],
        compiler_params=_params(VMEM_SMALL_MIB),
        name="attention_sample",
    )(sinks, q, cache_k, cache_v, k_new, v_new, g_attn_out)


def _sgu_out_kernel(u_ref, vn_ref, wm_ref, bias_ref, g_ref, an_ref, wo_ref, x_ref, h_ref, s_ref):
    tb = u_ref.shape[0]
    for ch in range(tb // SGU_CHUNK):
        rows = slice(ch * SGU_CHUNK, (ch + 1) * SGU_CHUNK)
        for g in range(SGU_HEADS):
            cols = slice(g * SGU_GROUP, (g + 1) * SGU_GROUP)
            mix = jnp.dot(wm_ref[g], vn_ref[rows, cols].astype(BF16),
                          preferred_element_type=F32) + bias_ref[g]
            s_ref[rows, cols] = u_ref[rows, cols] * mix
    sn = _rms(s_ref[...], g_ref[...]).astype(BF16)
    mixed = jnp.dot(an_ref[...], wo_ref[:ATTN_WIDTH, :], preferred_element_type=F32)
    mixed = mixed + jnp.dot(sn, wo_ref[ATTN_WIDTH:, :], preferred_element_type=F32)
    h_ref[...] = x_ref[...] + mixed


def _sgu_out(u, vn, wm_bf, bias, g_sgu_out, an, w_out_bf, x, tb):
    n = x.shape[0]
    tok = lambda w: pl.BlockSpec((tb, w), lambda i: (i, 0))
    return pl.pallas_call(
        _sgu_out_kernel,
        grid=(n // tb,),
        in_specs=[tok(SGU_WIDTH), tok(SGU_WIDTH),
                  _const_spec((SGU_HEADS, SGU_CHUNK, SGU_CHUNK)),
                  _const_spec((SGU_HEADS, SGU_CHUNK, SGU_GROUP)),
                  _const_spec((1, SGU_WIDTH)), tok(ATTN_WIDTH),
                  _const_spec((D_MODEL, D_MODEL)), tok(D_MODEL)],
        out_specs=tok(D_MODEL),
        out_shape=jax.ShapeDtypeStruct((n, D_MODEL), F32),
        scratch_shapes=[pltpu.VMEM((tb, SGU_WIDTH), F32)],
        compiler_params=_params(VMEM_DENSE_MIB),
        name="sgu_out",
    )(u, vn, wm_bf, bias, g_sgu_out, an, w_out_bf, x)


def _odd_even_merge_sort_pairs(n):
    pairs = []

    def merge(lo, length, r):
        step = 2 * r
        if step < length:
            merge(lo, length, step)
            merge(lo + r, length, step)
            pairs.extend((i, i + r) for i in range(lo + r, lo + length - r, step))
        else:
            pairs.append((lo, lo + r))

    def sort(lo, length):
        if length > 1:
            half = length // 2
            sort(lo, half)
            sort(lo + half, half)
            merge(lo, length, 1)

    sort(0, n)
    return pairs


_BIG_ROW = 2 ** 30


def _top16_rows(s, tag=None, tag_bits=0):
    depth = s.shape[0] // SUBLANES
    row8 = lax.broadcasted_iota(jnp.int32, (SUBLANES, s.shape[1]), 0)
    vals = [s[d * SUBLANES:(d + 1) * SUBLANES, :] for d in range(depth)]
    pays = [(row8 + d * SUBLANES) << tag_bits for d in range(depth)]
    if tag is not None:
        pays = [p | tag[d * SUBLANES:(d + 1) * SUBLANES, :] for d, p in enumerate(pays)]
    for i, j in _odd_even_merge_sort_pairs(depth):
        a, b, pa, pb = vals[i], vals[j], pays[i], pays[j]
        swap = (b > a) | ((b == a) & (pb < pa))
        vals[i], vals[j] = jnp.where(swap, b, a), jnp.where(swap, a, b)
        pays[i], pays[j] = jnp.where(swap, pb, pa), jnp.where(swap, pa, pb)
    out_v, out_p = [], []
    for k in range(PEER_TOPK):
        head, head_p = vals[0], pays[0]
        m = jnp.max(head, axis=0, keepdims=True)
        p = jnp.min(jnp.where(head == m, head_p, _BIG_ROW), axis=0, keepdims=True)
        out_v.append(m)
        out_p.append(p)
        popped = head_p == p
        for d in range(min(depth, PEER_TOPK - 1 - k)):
            below_v, below_p = (vals[d + 1], pays[d + 1]) if d + 1 < depth else (-jnp.inf, _BIG_ROW)
            vals[d] = jnp.where(popped, below_v, vals[d])
            pays[d] = jnp.where(popped, below_p, pays[d])
    return jnp.concatenate(out_v, axis=0), jnp.concatenate(out_p, axis=0)


_CAND_WIDTHS = (16, 8, 8, 8, 4, 4, 4, 4)


def _staircase(a, b, combine):
    rows = [combine(a[i:i + 1, :], b[:w, :]) for i, w in enumerate(_CAND_WIDTHS)]
    rows.append(combine(a[len(_CAND_WIDTHS):, :], b[0:1, :]))
    return jnp.concatenate(rows, axis=0)


def _peer_topk_kernel(h_ref, g_ref, wq_ref, k1_ref, k2_ref, xn_ref, idx_ref, gate_ref):
    xn = _rms(h_ref[...], g_ref[...])
    xn_ref[...] = xn
    qp = jnp.dot(xn.astype(BF16), wq_ref[...], preferred_element_type=F32).astype(BF16)
    idx_parts, gate_parts = [], []
    nt = (((1,), (1,)), ((), ()))
    for h in range(PEER_HEADS):
        base = h * 2 * PEER_HALF
        s1 = lax.dot_general(k1_ref[h], qp[:, base:base + PEER_HALF], nt,
                             preferred_element_type=F32)
        s2 = lax.dot_general(k2_ref[h], qp[:, base + PEER_HALF:base + 2 * PEER_HALF], nt,
                             preferred_element_type=F32)
        t1, i1 = _top16_rows(s1)
        t2, i2 = _top16_rows(s2)
        cand = _staircase(t1, t2, lambda a, b: a + b)
        expert = _staircase(i1, i2, lambda a, b: a * PEER_NKEYS + b)
        expert_bits = (PEER_NKEYS * PEER_NKEYS - 1).bit_length()
        sc, code = _top16_rows(cand, expert, expert_bits)
        idx_parts.append(code & ((1 << expert_bits) - 1))
        ex = jnp.exp(sc - sc[0:1, :])
        gate_parts.append(ex / jnp.sum(ex, axis=0, keepdims=True))
    idx_t = jnp.concatenate(idx_parts, axis=0)
    gate_t = jnp.concatenate(gate_parts, axis=0)
    tb = h_ref.shape[0]
    for c in range(tb // LANES):
        idx_ref[c * LANES:(c + 1) * LANES, :] = idx_t[:, c * LANES:(c + 1) * LANES].T
        gate_ref[c * LANES:(c + 1) * LANES, :] = gate_t[:, c * LANES:(c + 1) * LANES].T


def _peer_topk(h, g_ffn, w_pq_bf, k1_bf, k2_bf, tb):
    n = h.shape[0]
    tok = lambda w: pl.BlockSpec((tb, w), lambda i: (i, 0))
    return pl.pallas_call(
        _peer_topk_kernel,
        grid=(n // tb,),
        in_specs=[tok(D_MODEL), _const_spec((1, D_MODEL)),
                  _const_spec((D_MODEL, PEER_HEADS * 2 * PEER_HALF)),
                  _const_spec((PEER_HEADS, PEER_NKEYS, PEER_HALF)),
                  _const_spec((PEER_HEADS, PEER_NKEYS, PEER_HALF))],
        out_specs=[tok(D_MODEL), tok(PEER_PICKS), tok(PEER_PICKS)],
        out_shape=[jax.ShapeDtypeStruct((n, D_MODEL), F32),
                   jax.ShapeDtypeStruct((n, PEER_PICKS), jnp.int32),
                   jax.ShapeDtypeStruct((n, PEER_PICKS), F32)],
        compiler_params=_params(VMEM_DENSE_MIB),
        name="peer_topk",
    )(h, g_ffn, w_pq_bf, k1_bf, k2_bf)


PEER_RING = 16
PEER_CHUNKS = D_MODEL // LANES
PEER_EGROUPS = PEER_PICKS // SUBLANES
PEER_PITCH = 20

U_MASK = 0xFFFF0000


def _pack_uv(u_tab, v_tab):
    hi = lax.bitcast_convert_type(u_tab.astype(BF16), jnp.uint16).astype(jnp.uint32) << 16
    lo = lax.bitcast_convert_type(v_tab.astype(BF16), jnp.uint16).astype(jnp.uint32)
    return (hi | lo).reshape(u_tab.shape[0], PEER_CHUNKS, LANES)


def _peer_expert_kernel(idx_ref, nidx_ref, xn_ref, gate_ref, h_ref, tab_hbm, o_ref, *scratch, tokens):
    bufs, (stage, hk_scr, sem) = scratch[:PEER_RING], scratch[PEER_RING:]

    def row_copy(slot, e, r):
        return pltpu.make_async_copy(tab_hbm.at[r], bufs[slot].at[pl.ds(e * PEER_PITCH, PEER_CHUNKS), :],
                                     sem.at[slot])

    def issue(src_idx_ref, t, slot):
        for e in range(PEER_PICKS):
            row_copy(slot, e, src_idx_ref[t, e]).start(priority=e % 2)

    def wait(slot):
        for e in range(PEER_PICKS):
            row_copy(slot, e, 0).wait()

    sub = lax.broadcasted_iota(jnp.int32, (SUBLANES, LANES), 0)
    lane_minus_sub = lax.broadcasted_iota(jnp.int32, (SUBLANES, LANES), 1) - sub

    def picks_tile(slot, g, c):
        return bufs[slot][pl.ds(g * SUBLANES * PEER_PITCH + c, SUBLANES, stride=PEER_PITCH), :]

    def u_pass(t, slot):
        row = pl.ds(t, 1)
        x_row = xn_ref[row, :]
        xb = [jnp.broadcast_to(x_row[:, c * LANES:(c + 1) * LANES], (SUBLANES, LANES))
              for c in range(PEER_CHUNKS)]
        a_diag = jnp.zeros((SUBLANES, LANES), F32)
        for g in range(PEER_EGROUPS):
            acc = None
            for c in range(PEER_CHUNKS):
                u = pltpu.bitcast(picks_tile(slot, g, c) & jnp.uint32(U_MASK), F32)
                acc = u * xb[c] if acc is None else acc + u * xb[c]
            a_g = jnp.sum(acc, axis=1, keepdims=True)
            a_diag = jnp.where(lane_minus_sub == g * SUBLANES, a_g, a_diag)
        a_row = jnp.sum(a_diag, axis=0, keepdims=True)
        hk_scr[slot:slot + 1, :] = _gelu(a_row) * gate_ref[row, :]

    def v_pass(t, slot):
        row = pl.ds(t, 1)
        hk_row = jnp.broadcast_to(hk_scr[slot:slot + 1, :], (SUBLANES, LANES))
        hk_g = [jnp.broadcast_to(
                    jnp.sum(jnp.where(lane_minus_sub == g * SUBLANES, hk_row, 0.0), axis=1, keepdims=True),
                    (SUBLANES, LANES))
                for g in range(PEER_EGROUPS)]
        for c in range(PEER_CHUNKS):
            acc = None
            for g in range(PEER_EGROUPS):
                v = pltpu.bitcast(picks_tile(slot, g, c) << 16, F32)
                acc = v * hk_g[g] if acc is None else acc + v * hk_g[g]
            stage[slot:slot + 1, c * LANES:(c + 1) * LANES] = jnp.sum(acc, axis=0, keepdims=True)
        o_ref[row, :] = h_ref[row, :] + stage[slot:slot + 1, :]

    step, last_step = pl.program_id(0), pl.num_programs(0) - 1
    ahead = PEER_RING - 2

    def phase(k, s, do_v):
        wait(s)
        j, slot = k + ahead, (s + ahead) % PEER_RING
        if isinstance(j, int) and j >= tokens:
            @pl.when(step < last_step)
            def _():
                issue(nidx_ref, j - tokens, slot)
        else:
            issue(idx_ref, j, slot)
        u_pass(k, s)
        if do_v:
            v_pass(k - 1, (s - 1) % PEER_RING)

    def static_group(g):
        for s in range(PEER_RING):
            k = g * PEER_RING + s
            phase(k, s, k >= 1)

    def group(g, carry):
        for s in range(PEER_RING):
            phase(g * PEER_RING + s, s, True)
        return carry

    @pl.when(step == 0)
    def _():
        for s in range(ahead):
            issue(idx_ref, s, s)

    n_groups = tokens // PEER_RING
    static_group(0)
    if n_groups > 1:
        lax.fori_loop(1, n_groups - 1, group, 0)
        static_group(n_groups - 1)
    v_pass(tokens - 1, (tokens - 1) % PEER_RING)


def _peer_expert(idx, xn, gate, h, tab, tokens):
    n = h.shape[0]
    assert tokens % PEER_RING == 0 and n % tokens == 0
    tok = lambda w: pl.BlockSpec((tokens, w), lambda i: (i, 0))
    kern = functools.partial(_peer_expert_kernel, tokens=tokens)
    steps = n // tokens
    return pl.pallas_call(
        kern,
        grid=(steps,),
        in_specs=[pl.BlockSpec((tokens, PEER_PICKS), lambda i: (i, 0), memory_space=pltpu.SMEM),
                  pl.BlockSpec((tokens, PEER_PICKS), lambda i: (jnp.minimum(i + 1, steps - 1), 0),
                               memory_space=pltpu.SMEM),
                  tok(D_MODEL), tok(PEER_PICKS), tok(D_MODEL),
                  pl.BlockSpec(memory_space=pl.ANY)],
        out_specs=tok(D_MODEL),
        out_shape=jax.ShapeDtypeStruct((n, D_MODEL), F32),
        scratch_shapes=[pltpu.VMEM((PEER_PICKS * PEER_PITCH, LANES), jnp.uint32)
                        for _ in range(PEER_RING)] + [
                        pltpu.VMEM((PEER_RING, D_MODEL), F32),
                        pltpu.VMEM((PEER_RING, PEER_PICKS), F32),
                        pltpu.SemaphoreType.DMA((PEER_RING,))],
        compiler_params=_params(VMEM_DENSE_MIB),
        name="peer_expert",
    )(idx, idx, xn, gate, h, tab)


def _final_kernel(h_ref, p_ref, gg_ref, wg_ref, wp_ref, gp_ref, gf_ref, y_ref):
    h = h_ref[...]
    gate = jax.nn.sigmoid(jnp.dot(_rms(h, gg_ref[...]).astype(BF16), wg_ref[...],
                                  preferred_element_type=F32))
    ple = jnp.dot(p_ref[...].astype(BF16), wp_ref[...], preferred_element_type=F32)
    h = h + _rms(ple, gp_ref[...]) * gate
    y_ref[...] = _rms(h, gf_ref[...])


def _final(h, p, g_gate, w_gate_bf, w_ple_bf, g_ple, g_final, tb):
    n = h.shape[0]
    tok = lambda w: pl.BlockSpec((tb, w), lambda i: (i, 0))
    return pl.pallas_call(
        _final_kernel,
        grid=(n // tb,),
        in_specs=[tok(D_MODEL), tok(PLE_DIM), _const_spec((1, D_MODEL)),
                  _const_spec((D_MODEL, D_MODEL)), _const_spec((PLE_DIM, D_MODEL)),
                  _const_spec((1, D_MODEL)), _const_spec((1, D_MODEL))],
        out_specs=tok(D_MODEL),
        out_shape=jax.ShapeDtypeStruct((n, D_MODEL), F32),
        compiler_params=_params(VMEM_DENSE_MIB),
        name="final",
    )(h, p, g_gate, w_gate_bf, w_ple_bf, g_ple, g_final)


def _rope_tables(pos):
    half = HEAD_DIM // 2
    inv_freq = 1.0 / (ROPE_THETA ** (jnp.arange(0, HEAD_DIM, 2, dtype=F32) / HEAD_DIM))
    ang = pos.astype(F32)[:, None] * inv_freq[None, :]
    cos, sin = jnp.cos(ang), jnp.sin(ang)
    reps = LANES // HEAD_DIM
    return (jnp.tile(jnp.concatenate([cos, cos], axis=1), (1, reps)),
            jnp.tile(jnp.concatenate([-sin, sin], axis=1), (1, reps)))


def _row(v):
    return v.reshape(1, -1).astype(F32)


def _token_stages(x, p, cos_t, sin_t, attn_fn, wm_bf, bias, lw, tb):
    q, k, v, u, vn = _in_proj(x, lw["g_mix"], lw["w_in"], cos_t, sin_t, lw["g_sgu"], lw["b_sgu"], tb)
    an = attn_fn(q, k, v)
    h = _sgu_out(u, vn, wm_bf, bias, lw["g_sgu_out"], an, lw["w_out"], x, tb)
    xn, idx, gate = _peer_topk(h, lw["g_ffn"], lw["w_pq"], lw["sub_k1"], lw["sub_k2"], tb)
    h = _peer_expert(idx, xn, gate, h, lw["uv_tab"], min(256, x.shape[0]))
    return h, k, v, vn


def kernel(x_prompt, x_sample, cache_k, cache_v, p_prompt, p_sample, g_mix, w_in, sinks, g_sgu, b_sgu,
           w_s, b_s, g_attn_out, g_sgu_out, w_out, g_ffn, w_pq, sub_k1, sub_k2, u_tab, v_tab, g_gate,
           w_gate, w_ple, g_ple, g_final):
    depth = w_in.shape[0]
    bsz, seq, _ = x_prompt.shape
    dbsz, dseq, _ = x_sample.shape
    wc = cache_k.shape[2]
    assert depth == 1, "the final kernel fuses the layer tail with the final norm"
    assert seq % WINDOW == 0 and wc == WINDOW and WINDOW % dseq == 0 and dseq <= SGU_CHUNK
    n_p, n_s = bsz * seq, dbsz * dseq
    tb_p = 256 if seq % 256 == 0 else WINDOW
    tb_s = WINDOW
    assert n_s % tb_s == 0 and seq % tb_p == 0
    nb = seq // WINDOW

    cos_p, sin_p = _rope_tables(jnp.arange(seq, dtype=jnp.int32))
    cos_s, sin_s = _rope_tables(PAST_LEN + (jnp.arange(tb_s, dtype=jnp.int32) % dseq))
    causal = jnp.tril(jnp.ones((SGU_CHUNK, SGU_CHUNK), F32))
    eye = jnp.eye(SGU_CHUNK // dseq, dtype=F32)

    hp = x_prompt.reshape(n_p, D_MODEL)
    hs = x_sample.reshape(n_s, D_MODEL)
    kp_l, vp_l, ks_l, vs_l, gs_l = [], [], [], [], []
    for i in range(depth):
        lw = dict(
            g_mix=_row(g_mix[i]), w_in=w_in[i].astype(BF16), g_sgu=_row(g_sgu[i]), b_sgu=_row(b_sgu[i]),
            g_sgu_out=_row(g_sgu_out[i]), w_out=w_out[i].astype(BF16), g_ffn=_row(g_ffn[i]),
            w_pq=w_pq[i].astype(BF16), sub_k1=sub_k1[i].astype(BF16), sub_k2=sub_k2[i].astype(BF16),
            uv_tab=_pack_uv(u_tab[i], v_tab[i]))
        sinks_i = sinks[i].astype(F32)
        g_ao = _row(g_attn_out[i])
        ws = w_s[i].astype(F32) * causal
        wm_p = ws.astype(BF16)
        bias_p = jnp.broadcast_to(b_s[i].astype(F32)[:, :, None], (SGU_HEADS, SGU_CHUNK, SGU_GROUP))
        wm_s = jnp.einsum("ab,gts->gatbs", eye, ws[:, :dseq, :dseq]).reshape(
            SGU_HEADS, SGU_CHUNK, SGU_CHUNK).astype(BF16)
        bias_s = jnp.broadcast_to(jnp.tile(b_s[i].astype(F32)[:, :dseq], (1, SGU_CHUNK // dseq))[:, :, None],
                                  (SGU_HEADS, SGU_CHUNK, SGU_GROUP))

        def attn_prompt(q, k, v):
            return _attention(sinks_i, q, k, v, g_ao, nb)

        def attn_sample(q, k, v, ck=cache_k[i], cv=cache_v[i]):
            return _attention_sample(sinks_i, q, ck.reshape(dbsz * wc, KV_COLS), cv.reshape(dbsz * wc, KV_COLS),
                                     k, v, g_ao, dseq, SGU_CHUNK // dseq)

        hp, kp, vp, _ = _token_stages(hp, None, cos_p, sin_p, attn_prompt, wm_p, bias_p, lw, tb_p)
        hs, ks, vs, vns = _token_stages(hs, None, cos_s, sin_s, attn_sample, wm_s, bias_s, lw, tb_s)
        fin = (_row(g_gate[i]), w_gate[i].astype(BF16), w_ple[i].astype(BF16), _row(g_ple[i]),
               _row(g_final))
        hp = _final(hp, p_prompt[i].reshape(n_p, PLE_DIM), *fin, tb_p)
        hs = _final(hs, p_sample[i].reshape(n_s, PLE_DIM), *fin, tb_s)

        tail = lambda a: a.reshape(bsz, seq, KV_COLS)[:, seq - WINDOW:].reshape(bsz, WINDOW, N_KV, HEAD_DIM)
        kp_l.append(tail(kp))
        vp_l.append(tail(vp))
        ks_l.append(jnp.concatenate([cache_k[i][:, dseq:], ks.reshape(dbsz, dseq, N_KV, HEAD_DIM)], axis=1))
        vs_l.append(jnp.concatenate([cache_v[i][:, dseq:], vs.reshape(dbsz, dseq, N_KV, HEAD_DIM)], axis=1))
        gs_l.append(vns.reshape(dbsz, dseq, SGU_HEADS, SGU_GROUP))

    return (hp.reshape(bsz, seq, D_MODEL), hs.reshape(dbsz, dseq, D_MODEL),
            jnp.stack(kp_l), jnp.stack(vp_l), jnp.stack(ks_l), jnp.stack(vs_l), jnp.stack(gs_l))
```

```python
import functools

import jax
import jax.numpy as jnp
from jax import lax
from jax.experimental import pallas as pl
from jax.experimental.pallas import tpu as pltpu

F32 = jnp.float32
BF16 = jnp.bfloat16

D_MODEL = 2048
PAST_LEN = 16384
HEAD_DIM = 64
N_HEADS = 16
N_KV = 2
Q_PER_KV = N_HEADS // N_KV
WINDOW = 128
ROPE_THETA = 10000.0
NEG_INF = -1e30
ATTN_WIDTH = N_HEADS * HEAD_DIM
SGU_CHUNK = 128
SGU_GROUP = 128
SGU_HEADS = 8
SGU_WIDTH = SGU_HEADS * SGU_GROUP
Q_COLS = N_HEADS * HEAD_DIM
KV_COLS = N_KV * HEAD_DIM
IN_COLS = Q_COLS + 2 * KV_COLS + 2 * SGU_WIDTH
PEER_HEADS = 8
PEER_NKEYS = 128
PEER_HALF = 128
PEER_TOPK = 16
PEER_PICKS = PEER_HEADS * PEER_TOPK
PLE_DIM = 256
EPS = 1e-6

LANES = 128
SUBLANES = 8
MIB = 1024 * 1024
VMEM_DENSE_MIB = 48
VMEM_SMALL_MIB = 32
VMEM_PEER_MIB = 56


def _rms(x, g):
    return x * lax.rsqrt(jnp.mean(x * x, axis=-1, keepdims=True) + EPS) * g


def _gelu(x):
    return 0.5 * x * (1.0 + lax.erf(x * (0.5 ** 0.5)))


def _const_spec(shape):
    nd = len(shape)
    return pl.BlockSpec(shape, lambda *_: (0,) * nd, pipeline_mode=pl.Buffered(1))


def _params(vmem_mib):
    return pltpu.CompilerParams(dimension_semantics=("arbitrary",), vmem_limit_bytes=vmem_mib * MIB)


def _in_proj_kernel(x_ref, g_ref, w_ref, cos_ref, sin_ref, gs_ref, bs_ref,
                    q_ref, k_ref, v_ref, u_ref, vn_ref):
    xn = _rms(x_ref[...], g_ref[...]).astype(BF16)
    z = jnp.dot(xn, w_ref[...], preferred_element_type=F32)
    cos = cos_ref[...]
    sin = sin_ref[...]
    lane = lax.broadcasted_iota(jnp.int32, cos.shape, 1)
    first_half = (lane % HEAD_DIM) < (HEAD_DIM // 2)

    def rope(zc):
        swapped = jnp.where(first_half,
                            pltpu.roll(zc, LANES - HEAD_DIM // 2, 1),
                            pltpu.roll(zc, HEAD_DIM // 2, 1))
        return zc * cos + swapped * sin

    for c in range(Q_COLS // LANES):
        q_ref[:, c * LANES:(c + 1) * LANES] = rope(z[:, c * LANES:(c + 1) * LANES])
    k_ref[...] = rope(z[:, Q_COLS:Q_COLS + KV_COLS])
    v_ref[...] = z[:, Q_COLS + KV_COLS:Q_COLS + 2 * KV_COLS]
    u0 = Q_COLS + 2 * KV_COLS
    u_ref[...] = _gelu(z[:, u0:u0 + SGU_WIDTH])
    sv = _gelu(z[:, u0 + SGU_WIDTH:u0 + 2 * SGU_WIDTH])
    mu = jnp.mean(sv, axis=-1, keepdims=True)
    svc = sv - mu
    var = jnp.mean(svc * svc, axis=-1, keepdims=True)
    vn_ref[...] = svc * lax.rsqrt(var + EPS) * gs_ref[...] + bs_ref[...]


def _in_proj(x, g_mix, w_in_bf, cos_t, sin_t, g_sgu, b_sgu, tb):
    n = x.shape[0]
    period_blocks = cos_t.shape[0] // tb
    tok = lambda w: pl.BlockSpec((tb, w), lambda i: (i, 0))
    tab = pl.BlockSpec((tb, LANES), lambda i: (i % period_blocks, 0))
    return pl.pallas_call(
        _in_proj_kernel,
        grid=(n // tb,),
        in_specs=[tok(D_MODEL), _const_spec((1, D_MODEL)), _const_spec((D_MODEL, IN_COLS)),
                  tab, tab, _const_spec((1, SGU_WIDTH)), _const_spec((1, SGU_WIDTH))],
        out_specs=[tok(Q_COLS), tok(KV_COLS), tok(KV_COLS), tok(SGU_WIDTH), tok(SGU_WIDTH)],
        out_shape=[jax.ShapeDtypeStruct((n, w), F32)
                   for w in (Q_COLS, KV_COLS, KV_COLS, SGU_WIDTH, SGU_WIDTH)],
        compiler_params=_params(VMEM_DENSE_MIB),
        name="in_proj",
    )(x, g_mix, w_in_bf, cos_t, sin_t, g_sgu, b_sgu)


def _attn_kernel(sinks_ref, q_ref, ka_ref, kb_ref, va_ref, vb_ref, g_ref, o_ref, acc_ref, *, blocks_per_seq):
    q = q_ref[...]
    kk_t_f32 = jnp.concatenate([ka_ref[...].T, kb_ref[...].T], axis=1)
    vv = jnp.concatenate([va_ref[...], vb_ref[...]], axis=0).astype(BF16)
    r = lax.broadcasted_iota(jnp.int32, (WINDOW, 2 * WINDOW), 0)
    c = lax.broadcasted_iota(jnp.int32, (WINDOW, 2 * WINDOW), 1)
    first = pl.program_id(0) % blocks_per_seq == 0
    valid = (c > jnp.where(first, jnp.maximum(r, WINDOW - 1), r)) & (c <= r + WINDOW)
    kk_t = kk_t_f32.astype(BF16)
    for g in range(N_KV):
        kg_t = kk_t[g * HEAD_DIM:(g + 1) * HEAD_DIM, :]
        vg = vv[:, g * HEAD_DIM:(g + 1) * HEAD_DIM]
        for hh in range(Q_PER_KV):
            h = g * Q_PER_KV + hh
            qh = q[:, h * HEAD_DIM:(h + 1) * HEAD_DIM].astype(BF16)
            s = jnp.dot(qh, kg_t, preferred_element_type=F32) * (HEAD_DIM ** -0.5)
            s = jnp.where(valid, s, NEG_INF)
            sink = sinks_ref[h]
            m = jnp.maximum(jnp.max(s, axis=1, keepdims=True), sink)
            p = jnp.exp(s - m)
            denom = jnp.sum(p, axis=1, keepdims=True) + jnp.exp(sink - m)
            p = (p / denom).astype(BF16)
            acc_ref[:, h * HEAD_DIM:(h + 1) * HEAD_DIM] = jnp.dot(p, vg, preferred_element_type=F32)
    o_ref[...] = _rms(acc_ref[...], g_ref[...]).astype(o_ref.dtype)


def _attention(sinks, q, k, v, g_attn_out, blocks_per_seq):
    n = q.shape[0]
    prev = pl.BlockSpec((WINDOW, KV_COLS), lambda j: (jnp.where(j % blocks_per_seq == 0, j, j - 1), 0))
    cur = pl.BlockSpec((WINDOW, KV_COLS), lambda j: (j, 0))
    kern = functools.partial(_attn_kernel, blocks_per_seq=blocks_per_seq)
    return pl.pallas_call(
        kern,
        grid=(n // WINDOW,),
        in_specs=[pl.BlockSpec(memory_space=pltpu.SMEM),
                  pl.BlockSpec((WINDOW, Q_COLS), lambda i: (i, 0)),
                  prev, cur, prev, cur,
                  _const_spec((1, ATTN_WIDTH))],
        out_specs=pl.BlockSpec((WINDOW, ATTN_WIDTH), lambda i: (i, 0)),
        out_shape=jax.ShapeDtypeStruct((n, ATTN_WIDTH), BF16),
        scratch_shapes=[pltpu.VMEM((WINDOW, ATTN_WIDTH), F32)],
        compiler_params=_params(VMEM_SMALL_MIB),
        name="attention",
    )(sinks, q, k, k, v, v, g_attn_out)


def _attn_sample_kernel(sinks_ref, q_ref, ck_ref, cv_ref, kn_ref, vn_ref, g_ref, o_ref, acc_ref, *, dseq, seqs):
    rows = Q_PER_KV * dseq
    r = lax.broadcasted_iota(jnp.int32, (rows, WINDOW), 0) % dseq
    c = lax.broadcasted_iota(jnp.int32, (rows, WINDOW), 1)
    valid_cache = c > r
    valid_new = c <= r
    head_of_row = lax.broadcasted_iota(jnp.int32, (rows, 1), 0) // dseq
    sink_cols = []
    for g in range(N_KV):
        col = jnp.zeros((rows, 1), F32)
        for hh in range(Q_PER_KV):
            col = jnp.where(head_of_row == hh, sinks_ref[g * Q_PER_KV + hh], col)
        sink_cols.append(col)
    pad = jnp.zeros((WINDOW - dseq, HEAD_DIM), F32)
    nt = (((1,), (1,)), ((), ()))
    for i in range(seqs):
        new = slice(i * dseq, (i + 1) * dseq)
        old = slice(i * WINDOW, (i + 1) * WINDOW)
        q = q_ref[new, :]
        for g in range(N_KV):
            cols = slice(g * HEAD_DIM, (g + 1) * HEAD_DIM)
            qg = jnp.concatenate([q[:, (g * Q_PER_KV + hh) * HEAD_DIM:(g * Q_PER_KV + hh + 1) * HEAD_DIM]
                                  for hh in range(Q_PER_KV)], axis=0).astype(BF16)
            k_new = jnp.concatenate([kn_ref[new, cols], pad], axis=0).astype(BF16)
            v_new = jnp.concatenate([vn_ref[new, cols], pad], axis=0).astype(BF16)
            s_old = lax.dot_general(qg, ck_ref[old, cols].astype(BF16), nt,
                                    preferred_element_type=F32) * (HEAD_DIM ** -0.5)
            s_new = lax.dot_general(qg, k_new, nt, preferred_element_type=F32) * (HEAD_DIM ** -0.5)
            s_old = jnp.where(valid_cache, s_old, NEG_INF)
            s_new = jnp.where(valid_new, s_new, NEG_INF)
            sink = sink_cols[g]
            m = jnp.maximum(jnp.maximum(jnp.max(s_old, axis=1, keepdims=True),
                                        jnp.max(s_new, axis=1, keepdims=True)), sink)
            p_old = jnp.exp(s_old - m)
            p_new = jnp.exp(s_new - m)
            denom = (jnp.sum(p_old, axis=1, keepdims=True) + jnp.sum(p_new, axis=1, keepdims=True)
                     + jnp.exp(sink - m))
            o = (jnp.dot((p_old / denom).astype(BF16), cv_ref[old, cols].astype(BF16),
                         preferred_element_type=F32)
                 + jnp.dot((p_new / denom).astype(BF16), v_new, preferred_element_type=F32))
            for hh in range(Q_PER_KV):
                h = g * Q_PER_KV + hh
                acc_ref[new, h * HEAD_DIM:(h + 1) * HEAD_DIM] = o[hh * dseq:(hh + 1) * dseq, :]
    o_ref[...] = _rms(acc_ref[...], g_ref[...]).astype(o_ref.dtype)


def _attention_sample(sinks, q, cache_k, cache_v, k_new, v_new, g_attn_out, dseq, seqs):
    n = q.shape[0]
    tok = lambda w: pl.BlockSpec((seqs * dseq, w), lambda i: (i, 0))
    cache = pl.BlockSpec((seqs * WINDOW, KV_COLS), lambda i: (i, 0))
    kern = functools.partial(_attn_sample_kernel, dseq=dseq, seqs=seqs)
    return pl.pallas_call(
        kern,
        grid=(n // (seqs * dseq),),
        in_specs=[pl.BlockSpec(memory_space=pltpu.SMEM), tok(Q_COLS), cache, cache, tok(KV_COLS), tok(KV_COLS),
                  _const_spec((1, ATTN_WIDTH))],
        out_specs=tok(ATTN_WIDTH),
        out_shape=jax.ShapeDtypeStruct((n, ATTN_WIDTH), BF16),
        scratch_shapes=[pltpu.VMEM((seqs * dseq, ATTN_WIDTH), F32)],
        compiler_params=_params(VMEM_SMALL_MIB),
        name="attention_sample",
    )(sinks, q, cache_k, cache_v, k_new, v_new, g_attn_out)


def _sgu_out_kernel(u_ref, vn_ref, wm_ref, bias_ref, g_ref, an_ref, wo_ref, x_ref, h_ref, s_ref):
    tb = u_ref.shape[0]
    for ch in range(tb // SGU_CHUNK):
        rows = slice(ch * SGU_CHUNK, (ch + 1) * SGU_CHUNK)
        for g in range(SGU_HEADS):
            cols = slice(g * SGU_GROUP, (g + 1) * SGU_GROUP)
            mix = jnp.dot(wm_ref[g], vn_ref[rows, cols].astype(BF16),
                          preferred_element_type=F32) + bias_ref[g]
            s_ref[rows, cols] = u_ref[rows, cols] * mix
    sn = _rms(s_ref[...], g_ref[...]).astype(BF16)
    mixed = jnp.dot(an_ref[...], wo_ref[:ATTN_WIDTH, :], preferred_element_type=F32)
    mixed = mixed + jnp.dot(sn, wo_ref[ATTN_WIDTH:, :], preferred_element_type=F32)
    h_ref[...] = x_ref[...] + mixed


def _sgu_out(u, vn, wm_bf, bias, g_sgu_out, an, w_out_bf, x, tb):
    n = x.shape[0]
    tok = lambda w: pl.BlockSpec((tb, w), lambda i: (i, 0))
    return pl.pallas_call(
        _sgu_out_kernel,
        grid=(n // tb,),
        in_specs=[tok(SGU_WIDTH), tok(SGU_WIDTH),
                  _const_spec((SGU_HEADS, SGU_CHUNK, SGU_CHUNK)),
                  _const_spec((SGU_HEADS, SGU_CHUNK, SGU_GROUP)),
                  _const_spec((1, SGU_WIDTH)), tok(ATTN_WIDTH),
                  _const_spec((D_MODEL, D_MODEL)), tok(D_MODEL)],
        out_specs=tok(D_MODEL),
        out_shape=jax.ShapeDtypeStruct((n, D_MODEL), F32),
        scratch_shapes=[pltpu.VMEM((tb, SGU_WIDTH), F32)],
        compiler_params=_params(VMEM_DENSE_MIB),
        name="sgu_out",
    )(u, vn, wm_bf, bias, g_sgu_out, an, w_out_bf, x)


def _odd_even_merge_sort_pairs(n):
    pairs = []

    def merge(lo, length, r):
        step = 2 * r
        if step < length:
            merge(lo, length, step)
            merge(lo + r, length, step)
            pairs.extend((i, i + r) for i in range(lo + r, lo + length - r, step))
        else:
            pairs.append((lo, lo + r))

    def sort(lo, length):
        if length > 1:
            half = length // 2
            sort(lo, half)
            sort(lo + half, half)
            merge(lo, length, 1)

    sort(0, n)
    return pairs


_BIG_ROW = 2 ** 30


def _top16_rows(s, tag=None, tag_bits=0):
    depth = s.shape[0] // SUBLANES
    row8 = lax.broadcasted_iota(jnp.int32, (SUBLANES, s.shape[1]), 0)
    vals = [s[d * SUBLANES:(d + 1) * SUBLANES, :] for d in range(depth)]
    pays = [(row8 + d * SUBLANES) << tag_bits for d in range(depth)]
    if tag is not None:
        pays = [p | tag[d * SUBLANES:(d + 1) * SUBLANES, :] for d, p in enumerate(pays)]
    for i, j in _odd_even_merge_sort_pairs(depth):
        a, b, pa, pb = vals[i], vals[j], pays[i], pays[j]
        swap = (b > a) | ((b == a) & (pb < pa))
        vals[i], vals[j] = jnp.where(swap, b, a), jnp.where(swap, a, b)
        pays[i], pays[j] = jnp.where(swap, pb, pa), jnp.where(swap, pa, pb)
    out_v, out_p = [], []
    for k in range(PEER_TOPK):
        head, head_p = vals[0], pays[0]
        m = jnp.max(head, axis=0, keepdims=True)
        p = jnp.min(jnp.where(head == m, head_p, _BIG_ROW), axis=0, keepdims=True)
        out_v.append(m)
        out_p.append(p)
        popped = head_p == p
        for d in range(min(depth, PEER_TOPK - 1 - k)):
            below_v, below_p = (vals[d + 1], pays[d + 1]) if d + 1 < depth else (-jnp.inf, _BIG_ROW)
            vals[d] = jnp.where(popped, below_v, vals[d])
            pays[d] = jnp.where(popped, below_p, pays[d])
    return jnp.concatenate(out_v, axis=0), jnp.concatenate(out_p, axis=0)


_CAND_WIDTHS = (16, 8, 8, 8, 4, 4, 4, 4)


def _staircase(a, b, combine):
    rows = [combine(a[i:i + 1, :], b[:w, :]) for i, w in enumerate(_CAND_WIDTHS)]
    rows.append(combine(a[len(_CAND_WIDTHS):, :], b[0:1, :]))
    return jnp.concatenate(rows, axis=0)


def _peer_topk_kernel(h_ref, g_ref, wq_ref, k1_ref, k2_ref, xn_ref, idx_ref, gate_ref):
    xn = _rms(h_ref[...], g_ref[...])
    xn_ref[...] = xn
    qp = jnp.dot(xn.astype(BF16), wq_ref[...], preferred_element_type=F32).astype(BF16)
    idx_parts, gate_parts = [], []
    nt = (((1,), (1,)), ((), ()))
    for h in range(PEER_HEADS):
        base = h * 2 * PEER_HALF
        s1 = lax.dot_general(k1_ref[h], qp[:, base:base + PEER_HALF], nt,
                             preferred_element_type=F32)
        s2 = lax.dot_general(k2_ref[h], qp[:, base + PEER_HALF:base + 2 * PEER_HALF], nt,
                             preferred_element_type=F32)
        t1, i1 = _top16_rows(s1)
        t2, i2 = _top16_rows(s2)
        cand = _staircase(t1, t2, lambda a, b: a + b)
        expert = _staircase(i1, i2, lambda a, b: a * PEER_NKEYS + b)
        expert_bits = (PEER_NKEYS * PEER_NKEYS - 1).bit_length()
        sc, code = _top16_rows(cand, expert, expert_bits)
        idx_parts.append(code & ((1 << expert_bits) - 1))
        ex = jnp.exp(sc - sc[0:1, :])
        gate_parts.append(ex / jnp.sum(ex, axis=0, keepdims=True))
    idx_t = jnp.concatenate(idx_parts, axis=0)
    gate_t = jnp.concatenate(gate_parts, axis=0)
    tb = h_ref.shape[0]
    for c in range(tb // LANES):
        idx_ref[c * LANES:(c + 1) * LANES, :] = idx_t[:, c * LANES:(c + 1) * LANES].T
        gate_ref[c * LANES:(c + 1) * LANES, :] = gate_t[:, c * LANES:(c + 1) * LANES].T


def _peer_topk(h, g_ffn, w_pq_bf, k1_bf, k2_bf, tb):
    n = h.shape[0]
    tok = lambda w: pl.BlockSpec((tb, w), lambda i: (i, 0))
    return pl.pallas_call(
        _peer_topk_kernel,
        grid=(n // tb,),
        in_specs=[tok(D_MODEL), _const_spec((1, D_MODEL)),
                  _const_spec((D_MODEL, PEER_HEADS * 2 * PEER_HALF)),
                  _const_spec((PEER_HEADS, PEER_NKEYS, PEER_HALF)),
                  _const_spec((PEER_HEADS, PEER_NKEYS, PEER_HALF))],
        out_specs=[tok(D_MODEL), tok(PEER_PICKS), tok(PEER_PICKS)],
        out_shape=[jax.ShapeDtypeStruct((n, D_MODEL), F32),
                   jax.ShapeDtypeStruct((n, PEER_PICKS), jnp.int32),
                   jax.ShapeDtypeStruct((n, PEER_PICKS), F32)],
        compiler_params=_params(VMEM_DENSE_MIB),
        name="peer_topk",
    )(h, g_ffn, w_pq_bf, k1_bf, k2_bf)


PEER_RING = 16
PEER_CHUNKS = D_MODEL // LANES
PEER_EGROUPS = PEER_PICKS // SUBLANES
PEER_PITCH = 20

U_MASK = 0xFFFF0000


def _pack_uv(u_tab, v_tab):
    hi = lax.bitcast_convert_type(u_tab.astype(BF16), jnp.uint16).astype(jnp.uint32) << 16
    lo = lax.bitcast_convert_type(v_tab.astype(BF16), jnp.uint16).astype(jnp.uint32)
    return (hi | lo).reshape(u_tab.shape[0], PEER_CHUNKS, LANES)


def _layer_tail(h, p, g_gate, w_gate, w_ple, g_ple, g_final):
    gate = jax.nn.sigmoid(jnp.dot(_rms(h, g_gate).astype(BF16), w_gate, preferred_element_type=F32))
    ple = jnp.dot(p.astype(BF16), w_ple, preferred_element_type=F32)
    return _rms(h + _rms(ple, g_ple) * gate, g_final)


def _peer_expert_kernel(idx_ref, nidx_ref, xn_ref, gate_ref, h_ref, p_ref, gg_ref, wg_ref, wp_ref, gp_ref, gf_ref,
                        tab_hbm, y_ref, *scratch, tokens):
    bufs, (stage, hk_scr, h2, sem) = scratch[:PEER_RING], scratch[PEER_RING:]

    def row_copy(slot, e, r):
        return pltpu.make_async_copy(tab_hbm.at[r], bufs[slot].at[pl.ds(e * PEER_PITCH, PEER_CHUNKS), :],
                                     sem.at[slot])

    def issue(src_idx_ref, t, slot):
        for e in range(PEER_PICKS):
            row_copy(slot, e, src_idx_ref[t, e]).start(priority=e % 2)

    def wait(slot):
        for e in range(PEER_PICKS):
            row_copy(slot, e, 0).wait()

    sub = lax.broadcasted_iota(jnp.int32, (SUBLANES, LANES), 0)
    lane_minus_sub = lax.broadcasted_iota(jnp.int32, (SUBLANES, LANES), 1) - sub

    def picks_tile(slot, g, c):
        return bufs[slot][pl.ds(g * SUBLANES * PEER_PITCH + c, SUBLANES, stride=PEER_PITCH), :]

    def u_pass(t, slot):
        row = pl.ds(t, 1)
        x_row = xn_ref[row, :]
        xb = [jnp.broadcast_to(x_row[:, c * LANES:(c + 1) * LANES], (SUBLANES, LANES))
              for c in range(PEER_CHUNKS)]
        a_diag = jnp.zeros((SUBLANES, LANES), F32)
        for g in range(PEER_EGROUPS):
            acc = None
            for c in range(PEER_CHUNKS):
                u = pltpu.bitcast(picks_tile(slot, g, c) & jnp.uint32(U_MASK), F32)
                acc = u * xb[c] if acc is None else acc + u * xb[c]
            a_g = jnp.sum(acc, axis=1, keepdims=True)
            a_diag = jnp.where(lane_minus_sub == g * SUBLANES, a_g, a_diag)
        a_row = jnp.sum(a_diag, axis=0, keepdims=True)
        hk_scr[slot:slot + 1, :] = _gelu(a_row) * gate_ref[row, :]

    def v_pass(t, slot):
        row = pl.ds(t, 1)
        hk_row = jnp.broadcast_to(hk_scr[slot:slot + 1, :], (SUBLANES, LANES))
        hk_g = [jnp.broadcast_to(
                    jnp.sum(jnp.where(lane_minus_sub == g * SUBLANES, hk_row, 0.0), axis=1, keepdims=True),
                    (SUBLANES, LANES))
                for g in range(PEER_EGROUPS)]
        for c in range(PEER_CHUNKS):
            acc = None
            for g in range(PEER_EGROUPS):
                v = pltpu.bitcast(picks_tile(slot, g, c) << 16, F32)
                acc = v * hk_g[g] if acc is None else acc + v * hk_g[g]
            stage[slot:slot + 1, c * LANES:(c + 1) * LANES] = jnp.sum(acc, axis=0, keepdims=True)
        h2[row, :] = h_ref[row, :] + stage[slot:slot + 1, :]

    step, last_step = pl.program_id(0), pl.num_programs(0) - 1
    ahead = PEER_RING - 2

    def phase(k, s, do_v):
        wait(s)
        j, slot = k + ahead, (s + ahead) % PEER_RING
        if isinstance(j, int) and j >= tokens:
            @pl.when(step < last_step)
            def _():
                issue(nidx_ref, j - tokens, slot)
        else:
            issue(idx_ref, j, slot)
        u_pass(k, s)
        if do_v:
            v_pass(k - 1, (s - 1) % PEER_RING)

    def static_group(g):
        for s in range(PEER_RING):
            k = g * PEER_RING + s
            phase(k, s, k >= 1)

    def group(g, carry):
        for s in range(PEER_RING):
            phase(g * PEER_RING + s, s, True)
        return carry

    @pl.when(step == 0)
    def _():
        for s in range(ahead):
            issue(idx_ref, s, s)

    n_groups = tokens // PEER_RING
    static_group(0)
    if n_groups > 1:
        lax.fori_loop(1, n_groups - 1, group, 0)
        static_group(n_groups - 1)
    v_pass(tokens - 1, (tokens - 1) % PEER_RING)
    y_ref[...] = _layer_tail(h2[...], p_ref[...], gg_ref[...], wg_ref[...], wp_ref[...], gp_ref[...], gf_ref[...])


def _peer_expert(idx, xn, gate, h, tab, p, g_gate, w_gate_bf, w_ple_bf, g_ple, g_final, tokens):
    n = h.shape[0]
    assert tokens % PEER_RING == 0 and n % tokens == 0
    tok = lambda w: pl.BlockSpec((tokens, w), lambda i: (i, 0))
    kern = functools.partial(_peer_expert_kernel, tokens=tokens)
    steps = n // tokens
    return pl.pallas_call(
        kern,
        grid=(steps,),
        in_specs=[pl.BlockSpec((tokens, PEER_PICKS), lambda i: (i, 0), memory_space=pltpu.SMEM),
                  pl.BlockSpec((tokens, PEER_PICKS), lambda i: (jnp.minimum(i + 1, steps - 1), 0),
                               memory_space=pltpu.SMEM),
                  tok(D_MODEL), tok(PEER_PICKS), tok(D_MODEL), tok(PLE_DIM),
                  _const_spec((1, D_MODEL)), _const_spec((D_MODEL, D_MODEL)), _const_spec((PLE_DIM, D_MODEL)),
                  _const_spec((1, D_MODEL)), _const_spec((1, D_MODEL)),
                  pl.BlockSpec(memory_space=pl.ANY)],
        out_specs=tok(D_MODEL),
        out_shape=jax.ShapeDtypeStruct((n, D_MODEL), F32),
        scratch_shapes=[pltpu.VMEM((PEER_PICKS * PEER_PITCH, LANES), jnp.uint32)
                        for _ in range(PEER_RING)] + [
                        pltpu.VMEM((PEER_RING, D_MODEL), F32),
                        pltpu.VMEM((PEER_RING, PEER_PICKS), F32),
                        pltpu.VMEM((tokens, D_MODEL), F32),
                        pltpu.SemaphoreType.DMA((PEER_RING,))],
        compiler_params=_params(VMEM_PEER_MIB),
        name="peer_expert",
    )(idx, idx, xn, gate, h, p, g_gate, w_gate_bf, w_ple_bf, g_ple, g_final, tab)


def _rope_tables(pos):
    half = HEAD_DIM // 2
    inv_freq = 1.0 / (ROPE_THETA ** (jnp.arange(0, HEAD_DIM, 2, dtype=F32) / HEAD_DIM))
    ang = pos.astype(F32)[:, None] * inv_freq[None, :]
    cos, sin = jnp.cos(ang), jnp.sin(ang)
    reps = LANES // HEAD_DIM
    return (jnp.tile(jnp.concatenate([cos, cos], axis=1), (1, reps)),
            jnp.tile(jnp.concatenate([-sin, sin], axis=1), (1, reps)))


def _row(v):
    return v.reshape(1, -1).astype(F32)


def _token_stages(x, p, cos_t, sin_t, attn_fn, wm_bf, bias, lw, tb):
    q, k, v, u, vn = _in_proj(x, lw["g_mix"], lw["w_in"], cos_t, sin_t, lw["g_sgu"], lw["b_sgu"], tb)
    an = attn_fn(q, k, v)
    h = _sgu_out(u, vn, wm_bf, bias, lw["g_sgu_out"], an, lw["w_out"], x, tb)
    xn, idx, gate = _peer_topk(h, lw["g_ffn"], lw["w_pq"], lw["sub_k1"], lw["sub_k2"], tb)
    y = _peer_expert(idx, xn, gate, h, lw["uv_tab"], p, *lw["tail"], min(256, x.shape[0]))
    return y, k, v, vn


def kernel(x_prompt, x_sample, cache_k, cache_v, p_prompt, p_sample, g_mix, w_in, sinks, g_sgu, b_sgu,
           w_s, b_s, g_attn_out, g_sgu_out, w_out, g_ffn, w_pq, sub_k1, sub_k2, u_tab, v_tab, g_gate,
           w_gate, w_ple, g_ple, g_final):
    depth = w_in.shape[0]
    bsz, seq, _ = x_prompt.shape
    dbsz, dseq, _ = x_sample.shape
    wc = cache_k.shape[2]
    assert depth == 1, "the final kernel fuses the layer tail with the final norm"
    assert seq % WINDOW == 0 and wc == WINDOW and WINDOW % dseq == 0 and dseq <= SGU_CHUNK
    n_p, n_s = bsz * seq, dbsz * dseq
    tb_p = 256 if seq % 256 == 0 else WINDOW
    tb_s = WINDOW
    assert n_s % tb_s == 0 and seq % tb_p == 0
    nb = seq // WINDOW

    cos_p, sin_p = _rope_tables(jnp.arange(seq, dtype=jnp.int32))
    cos_s, sin_s = _rope_tables(PAST_LEN + (jnp.arange(tb_s, dtype=jnp.int32) % dseq))
    causal = jnp.tril(jnp.ones((SGU_CHUNK, SGU_CHUNK), F32))
    eye = jnp.eye(SGU_CHUNK // dseq, dtype=F32)

    hp = x_prompt.reshape(n_p, D_MODEL)
    hs = x_sample.reshape(n_s, D_MODEL)
    kp_l, vp_l, ks_l, vs_l, gs_l = [], [], [], [], []
    for i in range(depth):
        lw = dict(
            g_mix=_row(g_mix[i]), w_in=w_in[i].astype(BF16), g_sgu=_row(g_sgu[i]), b_sgu=_row(b_sgu[i]),
            g_sgu_out=_row(g_sgu_out[i]), w_out=w_out[i].astype(BF16), g_ffn=_row(g_ffn[i]),
            w_pq=w_pq[i].astype(BF16), sub_k1=sub_k1[i].astype(BF16), sub_k2=sub_k2[i].astype(BF16),
            uv_tab=_pack_uv(u_tab[i], v_tab[i]),
            tail=(_row(g_gate[i]), w_gate[i].astype(BF16), w_ple[i].astype(BF16), _row(g_ple[i]), _row(g_final)))
        sinks_i = sinks[i].astype(F32)
        g_ao = _row(g_attn_out[i])
        ws = w_s[i].astype(F32) * causal
        wm_p = ws.astype(BF16)
        bias_p = jnp.broadcast_to(b_s[i].astype(F32)[:, :, None], (SGU_HEADS, SGU_CHUNK, SGU_GROUP))
        wm_s = jnp.einsum("ab,gts->gatbs", eye, ws[:, :dseq, :dseq]).reshape(
            SGU_HEADS, SGU_CHUNK, SGU_CHUNK).astype(BF16)
        bias_s = jnp.broadcast_to(jnp.tile(b_s[i].astype(F32)[:, :dseq], (1, SGU_CHUNK // dseq))[:, :, None],
                                  (SGU_HEADS, SGU_CHUNK, SGU_GROUP))

        def attn_prompt(q, k, v):
            return _attention(sinks_i, q, k, v, g_ao, nb)

        def attn_sample(q, k, v, ck=cache_k[i], cv=cache_v[i]):
            return _attention_sample(sinks_i, q, ck.reshape(dbsz * wc, KV_COLS), cv.reshape(dbsz * wc, KV_COLS),
                                     k, v, g_ao, dseq, SGU_CHUNK // dseq)

        hp, kp, vp, _ = _token_stages(hp, p_prompt[i].reshape(n_p, PLE_DIM), cos_p, sin_p, attn_prompt,
                                      wm_p, bias_p, lw, tb_p)
        hs, ks, vs, vns = _token_stages(hs, p_sample[i].reshape(n_s, PLE_DIM), cos_s, sin_s, attn_sample,
                                        wm_s, bias_s, lw, tb_s)

        tail = lambda a: a.reshape(bsz, seq, KV_COLS)[:, seq - WINDOW:].reshape(bsz, WINDOW, N_KV, HEAD_DIM)
        kp_l.append(tail(kp))
        vp_l.append(tail(vp))
        ks_l.append(jnp.concatenate([cache_k[i][:, dseq:], ks.reshape(dbsz, dseq, N_KV, HEAD_DIM)], axis=1))
        vs_l.append(jnp.concatenate([cache_v[i][:, dseq:], vs.reshape(dbsz, dseq, N_KV, HEAD_DIM)], axis=1))
        gs_l.append(vns.reshape(dbsz, dseq, SGU_HEADS, SGU_GROUP))

    return (hp.reshape(bsz, seq, D_MODEL), hs.reshape(dbsz, dseq, D_MODEL),
            jnp.stack(kp_l), jnp.stack(vp_l), jnp.stack(ks_l), jnp.stack(vs_l), jnp.stack(gs_l))
```

```python
import functools

import jax
import jax.numpy as jnp
from jax import lax
from jax.experimental import pallas as pl
from jax.experimental.pallas import tpu as pltpu

F32 = jnp.float32
BF16 = jnp.bfloat16

D_MODEL = 2048
PAST_LEN = 16384
HEAD_DIM = 64
N_HEADS = 16
N_KV = 2
Q_PER_KV = N_HEADS // N_KV
WINDOW = 128
ROPE_THETA = 10000.0
NEG_INF = -1e30
ATTN_WIDTH = N_HEADS * HEAD_DIM
SGU_CHUNK = 128
SGU_GROUP = 128
SGU_HEADS = 8
SGU_WIDTH = SGU_HEADS * SGU_GROUP
Q_COLS = N_HEADS * HEAD_DIM
KV_COLS = N_KV * HEAD_DIM
IN_COLS = Q_COLS + 2 * KV_COLS + 2 * SGU_WIDTH
PEER_HEADS = 8
PEER_NKEYS = 128
PEER_HALF = 128
PEER_TOPK = 16
PEER_PICKS = PEER_HEADS * PEER_TOPK
PLE_DIM = 256
EPS = 1e-6

LANES = 128
SUBLANES = 8
MIB = 1024 * 1024
VMEM_DENSE_MIB = 48
VMEM_SMALL_MIB = 32


def _rms(x, g):
    return x * lax.rsqrt(jnp.mean(x * x, axis=-1, keepdims=True) + EPS) * g


def _gelu(x):
    return 0.5 * x * (1.0 + lax.erf(x * (0.5 ** 0.5)))


def _const_spec(shape):
    nd = len(shape)
    return pl.BlockSpec(shape, lambda *_: (0,) * nd, pipeline_mode=pl.Buffered(1))


def _params(vmem_mib):
    return pltpu.CompilerParams(dimension_semantics=("arbitrary",), vmem_limit_bytes=vmem_mib * MIB)


def _in_proj_kernel(x_ref, g_ref, w_ref, cos_ref, sin_ref, gs_ref, bs_ref,
                    q_ref, k_ref, v_ref, u_ref, vn_ref):
    xn = _rms(x_ref[...], g_ref[...]).astype(BF16)
    z = jnp.dot(xn, w_ref[...], preferred_element_type=F32)
    cos = cos_ref[...]
    sin = sin_ref[...]
    lane = lax.broadcasted_iota(jnp.int32, cos.shape, 1)
    first_half = (lane % HEAD_DIM) < (HEAD_DIM // 2)

    def rope(zc):
        swapped = jnp.where(first_half,
                            pltpu.roll(zc, LANES - HEAD_DIM // 2, 1),
                            pltpu.roll(zc, HEAD_DIM // 2, 1))
        return zc * cos + swapped * sin

    for c in range(Q_COLS // LANES):
        q_ref[:, c * LANES:(c + 1) * LANES] = rope(z[:, c * LANES:(c + 1) * LANES])
    k_ref[...] = rope(z[:, Q_COLS:Q_COLS + KV_COLS])
    v_ref[...] = z[:, Q_COLS + KV_COLS:Q_COLS + 2 * KV_COLS]
    u0 = Q_COLS + 2 * KV_COLS
    u_ref[...] = _gelu(z[:, u0:u0 + SGU_WIDTH])
    sv = _gelu(z[:, u0 + SGU_WIDTH:u0 + 2 * SGU_WIDTH])
    mu = jnp.mean(sv, axis=-1, keepdims=True)
    svc = sv - mu
    var = jnp.mean(svc * svc, axis=-1, keepdims=True)
    vn_ref[...] = svc * lax.rsqrt(var + EPS) * gs_ref[...] + bs_ref[...]


def _in_proj(x, g_mix, w_in_bf, cos_t, sin_t, g_sgu, b_sgu, tb):
    n = x.shape[0]
    period_blocks = cos_t.shape[0] // tb
    tok = lambda w: pl.BlockSpec((tb, w), lambda i: (i, 0))
    tab = pl.BlockSpec((tb, LANES), lambda i: (i % period_blocks, 0))
    return pl.pallas_call(
        _in_proj_kernel,
        grid=(n // tb,),
        in_specs=[tok(D_MODEL), _const_spec((1, D_MODEL)), _const_spec((D_MODEL, IN_COLS)),
                  tab, tab, _const_spec((1, SGU_WIDTH)), _const_spec((1, SGU_WIDTH))],
        out_specs=[tok(Q_COLS), tok(KV_COLS), tok(KV_COLS), tok(SGU_WIDTH), tok(SGU_WIDTH)],
        out_shape=[jax.ShapeDtypeStruct((n, w), F32)
                   for w in (Q_COLS, KV_COLS, KV_COLS, SGU_WIDTH, SGU_WIDTH)],
        compiler_params=_params(VMEM_DENSE_MIB),
        name="in_proj",
    )(x, g_mix, w_in_bf, cos_t, sin_t, g_sgu, b_sgu)


def _attn_kernel(sinks_ref, q_ref, ka_ref, kb_ref, va_ref, vb_ref, g_ref, o_ref, acc_ref, *, blocks_per_seq):
    q = q_ref[...]
    kk_t_f32 = jnp.concatenate([ka_ref[...].T, kb_ref[...].T], axis=1)
    vv = jnp.concatenate([va_ref[...], vb_ref[...]], axis=0).astype(BF16)
    r = lax.broadcasted_iota(jnp.int32, (WINDOW, 2 * WINDOW), 0)
    c = lax.broadcasted_iota(jnp.int32, (WINDOW, 2 * WINDOW), 1)
    first = pl.program_id(0) % blocks_per_seq == 0
    valid = (c > jnp.where(first, jnp.maximum(r, WINDOW - 1), r)) & (c <= r + WINDOW)
    kk_t = kk_t_f32.astype(BF16)
    for g in range(N_KV):
        kg_t = kk_t[g * HEAD_DIM:(g + 1) * HEAD_DIM, :]
        vg = vv[:, g * HEAD_DIM:(g + 1) * HEAD_DIM]
        for hh in range(Q_PER_KV):
            h = g * Q_PER_KV + hh
            qh = q[:, h * HEAD_DIM:(h + 1) * HEAD_DIM].astype(BF16)
            s = jnp.dot(qh, kg_t, preferred_element_type=F32) * (HEAD_DIM ** -0.5)
            s = jnp.where(valid, s, NEG_INF)
            sink = sinks_ref[h]
            m = jnp.maximum(jnp.max(s, axis=1, keepdims=True), sink)
            p = jnp.exp(s - m)
            denom = jnp.sum(p, axis=1, keepdims=True) + jnp.exp(sink - m)
            p = (p / denom).astype(BF16)
            acc_ref[:, h * HEAD_DIM:(h + 1) * HEAD_DIM] = jnp.dot(p, vg, preferred_element_type=F32)
    o_ref[...] = _rms(acc_ref[...], g_ref[...]).astype(o_ref.dtype)


def _attention(sinks, q, k, v, g_attn_out, blocks_per_seq):
    n = q.shape[0]
    prev = pl.BlockSpec((WINDOW, KV_COLS), lambda j: (jnp.where(j % blocks_per_seq == 0, j, j - 1), 0))
    cur = pl.BlockSpec((WINDOW, KV_COLS), lambda j: (j, 0))
    kern = functools.partial(_attn_kernel, blocks_per_seq=blocks_per_seq)
    return pl.pallas_call(
        kern,
        grid=(n // WINDOW,),
        in_specs=[pl.BlockSpec(memory_space=pltpu.SMEM),
                  pl.BlockSpec((WINDOW, Q_COLS), lambda i: (i, 0)),
                  prev, cur, prev, cur,
                  _const_spec((1, ATTN_WIDTH))],
        out_specs=pl.BlockSpec((WINDOW, ATTN_WIDTH), lambda i: (i, 0)),
        out_shape=jax.ShapeDtypeStruct((n, ATTN_WIDTH), BF16),
        scratch_shapes=[pltpu.VMEM((WINDOW, ATTN_WIDTH), F32)],
        compiler_params=_params(VMEM_SMALL_MIB),
        name="attention",
    )(sinks, q, k, k, v, v, g_attn_out)


def _attn_sample_kernel(sinks_ref, q_ref, ck_ref, cv_ref, kn_ref, vn_ref, g_ref, o_ref, acc_ref, *, dseq, seqs):
    rows = Q_PER_KV * dseq
    r = lax.broadcasted_iota(jnp.int32, (rows, WINDOW), 0) % dseq
    c = lax.broadcasted_iota(jnp.int32, (rows, WINDOW), 1)
    valid_cache = c > r
    valid_new = c <= r
    head_of_row = lax.broadcasted_iota(jnp.int32, (rows, 1), 0) // dseq
    sink_cols = []
    for g in range(N_KV):
        col = jnp.zeros((rows, 1), F32)
        for hh in range(Q_PER_KV):
            col = jnp.where(head_of_row == hh, sinks_ref[g * Q_PER_KV + hh], col)
        sink_cols.append(col)
    pad = jnp.zeros((WINDOW - dseq, HEAD_DIM), F32)
    nt = (((1,), (1,)), ((), ()))
    for i in range(seqs):
        new = slice(i * dseq, (i + 1) * dseq)
        old = slice(i * WINDOW, (i + 1) * WINDOW)
        q = q_ref[new, :]
        for g in range(N_KV):
            cols = slice(g * HEAD_DIM, (g + 1) * HEAD_DIM)
            qg = jnp.concatenate([q[:, (g * Q_PER_KV + hh) * HEAD_DIM:(g * Q_PER_KV + hh + 1) * HEAD_DIM]
                                  for hh in range(Q_PER_KV)], axis=0).astype(BF16)
            k_new = jnp.concatenate([kn_ref[new, cols], pad], axis=0).astype(BF16)
            v_new = jnp.concatenate([vn_ref[new, cols], pad], axis=0).astype(BF16)
            s_old = lax.dot_general(qg, ck_ref[old, cols].astype(BF16), nt,
                                    preferred_element_type=F32) * (HEAD_DIM ** -0.5)
            s_new = lax.dot_general(qg, k_new, nt, preferred_element_type=F32) * (HEAD_DIM ** -0.5)
            s_old = jnp.where(valid_cache, s_old, NEG_INF)
            s_new = jnp.where(valid_new, s_new, NEG_INF)
            sink = sink_cols[g]
            m = jnp.maximum(jnp.maximum(jnp.max(s_old, axis=1, keepdims=True),
                                        jnp.max(s_new, axis=1, keepdims=True)), sink)
            p_old = jnp.exp(s_old - m)
            p_new = jnp.exp(s_new - m)
            denom = (jnp.sum(p_old, axis=1, keepdims=True) + jnp.sum(p_new, axis=1, keepdims=True)
                     + jnp.exp(sink - m))
            o = (jnp.dot((p_old / denom).astype(BF16), cv_ref[old, cols].astype(BF16),
                         preferred_element_type=F32)
                 + jnp.dot((p_new / denom).astype(BF16), v_new, preferred_element_type=F32))
            for hh in range(Q_PER_KV):
                h = g * Q_PER_KV + hh
                acc_ref[new, h * HEAD_DIM:(h + 1) * HEAD_DIM] = o[hh * dseq:(hh + 1) * dseq, :]
    o_ref[...] = _rms(acc_ref[...], g_ref[...]).astype(o_ref.dtype)


def _attention_sample(sinks, q, cache_k, cache_v, k_new, v_new, g_attn_out, dseq, seqs):
    n = q.shape[0]
    tok = lambda w: pl.BlockSpec((seqs * dseq, w), lambda i: (i, 0))
    cache = pl.BlockSpec((seqs * WINDOW, KV_COLS), lambda i: (i, 0))
    kern = functools.partial(_attn_sample_kernel, dseq=dseq, seqs=seqs)
    return pl.pallas_call(
        kern,
        grid=(n // (seqs * dseq),),
        in_specs=[pl.BlockSpec(memory_space=pltpu.SMEM), tok(Q_COLS), cache, cache, tok(KV_COLS), tok(KV_COLS),
                  _const_spec((1, ATTN_WIDTH))],
        out_specs=tok(ATTN_WIDTH),
        out_shape=jax.ShapeDtypeStruct((n, ATTN_WIDTH), BF16),
        scratch_shapes=[pltpu.VMEM((seqs * dseq, ATTN_WIDTH), F32)],
        compiler_params=_params(VMEM_SMALL_MIB),
        name="attention_sample",
    )(sinks, q, cache_k, cache_v, k_new, v_new, g_attn_out)


def _sgu_out_kernel(u_ref, vn_ref, wm_ref, bias_ref, g_ref, an_ref, wo_ref, x_ref, h_ref, s_ref):
    tb = u_ref.shape[0]
    for ch in range(tb // SGU_CHUNK):
        rows = slice(ch * SGU_CHUNK, (ch + 1) * SGU_CHUNK)
        for g in range(SGU_HEADS):
            cols = slice(g * SGU_GROUP, (g + 1) * SGU_GROUP)
            mix = jnp.dot(wm_ref[g], vn_ref[rows, cols].astype(BF16),
                          preferred_element_type=F32) + bias_ref[g]
            s_ref[rows, cols] = u_ref[rows, cols] * mix
    sn = _rms(s_ref[...], g_ref[...]).astype(BF16)
    mixed = jnp.dot(an_ref[...], wo_ref[:ATTN_WIDTH, :], preferred_element_type=F32)
    mixed = mixed + jnp.dot(sn, wo_ref[ATTN_WIDTH:, :], preferred_element_type=F32)
    h_ref[...] = x_ref[...] + mixed


def _sgu_out(u, vn, wm_bf, bias, g_sgu_out, an, w_out_bf, x, tb):
    n = x.shape[0]
    tok = lambda w: pl.BlockSpec((tb, w), lambda i: (i, 0))
    return pl.pallas_call(
        _sgu_out_kernel,
        grid=(n // tb,),
        in_specs=[tok(SGU_WIDTH), tok(SGU_WIDTH),
                  _const_spec((SGU_HEADS, SGU_CHUNK, SGU_CHUNK)),
                  _const_spec((SGU_HEADS, SGU_CHUNK, SGU_GROUP)),
                  _const_spec((1, SGU_WIDTH)), tok(ATTN_WIDTH),
                  _const_spec((D_MODEL, D_MODEL)), tok(D_MODEL)],
        out_specs=tok(D_MODEL),
        out_shape=jax.ShapeDtypeStruct((n, D_MODEL), F32),
        scratch_shapes=[pltpu.VMEM((tb, SGU_WIDTH), F32)],
        compiler_params=_params(VMEM_DENSE_MIB),
        name="sgu_out",
    )(u, vn, wm_bf, bias, g_sgu_out, an, w_out_bf, x)


def _odd_even_merge_sort_pairs(n):
    pairs = []

    def merge(lo, length, r):
        step = 2 * r
        if step < length:
            merge(lo, length, step)
            merge(lo + r, length, step)
            pairs.extend((i, i + r) for i in range(lo + r, lo + length - r, step))
        else:
            pairs.append((lo, lo + r))

    def sort(lo, length):
        if length > 1:
            half = length // 2
            sort(lo, half)
            sort(lo + half, half)
            merge(lo, length, 1)

    sort(0, n)
    return pairs


_BIG_ROW = 2 ** 30


def _top16_rows(s, tag=None, tag_bits=0):
    depth = s.shape[0] // SUBLANES
    row8 = lax.broadcasted_iota(jnp.int32, (SUBLANES, s.shape[1]), 0)
    vals = [s[d * SUBLANES:(d + 1) * SUBLANES, :] for d in range(depth)]
    pays = [(row8 + d * SUBLANES) << tag_bits for d in range(depth)]
    if tag is not None:
        pays = [p | tag[d * SUBLANES:(d + 1) * SUBLANES, :] for d, p in enumerate(pays)]
    for i, j in _odd_even_merge_sort_pairs(depth):
        a, b, pa, pb = vals[i], vals[j], pays[i], pays[j]
        swap = (b > a) | ((b == a) & (pb < pa))
        vals[i], vals[j] = jnp.where(swap, b, a), jnp.where(swap, a, b)
        pays[i], pays[j] = jnp.where(swap, pb, pa), jnp.where(swap, pa, pb)
    out_v, out_p = [], []
    for k in range(PEER_TOPK):
        head, head_p = vals[0], pays[0]
        m = jnp.max(head, axis=0, keepdims=True)
        p = jnp.min(jnp.where(head == m, head_p, _BIG_ROW), axis=0, keepdims=True)
        out_v.append(m)
        out_p.append(p)
        popped = head_p == p
        for d in range(min(depth, PEER_TOPK - 1 - k)):
            below_v, below_p = (vals[d + 1], pays[d + 1]) if d + 1 < depth else (-jnp.inf, _BIG_ROW)
            vals[d] = jnp.where(popped, below_v, vals[d])
            pays[d] = jnp.where(popped, below_p, pays[d])
    return jnp.concatenate(out_v, axis=0), jnp.concatenate(out_p, axis=0)


_CAND_WIDTHS = (16, 8, 8, 8, 4, 4, 4, 4)


def _staircase(a, b, combine):
    rows = [combine(a[i:i + 1, :], b[:w, :]) for i, w in enumerate(_CAND_WIDTHS)]
    rows.append(combine(a[len(_CAND_WIDTHS):, :], b[0:1, :]))
    return jnp.concatenate(rows, axis=0)


def _peer_topk_kernel(h_ref, g_ref, wq_ref, k1_ref, k2_ref, idx_ref, gate_ref):
    xn = _rms(h_ref[...], g_ref[...])
    qp = jnp.dot(xn.astype(BF16), wq_ref[...], preferred_element_type=F32).astype(BF16)
    idx_parts, gate_parts = [], []
    nt = (((1,), (1,)), ((), ()))
    for h in range(PEER_HEADS):
        base = h * 2 * PEER_HALF
        s1 = lax.dot_general(k1_ref[h], qp[:, base:base + PEER_HALF], nt,
                             preferred_element_type=F32)
        s2 = lax.dot_general(k2_ref[h], qp[:, base + PEER_HALF:base + 2 * PEER_HALF], nt,
                             preferred_element_type=F32)
        t1, i1 = _top16_rows(s1)
        t2, i2 = _top16_rows(s2)
        cand = _staircase(t1, t2, lambda a, b: a + b)
        expert = _staircase(i1, i2, lambda a, b: a * PEER_NKEYS + b)
        expert_bits = (PEER_NKEYS * PEER_NKEYS - 1).bit_length()
        sc, code = _top16_rows(cand, expert, expert_bits)
        idx_parts.append(code & ((1 << expert_bits) - 1))
        ex = jnp.exp(sc - sc[0:1, :])
        gate_parts.append(ex / jnp.sum(ex, axis=0, keepdims=True))
    idx_t = jnp.concatenate(idx_parts, axis=0)
    gate_t = jnp.concatenate(gate_parts, axis=0)
    tb = h_ref.shape[0]
    for c in range(tb // LANES):
        idx_ref[c * LANES:(c + 1) * LANES, :] = idx_t[:, c * LANES:(c + 1) * LANES].T
        gate_ref[c * LANES:(c + 1) * LANES, :] = gate_t[:, c * LANES:(c + 1) * LANES].T


def _peer_topk(h, g_ffn, w_pq_bf, k1_bf, k2_bf, tb):
    n = h.shape[0]
    tok = lambda w: pl.BlockSpec((tb, w), lambda i: (i, 0))
    return pl.pallas_call(
        _peer_topk_kernel,
        grid=(n // tb,),
        in_specs=[tok(D_MODEL), _const_spec((1, D_MODEL)),
                  _const_spec((D_MODEL, PEER_HEADS * 2 * PEER_HALF)),
                  _const_spec((PEER_HEADS, PEER_NKEYS, PEER_HALF)),
                  _const_spec((PEER_HEADS, PEER_NKEYS, PEER_HALF))],
        out_specs=[tok(PEER_PICKS), tok(PEER_PICKS)],
        out_shape=[jax.ShapeDtypeStruct((n, PEER_PICKS), jnp.int32),
                   jax.ShapeDtypeStruct((n, PEER_PICKS), F32)],
        compiler_params=_params(VMEM_DENSE_MIB),
        name="peer_topk",
    )(h, g_ffn, w_pq_bf, k1_bf, k2_bf)


PEER_RING = 16
PEER_CHUNKS = D_MODEL // LANES
PEER_EGROUPS = PEER_PICKS // SUBLANES
PEER_PITCH = 20

U_MASK = 0xFFFF0000


def _pack_uv(u_tab, v_tab):
    hi = lax.bitcast_convert_type(u_tab.astype(BF16), jnp.uint16).astype(jnp.uint32) << 16
    lo = lax.bitcast_convert_type(v_tab.astype(BF16), jnp.uint16).astype(jnp.uint32)
    return (hi | lo).reshape(u_tab.shape[0], PEER_CHUNKS, LANES)


def _peer_expert_kernel(idx_ref, nidx_ref, gate_ref, h_ref, gffn_ref, tab_hbm, o_ref, *scratch, tokens):
    bufs, (stage, hk_scr, sem) = scratch[:PEER_RING], scratch[PEER_RING:]

    def row_copy(slot, e, r):
        return pltpu.make_async_copy(tab_hbm.at[r], bufs[slot].at[pl.ds(e * PEER_PITCH, PEER_CHUNKS), :],
                                     sem.at[slot])

    def issue(src_idx_ref, t, slot):
        for e in range(PEER_PICKS):
            row_copy(slot, e, src_idx_ref[t, e]).start(priority=e % 2)

    def wait(slot):
        for e in range(PEER_PICKS):
            row_copy(slot, e, 0).wait()

    sub = lax.broadcasted_iota(jnp.int32, (SUBLANES, LANES), 0)
    lane_minus_sub = lax.broadcasted_iota(jnp.int32, (SUBLANES, LANES), 1) - sub

    def picks_tile(slot, g, c):
        return bufs[slot][pl.ds(g * SUBLANES * PEER_PITCH + c, SUBLANES, stride=PEER_PITCH), :]

    def u_pass(t, slot):
        row = pl.ds(t, 1)
        x_row = _rms(h_ref[row, :], gffn_ref[...])
        xb = [jnp.broadcast_to(x_row[:, c * LANES:(c + 1) * LANES], (SUBLANES, LANES))
              for c in range(PEER_CHUNKS)]
        a_diag = jnp.zeros((SUBLANES, LANES), F32)
        for g in range(PEER_EGROUPS):
            acc = None
            for c in range(PEER_CHUNKS):
                u = pltpu.bitcast(picks_tile(slot, g, c) & jnp.uint32(U_MASK), F32)
                acc = u * xb[c] if acc is None else acc + u * xb[c]
            a_g = jnp.sum(acc, axis=1, keepdims=True)
            a_diag = jnp.where(lane_minus_sub == g * SUBLANES, a_g, a_diag)
        a_row = jnp.sum(a_diag, axis=0, keepdims=True)
        hk_scr[slot:slot + 1, :] = _gelu(a_row) * gate_ref[row, :]

    def v_pass(t, slot):
        row = pl.ds(t, 1)
        hk_row = jnp.broadcast_to(hk_scr[slot:slot + 1, :], (SUBLANES, LANES))
        hk_g = [jnp.broadcast_to(
                    jnp.sum(jnp.where(lane_minus_sub == g * SUBLANES, hk_row, 0.0), axis=1, keepdims=True),
                    (SUBLANES, LANES))
                for g in range(PEER_EGROUPS)]
        for c in range(PEER_CHUNKS):
            acc = None
            for g in range(PEER_EGROUPS):
                v = pltpu.bitcast(picks_tile(slot, g, c) << 16, F32)
                acc = v * hk_g[g] if acc is None else acc + v * hk_g[g]
            stage[slot:slot + 1, c * LANES:(c + 1) * LANES] = jnp.sum(acc, axis=0, keepdims=True)
        o_ref[row, :] = h_ref[row, :] + stage[slot:slot + 1, :]

    step, last_step = pl.program_id(0), pl.num_programs(0) - 1
    ahead = PEER_RING - 2

    def phase(k, s, do_v):
        wait(s)
        j, slot = k + ahead, (s + ahead) % PEER_RING
        if isinstance(j, int) and j >= tokens:
            @pl.when(step < last_step)
            def _():
                issue(nidx_ref, j - tokens, slot)
        else:
            issue(idx_ref, j, slot)
        u_pass(k, s)
        if do_v:
            v_pass(k - 1, (s - 1) % PEER_RING)

    def static_group(g):
        for s in range(PEER_RING):
            k = g * PEER_RING + s
            phase(k, s, k >= 1)

    def group(g, carry):
        for s in range(PEER_RING):
            phase(g * PEER_RING + s, s, True)
        return carry

    @pl.when(step == 0)
    def _():
        for s in range(ahead):
            issue(idx_ref, s, s)

    n_groups = tokens // PEER_RING
    static_group(0)
    if n_groups > 1:
        lax.fori_loop(1, n_groups - 1, group, 0)
        static_group(n_groups - 1)
    v_pass(tokens - 1, (tokens - 1) % PEER_RING)


def _peer_expert(idx, gate, h, g_ffn, tab, tokens):
    n = h.shape[0]
    assert tokens % PEER_RING == 0 and n % tokens == 0
    tok = lambda w: pl.BlockSpec((tokens, w), lambda i: (i, 0))
    kern = functools.partial(_peer_expert_kernel, tokens=tokens)
    steps = n // tokens
    return pl.pallas_call(
        kern,
        grid=(steps,),
        in_specs=[pl.BlockSpec((tokens, PEER_PICKS), lambda i: (i, 0), memory_space=pltpu.SMEM),
                  pl.BlockSpec((tokens, PEER_PICKS), lambda i: (jnp.minimum(i + 1, steps - 1), 0),
                               memory_space=pltpu.SMEM),
                  tok(PEER_PICKS), tok(D_MODEL), _const_spec((1, D_MODEL)),
                  pl.BlockSpec(memory_space=pl.ANY)],
        out_specs=tok(D_MODEL),
        out_shape=jax.ShapeDtypeStruct((n, D_MODEL), F32),
        scratch_shapes=[pltpu.VMEM((PEER_PICKS * PEER_PITCH, LANES), jnp.uint32)
                        for _ in range(PEER_RING)] + [
                        pltpu.VMEM((PEER_RING, D_MODEL), F32),
                        pltpu.VMEM((PEER_RING, PEER_PICKS), F32),
                        pltpu.SemaphoreType.DMA((PEER_RING,))],
        compiler_params=_params(VMEM_DENSE_MIB),
        name="peer_expert",
    )(idx, idx, gate, h, g_ffn, tab)


def _final_kernel(h_ref, p_ref, gg_ref, wg_ref, wp_ref, gp_ref, gf_ref, y_ref):
    h = h_ref[...]
    gate = jax.nn.sigmoid(jnp.dot(_rms(h, gg_ref[...]).astype(BF16), wg_ref[...],
                                  preferred_element_type=F32))
    ple = jnp.dot(p_ref[...].astype(BF16), wp_ref[...], preferred_element_type=F32)
    h = h + _rms(ple, gp_ref[...]) * gate
    y_ref[...] = _rms(h, gf_ref[...])


def _final(h, p, g_gate, w_gate_bf, w_ple_bf, g_ple, g_final, tb):
    n = h.shape[0]
    tok = lambda w: pl.BlockSpec((tb, w), lambda i: (i, 0))
    return pl.pallas_call(
        _final_kernel,
        grid=(n // tb,),
        in_specs=[tok(D_MODEL), tok(PLE_DIM), _const_spec((1, D_MODEL)),
                  _const_spec((D_MODEL, D_MODEL)), _const_spec((PLE_DIM, D_MODEL)),
                  _const_spec((1, D_MODEL)), _const_spec((1, D_MODEL))],
        out_specs=tok(D_MODEL),
        out_shape=jax.ShapeDtypeStruct((n, D_MODEL), F32),
        compiler_params=_params(VMEM_DENSE_MIB),
        name="final",
    )(h, p, g_gate, w_gate_bf, w_ple_bf, g_ple, g_final)


def _rope_tables(pos):
    half = HEAD_DIM // 2
    inv_freq = 1.0 / (ROPE_THETA ** (jnp.arange(0, HEAD_DIM, 2, dtype=F32) / HEAD_DIM))
    ang = pos.astype(F32)[:, None] * inv_freq[None, :]
    cos, sin = jnp.cos(ang), jnp.sin(ang)
    reps = LANES // HEAD_DIM
    return (jnp.tile(jnp.concatenate([cos, cos], axis=1), (1, reps)),
            jnp.tile(jnp.concatenate([-sin, sin], axis=1), (1, reps)))


def _row(v):
    return v.reshape(1, -1).astype(F32)


def _token_stages(x, p, cos_t, sin_t, attn_fn, wm_bf, bias, lw, tb):
    q, k, v, u, vn = _in_proj(x, lw["g_mix"], lw["w_in"], cos_t, sin_t, lw["g_sgu"], lw["b_sgu"], tb)
    an = attn_fn(q, k, v)
    h = _sgu_out(u, vn, wm_bf, bias, lw["g_sgu_out"], an, lw["w_out"], x, tb)
    idx, gate = _peer_topk(h, lw["g_ffn"], lw["w_pq"], lw["sub_k1"], lw["sub_k2"], tb)
    h = _peer_expert(idx, gate, h, lw["g_ffn"], lw["uv_tab"], min(256, x.shape[0]))
    return h, k, v, vn


def kernel(x_prompt, x_sample, cache_k, cache_v, p_prompt, p_sample, g_mix, w_in, sinks, g_sgu, b_sgu,
           w_s, b_s, g_attn_out, g_sgu_out, w_out, g_ffn, w_pq, sub_k1, sub_k2, u_tab, v_tab, g_gate,
           w_gate, w_ple, g_ple, g_final):
    depth = w_in.shape[0]
    bsz, seq, _ = x_prompt.shape
    dbsz, dseq, _ = x_sample.shape
    wc = cache_k.shape[2]
    assert depth == 1, "the final kernel fuses the layer tail with the final norm"
    assert seq % WINDOW == 0 and wc == WINDOW and WINDOW % dseq == 0 and dseq <= SGU_CHUNK
    n_p, n_s = bsz * seq, dbsz * dseq
    tb_p = 256 if seq % 256 == 0 else WINDOW
    tb_s = WINDOW
    assert n_s % tb_s == 0 and seq % tb_p == 0
    nb = seq // WINDOW

    cos_p, sin_p = _rope_tables(jnp.arange(seq, dtype=jnp.int32))
    cos_s, sin_s = _rope_tables(PAST_LEN + (jnp.arange(tb_s, dtype=jnp.int32) % dseq))
    causal = jnp.tril(jnp.ones((SGU_CHUNK, SGU_CHUNK), F32))
    eye = jnp.eye(SGU_CHUNK // dseq, dtype=F32)

    hp = x_prompt.reshape(n_p, D_MODEL)
    hs = x_sample.reshape(n_s, D_MODEL)
    kp_l, vp_l, ks_l, vs_l, gs_l = [], [], [], [], []
    for i in range(depth):
        lw = dict(
            g_mix=_row(g_mix[i]), w_in=w_in[i].astype(BF16), g_sgu=_row(g_sgu[i]), b_sgu=_row(b_sgu[i]),
            g_sgu_out=_row(g_sgu_out[i]), w_out=w_out[i].astype(BF16), g_ffn=_row(g_ffn[i]),
            w_pq=w_pq[i].astype(BF16), sub_k1=sub_k1[i].astype(BF16), sub_k2=sub_k2[i].astype(BF16),
            uv_tab=_pack_uv(u_tab[i], v_tab[i]))
        sinks_i = sinks[i].astype(F32)
        g_ao = _row(g_attn_out[i])
        ws = w_s[i].astype(F32) * causal
        wm_p = ws.astype(BF16)
        bias_p = jnp.broadcast_to(b_s[i].astype(F32)[:, :, None], (SGU_HEADS, SGU_CHUNK, SGU_GROUP))
        wm_s = jnp.einsum("ab,gts->gatbs", eye, ws[:, :dseq, :dseq]).reshape(
            SGU_HEADS, SGU_CHUNK, SGU_CHUNK).astype(BF16)
        bias_s = jnp.broadcast_to(jnp.tile(b_s[i].astype(F32)[:, :dseq], (1, SGU_CHUNK // dseq))[:, :, None],
                                  (SGU_HEADS, SGU_CHUNK, SGU_GROUP))

        def attn_prompt(q, k, v):
            return _attention(sinks_i, q, k, v, g_ao, nb)

        def attn_sample(q, k, v, ck=cache_k[i], cv=cache_v[i]):
            return _attention_sample(sinks_i, q, ck.reshape(dbsz * wc, KV_COLS), cv.reshape(dbsz * wc, KV_COLS),
                                     k, v, g_ao, dseq, SGU_CHUNK // dseq)

        hp, kp, vp, _ = _token_stages(hp, None, cos_p, sin_p, attn_prompt, wm_p, bias_p, lw, tb_p)
        hs, ks, vs, vns = _token_stages(hs, None, cos_s, sin_s, attn_sample, wm_s, bias_s, lw, tb_s)
        fin = (_row(g_gate[i]), w_gate[i].astype(BF16), w_ple[i].astype(BF16), _row(g_ple[i]),
               _row(g_final))
        hp = _final(hp, p_prompt[i].reshape(n_p, PLE_DIM), *fin, tb_p)
        hs = _final(hs, p_sample[i].reshape(n_s, PLE_DIM), *fin, tb_s)

        tail = lambda a: a.reshape(bsz, seq, KV_COLS)[:, seq - WINDOW:].reshape(bsz, WINDOW, N_KV, HEAD_DIM)
        kp_l.append(tail(kp))
        vp_l.append(tail(vp))
        ks_l.append(jnp.concatenate([cache_k[i][:, dseq:], ks.reshape(dbsz, dseq, N_KV, HEAD_DIM)], axis=1))
        vs_l.append(jnp.concatenate([cache_v[i][:, dseq:], vs.reshape(dbsz, dseq, N_KV, HEAD_DIM)], axis=1))
        gs_l.append(vns.reshape(dbsz, dseq, SGU_HEADS, SGU_GROUP))

    return (hp.reshape(bsz, seq, D_MODEL), hs.reshape(dbsz, dseq, D_MODEL),
            jnp.stack(kp_l), jnp.stack(vp_l), jnp.stack(ks_l), jnp.stack(vs_l), jnp.stack(gs_l))
```

```python
import functools

import jax
import jax.numpy as jnp
from jax import lax
from jax.experimental import pallas as pl
from jax.experimental.pallas import tpu as pltpu

F32 = jnp.float32
BF16 = jnp.bfloat16

D_MODEL = 2048
PAST_LEN = 16384
HEAD_DIM = 64
N_HEADS = 16
N_KV = 2
Q_PER_KV = N_HEADS // N_KV
WINDOW = 128
ROPE_THETA = 10000.0
NEG_INF = -1e30
ATTN_WIDTH = N_HEADS * HEAD_DIM
SGU_CHUNK = 128
SGU_GROUP = 128
SGU_HEADS = 8
SGU_WIDTH = SGU_HEADS * SGU_GROUP
Q_COLS = N_HEADS * HEAD_DIM
KV_COLS = N_KV * HEAD_DIM
IN_COLS = Q_COLS + 2 * KV_COLS + 2 * SGU_WIDTH
PEER_HEADS = 8
PEER_NKEYS = 128
PEER_HALF = 128
PEER_TOPK = 16
PEER_PICKS = PEER_HEADS * PEER_TOPK
PLE_DIM = 256
EPS = 1e-6

LANES = 128
SUBLANES = 8
MIB = 1024 * 1024
VMEM_DENSE_MIB = 48
VMEM_SMALL_MIB = 32


def _rms(x, g):
    return x * lax.rsqrt(jnp.mean(x * x, axis=-1, keepdims=True) + EPS) * g


def _gelu(x):
    return 0.5 * x * (1.0 + lax.erf(x * (0.5 ** 0.5)))


def _const_spec(shape):
    nd = len(shape)
    return pl.BlockSpec(shape, lambda *_: (0,) * nd, pipeline_mode=pl.Buffered(1))


def _params(vmem_mib):
    return pltpu.CompilerParams(dimension_semantics=("arbitrary",), vmem_limit_bytes=vmem_mib * MIB)


def _in_proj_kernel(x_ref, g_ref, w_ref, cos_ref, sin_ref, gs_ref, bs_ref,
                    q_ref, k_ref, v_ref, u_ref, vn_ref):
    xn = _rms(x_ref[...], g_ref[...]).astype(BF16)
    z = jnp.dot(xn, w_ref[...], preferred_element_type=F32)
    cos = cos_ref[...]
    sin = sin_ref[...]
    lane = lax.broadcasted_iota(jnp.int32, cos.shape, 1)
    first_half = (lane % HEAD_DIM) < (HEAD_DIM // 2)

    def rope(zc):
        swapped = jnp.where(first_half,
                            pltpu.roll(zc, LANES - HEAD_DIM // 2, 1),
                            pltpu.roll(zc, HEAD_DIM // 2, 1))
        return zc * cos + swapped * sin

    for c in range(Q_COLS // LANES):
        q_ref[:, c * LANES:(c + 1) * LANES] = rope(z[:, c * LANES:(c + 1) * LANES])
    k_ref[...] = rope(z[:, Q_COLS:Q_COLS + KV_COLS])
    v_ref[...] = z[:, Q_COLS + KV_COLS:Q_COLS + 2 * KV_COLS]
    u0 = Q_COLS + 2 * KV_COLS
    u_ref[...] = _gelu(z[:, u0:u0 + SGU_WIDTH])
    sv = _gelu(z[:, u0 + SGU_WIDTH:u0 + 2 * SGU_WIDTH])
    mu = jnp.mean(sv, axis=-1, keepdims=True)
    svc = sv - mu
    var = jnp.mean(svc * svc, axis=-1, keepdims=True)
    vn_ref[...] = svc * lax.rsqrt(var + EPS) * gs_ref[...] + bs_ref[...]


def _in_proj(x, g_mix, w_in_bf, cos_t, sin_t, g_sgu, b_sgu, tb):
    n = x.shape[0]
    period_blocks = cos_t.shape[0] // tb
    tok = lambda w: pl.BlockSpec((tb, w), lambda i: (i, 0))
    tab = pl.BlockSpec((tb, LANES), lambda i: (i % period_blocks, 0))
    return pl.pallas_call(
        _in_proj_kernel,
        grid=(n // tb,),
        in_specs=[tok(D_MODEL), _const_spec((1, D_MODEL)), _const_spec((D_MODEL, IN_COLS)),
                  tab, tab, _const_spec((1, SGU_WIDTH)), _const_spec((1, SGU_WIDTH))],
        out_specs=[tok(Q_COLS), tok(KV_COLS), tok(KV_COLS), tok(SGU_WIDTH), tok(SGU_WIDTH)],
        out_shape=[jax.ShapeDtypeStruct((n, w), F32)
                   for w in (Q_COLS, KV_COLS, KV_COLS, SGU_WIDTH, SGU_WIDTH)],
        compiler_params=_params(VMEM_DENSE_MIB),
        name="in_proj",
    )(x, g_mix, w_in_bf, cos_t, sin_t, g_sgu, b_sgu)


def _attn_kernel(sinks_ref, q_ref, ka_ref, kb_ref, va_ref, vb_ref, g_ref, o_ref, acc_ref, *, blocks_per_seq):
    q = q_ref[...]
    kk_t_f32 = jnp.concatenate([ka_ref[...].T, kb_ref[...].T], axis=1)
    vv = jnp.concatenate([va_ref[...], vb_ref[...]], axis=0).astype(BF16)
    r = lax.broadcasted_iota(jnp.int32, (WINDOW, 2 * WINDOW), 0)
    c = lax.broadcasted_iota(jnp.int32, (WINDOW, 2 * WINDOW), 1)
    first = pl.program_id(0) % blocks_per_seq == 0
    valid = (c > jnp.where(first, jnp.maximum(r, WINDOW - 1), r)) & (c <= r + WINDOW)
    kk_t = kk_t_f32.astype(BF16)
    for g in range(N_KV):
        kg_t = kk_t[g * HEAD_DIM:(g + 1) * HEAD_DIM, :]
        vg = vv[:, g * HEAD_DIM:(g + 1) * HEAD_DIM]
        for hh in range(Q_PER_KV):
            h = g * Q_PER_KV + hh
            qh = q[:, h * HEAD_DIM:(h + 1) * HEAD_DIM].astype(BF16)
            s = jnp.dot(qh, kg_t, preferred_element_type=F32) * (HEAD_DIM ** -0.5)
            s = jnp.where(valid, s, NEG_INF)
            sink = sinks_ref[h]
            m = jnp.maximum(jnp.max(s, axis=1, keepdims=True), sink)
            p = jnp.exp(s - m)
            denom = jnp.sum(p, axis=1, keepdims=True) + jnp.exp(sink - m)
            p = (p / denom).astype(BF16)
            acc_ref[:, h * HEAD_DIM:(h + 1) * HEAD_DIM] = jnp.dot(p, vg, preferred_element_type=F32)
    o_ref[...] = _rms(acc_ref[...], g_ref[...]).astype(o_ref.dtype)


def _attention(sinks, q, k, v, g_attn_out, blocks_per_seq):
    n = q.shape[0]
    prev = pl.BlockSpec((WINDOW, KV_COLS), lambda j: (jnp.where(j % blocks_per_seq == 0, j, j - 1), 0))
    cur = pl.BlockSpec((WINDOW, KV_COLS), lambda j: (j, 0))
    kern = functools.partial(_attn_kernel, blocks_per_seq=blocks_per_seq)
    return pl.pallas_call(
        kern,
        grid=(n // WINDOW,),
        in_specs=[pl.BlockSpec(memory_space=pltpu.SMEM),
                  pl.BlockSpec((WINDOW, Q_COLS), lambda i: (i, 0)),
                  prev, cur, prev, cur,
                  _const_spec((1, ATTN_WIDTH))],
        out_specs=pl.BlockSpec((WINDOW, ATTN_WIDTH), lambda i: (i, 0)),
        out_shape=jax.ShapeDtypeStruct((n, ATTN_WIDTH), BF16),
        scratch_shapes=[pltpu.VMEM((WINDOW, ATTN_WIDTH), F32)],
        compiler_params=_params(VMEM_SMALL_MIB),
        name="attention",
    )(sinks, q, k, k, v, v, g_attn_out)


def _attn_sample_kernel(sinks_ref, q_ref, ck_ref, cv_ref, kn_ref, vn_ref, g_ref, o_ref, acc_ref, *, dseq, seqs):
    rows = Q_PER_KV * dseq
    r = lax.broadcasted_iota(jnp.int32, (rows, WINDOW), 0) % dseq
    c = lax.broadcasted_iota(jnp.int32, (rows, WINDOW), 1)
    valid_cache = c > r
    valid_new = c <= r
    head_of_row = lax.broadcasted_iota(jnp.int32, (rows, 1), 0) // dseq
    sink_cols = []
    for g in range(N_KV):
        col = jnp.zeros((rows, 1), F32)
        for hh in range(Q_PER_KV):
            col = jnp.where(head_of_row == hh, sinks_ref[g * Q_PER_KV + hh], col)
        sink_cols.append(col)
    pad = jnp.zeros((WINDOW - dseq, HEAD_DIM), F32)
    nt = (((1,), (1,)), ((), ()))
    for i in range(seqs):
        new = slice(i * dseq, (i + 1) * dseq)
        old = slice(i * WINDOW, (i + 1) * WINDOW)
        q = q_ref[new, :]
        for g in range(N_KV):
            cols = slice(g * HEAD_DIM, (g + 1) * HEAD_DIM)
            qg = jnp.concatenate([q[:, (g * Q_PER_KV + hh) * HEAD_DIM:(g * Q_PER_KV + hh + 1) * HEAD_DIM]
                                  for hh in range(Q_PER_KV)], axis=0).astype(BF16)
            k_new = jnp.concatenate([kn_ref[new, cols], pad], axis=0).astype(BF16)
            v_new = jnp.concatenate([vn_ref[new, cols], pad], axis=0).astype(BF16)
            s_old = lax.dot_general(qg, ck_ref[old, cols].astype(BF16), nt,
                                    preferred_element_type=F32) * (HEAD_DIM ** -0.5)
            s_new = lax.dot_general(qg, k_new, nt, preferred_element_type=F32) * (HEAD_DIM ** -0.5)
            s_old = jnp.where(valid_cache, s_old, NEG_INF)
            s_new = jnp.where(valid_new, s_new, NEG_INF)
            sink = sink_cols[g]
            m = jnp.maximum(jnp.maximum(jnp.max(s_old, axis=1, keepdims=True),
                                        jnp.max(s_new, axis=1, keepdims=True)), sink)
            p_old = jnp.exp(s_old - m)
            p_new = jnp.exp(s_new - m)
            denom = (jnp.sum(p_old, axis=1, keepdims=True) + jnp.sum(p_new, axis=1, keepdims=True)
                     + jnp.exp(sink - m))
            o = (jnp.dot((p_old / denom).astype(BF16), cv_ref[old, cols].astype(BF16),
                         preferred_element_type=F32)
                 + jnp.dot((p_new / denom).astype(BF16), v_new, preferred_element_type=F32))
            for hh in range(Q_PER_KV):
                h = g * Q_PER_KV + hh
                acc_ref[new, h * HEAD_DIM:(h + 1) * HEAD_DIM] = o[hh * dseq:(hh + 1) * dseq, :]
    o_ref[...] = _rms(acc_ref[...], g_ref[...]).astype(o_ref.dtype)


def _attention_sample(sinks, q, cache_k, cache_v, k_new, v_new, g_attn_out, dseq, seqs):
    n = q.shape[0]
    tok = lambda w: pl.BlockSpec((seqs * dseq, w), lambda i: (i, 0))
    cache = pl.BlockSpec((seqs * WINDOW, KV_COLS), lambda i: (i, 0))
    kern = functools.partial(_attn_sample_kernel, dseq=dseq, seqs=seqs)
    return pl.pallas_call(
        kern,
        grid=(n // (seqs * dseq),),
        in_specs=[pl.BlockSpec(memory_space=pltpu.SMEM), tok(Q_COLS), cache, cache, tok(KV_COLS), tok(KV_COLS),
                  _const_spec((1, ATTN_WIDTH))],
        out_specs=tok(ATTN_WIDTH),
        out_shape=jax.ShapeDtypeStruct((n, ATTN_WIDTH), BF16),
        scratch_shapes=[pltpu.VMEM((seqs * dseq, ATTN_WIDTH), F32)],
        compiler_params=_params(VMEM_SMALL_MIB),
        name="attention_sample",
    )(sinks, q, cache_k, cache_v, k_new, v_new, g_attn_out)


def _sgu_out_kernel(u_ref, vn_ref, wm_ref, bias_ref, g_ref, an_ref, wo_ref, x_ref, h_ref, s_ref):
    tb = u_ref.shape[0]
    for ch in range(tb // SGU_CHUNK):
        rows = slice(ch * SGU_CHUNK, (ch + 1) * SGU_CHUNK)
        for g in range(SGU_HEADS):
            cols = slice(g * SGU_GROUP, (g + 1) * SGU_GROUP)
            mix = jnp.dot(wm_ref[g], vn_ref[rows, cols].astype(BF16),
                          preferred_element_type=F32) + bias_ref[g]
            s_ref[rows, cols] = u_ref[rows, cols] * mix
    sn = _rms(s_ref[...], g_ref[...]).astype(BF16)
    mixed = jnp.dot(an_ref[...], wo_ref[:ATTN_WIDTH, :], preferred_element_type=F32)
    mixed = mixed + jnp.dot(sn, wo_ref[ATTN_WIDTH:, :], preferred_element_type=F32)
    h_ref[...] = x_ref[...] + mixed


def _sgu_out(u, vn, wm_bf, bias, g_sgu_out, an, w_out_bf, x, tb):
    n = x.shape[0]
    tok = lambda w: pl.BlockSpec((tb, w), lambda i: (i, 0))
    return pl.pallas_call(
        _sgu_out_kernel,
        grid=(n // tb,),
        in_specs=[tok(SGU_WIDTH), tok(SGU_WIDTH),
                  _const_spec((SGU_HEADS, SGU_CHUNK, SGU_CHUNK)),
                  _const_spec((SGU_HEADS, SGU_CHUNK, SGU_GROUP)),
                  _const_spec((1, SGU_WIDTH)), tok(ATTN_WIDTH),
                  _const_spec((D_MODEL, D_MODEL)), tok(D_MODEL)],
        out_specs=tok(D_MODEL),
        out_shape=jax.ShapeDtypeStruct((n, D_MODEL), F32),
        scratch_shapes=[pltpu.VMEM((tb, SGU_WIDTH), F32)],
        compiler_params=_params(VMEM_DENSE_MIB),
        name="sgu_out",
    )(u, vn, wm_bf, bias, g_sgu_out, an, w_out_bf, x)


def _odd_even_merge_sort_pairs(n):
    pairs = []

    def merge(lo, length, r):
        step = 2 * r
        if step < length:
            merge(lo, length, step)
            merge(lo + r, length, step)
            pairs.extend((i, i + r) for i in range(lo + r, lo + length - r, step))
        else:
            pairs.append((lo, lo + r))

    def sort(lo, length):
        if length > 1:
            half = length // 2
            sort(lo, half)
            sort(lo + half, half)
            merge(lo, length, 1)

    sort(0, n)
    return pairs


_BIG_ROW = 2 ** 30


def _top16_rows(s, tag=None, tag_bits=0):
    depth = s.shape[0] // SUBLANES
    row8 = lax.broadcasted_iota(jnp.int32, (SUBLANES, s.shape[1]), 0)
    vals = [s[d * SUBLANES:(d + 1) * SUBLANES, :] for d in range(depth)]
    pays = [(row8 + d * SUBLANES) << tag_bits for d in range(depth)]
    if tag is not None:
        pays = [p | tag[d * SUBLANES:(d + 1) * SUBLANES, :] for d, p in enumerate(pays)]
    for i, j in _odd_even_merge_sort_pairs(depth):
        a, b, pa, pb = vals[i], vals[j], pays[i], pays[j]
        swap = (b > a) | ((b == a) & (pb < pa))
        vals[i], vals[j] = jnp.where(swap, b, a), jnp.where(swap, a, b)
        pays[i], pays[j] = jnp.where(swap, pb, pa), jnp.where(swap, pa, pb)
    out_v, out_p = [], []
    for k in range(PEER_TOPK):
        head, head_p = vals[0], pays[0]
        m = jnp.max(head, axis=0, keepdims=True)
        p = jnp.min(jnp.where(head == m, head_p, _BIG_ROW), axis=0, keepdims=True)
        out_v.append(m)
        out_p.append(p)
        popped = head_p == p
        for d in range(min(depth, PEER_TOPK - 1 - k)):
            below_v, below_p = (vals[d + 1], pays[d + 1]) if d + 1 < depth else (-jnp.inf, _BIG_ROW)
            vals[d] = jnp.where(popped, below_v, vals[d])
            pays[d] = jnp.where(popped, below_p, pays[d])
    return jnp.concatenate(out_v, axis=0), jnp.concatenate(out_p, axis=0)


_CAND_WIDTHS = (16, 8, 8, 8, 4, 4, 4, 4)


def _staircase(a, b, combine):
    rows = [combine(a[i:i + 1, :], b[:w, :]) for i, w in enumerate(_CAND_WIDTHS)]
    rows.append(combine(a[len(_CAND_WIDTHS):, :], b[0:1, :]))
    return jnp.concatenate(rows, axis=0)


def _peer_topk_kernel(h_ref, g_ref, wq_ref, k1_ref, k2_ref, xn_ref, idx_ref, gate_ref):
    xn = _rms(h_ref[...], g_ref[...])
    xn_ref[...] = xn
    qp = jnp.dot(xn.astype(BF16), wq_ref[...], preferred_element_type=F32).astype(BF16)
    idx_parts, gate_parts = [], []
    nt = (((1,), (1,)), ((), ()))
    for h in range(PEER_HEADS):
        base = h * 2 * PEER_HALF
        s1 = lax.dot_general(k1_ref[h], qp[:, base:base + PEER_HALF], nt,
                             preferred_element_type=F32)
        s2 = lax.dot_general(k2_ref[h], qp[:, base + PEER_HALF:base + 2 * PEER_HALF], nt,
                             preferred_element_type=F32)
        t1, i1 = _top16_rows(s1)
        t2, i2 = _top16_rows(s2)
        cand = _staircase(t1, t2, lambda a, b: a + b)
        expert = _staircase(i1, i2, lambda a, b: a * PEER_NKEYS + b)
        expert_bits = (PEER_NKEYS * PEER_NKEYS - 1).bit_length()
        sc, code = _top16_rows(cand, expert, expert_bits)
        idx_parts.append(code & ((1 << expert_bits) - 1))
        ex = jnp.exp(sc - sc[0:1, :])
        gate_parts.append(ex / jnp.sum(ex, axis=0, keepdims=True))
    idx_t = jnp.concatenate(idx_parts, axis=0)
    gate_t = jnp.concatenate(gate_parts, axis=0)
    tb = h_ref.shape[0]
    for c in range(tb // LANES):
        idx_ref[c * LANES:(c + 1) * LANES, :] = idx_t[:, c * LANES:(c + 1) * LANES].T
        gate_ref[c * LANES:(c + 1) * LANES, :] = gate_t[:, c * LANES:(c + 1) * LANES].T


def _peer_topk(h, g_ffn, w_pq_bf, k1_bf, k2_bf, tb):
    n = h.shape[0]
    tok = lambda w: pl.BlockSpec((tb, w), lambda i: (i, 0))
    return pl.pallas_call(
        _peer_topk_kernel,
        grid=(n // tb,),
        in_specs=[tok(D_MODEL), _const_spec((1, D_MODEL)),
                  _const_spec((D_MODEL, PEER_HEADS * 2 * PEER_HALF)),
                  _const_spec((PEER_HEADS, PEER_NKEYS, PEER_HALF)),
                  _const_spec((PEER_HEADS, PEER_NKEYS, PEER_HALF))],
        out_specs=[tok(D_MODEL), tok(PEER_PICKS), tok(PEER_PICKS)],
        out_shape=[jax.ShapeDtypeStruct((n, D_MODEL), F32),
                   jax.ShapeDtypeStruct((n, PEER_PICKS), jnp.int32),
                   jax.ShapeDtypeStruct((n, PEER_PICKS), F32)],
        compiler_params=_params(VMEM_DENSE_MIB),
        name="peer_topk",
    )(h, g_ffn, w_pq_bf, k1_bf, k2_bf)


def _sgu_topk_kernel(u_ref, vn_ref, wm_ref, bias_ref, g_ref, an_ref, wo_ref, x_ref, gf_ref, wq_ref, k1_ref, k2_ref,
                     h_ref, xn_ref, idx_ref, gate_ref, s_ref, hprev_ref):
    @pl.when(pl.program_id(0) == 0)
    def _():
        hprev_ref[...] = jnp.zeros_like(hprev_ref)

    _peer_topk_kernel(hprev_ref, gf_ref, wq_ref, k1_ref, k2_ref, xn_ref, idx_ref, gate_ref)
    _sgu_out_kernel(u_ref, vn_ref, wm_ref, bias_ref, g_ref, an_ref, wo_ref, x_ref, h_ref, s_ref)
    hprev_ref[...] = h_ref[...]


def _sgu_topk(u, vn, wm_bf, bias, g_sgu_out, an, w_out_bf, x, g_ffn, w_pq_bf, k1_bf, k2_bf, tb):
    n = x.shape[0]
    steps = n // tb
    cur = lambda w: pl.BlockSpec((tb, w), lambda i: (jnp.minimum(i, steps - 1), 0))
    prev = lambda w: pl.BlockSpec((tb, w), lambda i: (jnp.maximum(i - 1, 0), 0))
    return pl.pallas_call(
        _sgu_topk_kernel,
        grid=(steps + 1,),
        in_specs=[cur(SGU_WIDTH), cur(SGU_WIDTH),
                  _const_spec((SGU_HEADS, SGU_CHUNK, SGU_CHUNK)),
                  _const_spec((SGU_HEADS, SGU_CHUNK, SGU_GROUP)),
                  _const_spec((1, SGU_WIDTH)), cur(ATTN_WIDTH),
                  _const_spec((D_MODEL, D_MODEL)), cur(D_MODEL),
                  _const_spec((1, D_MODEL)),
                  _const_spec((D_MODEL, PEER_HEADS * 2 * PEER_HALF)),
                  _const_spec((PEER_HEADS, PEER_NKEYS, PEER_HALF)),
                  _const_spec((PEER_HEADS, PEER_NKEYS, PEER_HALF))],
        out_specs=[cur(D_MODEL), prev(D_MODEL), prev(PEER_PICKS), prev(PEER_PICKS)],
        out_shape=[jax.ShapeDtypeStruct((n, D_MODEL), F32), jax.ShapeDtypeStruct((n, D_MODEL), F32),
                   jax.ShapeDtypeStruct((n, PEER_PICKS), jnp.int32),
                   jax.ShapeDtypeStruct((n, PEER_PICKS), F32)],
        scratch_shapes=[pltpu.VMEM((tb, SGU_WIDTH), F32), pltpu.VMEM((tb, D_MODEL), F32)],
        compiler_params=_params(VMEM_DENSE_MIB),
        name="sgu_topk",
    )(u, vn, wm_bf, bias, g_sgu_out, an, w_out_bf, x, g_ffn, w_pq_bf, k1_bf, k2_bf)


PEER_RING = 16
PEER_CHUNKS = D_MODEL // LANES
PEER_EGROUPS = PEER_PICKS // SUBLANES
PEER_PITCH = 20

U_MASK = 0xFFFF0000


def _pack_uv(u_tab, v_tab):
    hi = lax.bitcast_convert_type(u_tab.astype(BF16), jnp.uint16).astype(jnp.uint32) << 16
    lo = lax.bitcast_convert_type(v_tab.astype(BF16), jnp.uint16).astype(jnp.uint32)
    return (hi | lo).reshape(u_tab.shape[0], PEER_CHUNKS, LANES)


def _peer_expert_kernel(idx_ref, nidx_ref, xn_ref, gate_ref, h_ref, tab_hbm, o_ref, *scratch, tokens):
    bufs, (stage, hk_scr, sem) = scratch[:PEER_RING], scratch[PEER_RING:]

    def row_copy(slot, e, r):
        return pltpu.make_async_copy(tab_hbm.at[r], bufs[slot].at[pl.ds(e * PEER_PITCH, PEER_CHUNKS), :],
                                     sem.at[slot])

    def issue(src_idx_ref, t, slot):
        for e in range(PEER_PICKS):
            row_copy(slot, e, src_idx_ref[t, e]).start(priority=e % 2)

    def wait(slot):
        for e in range(PEER_PICKS):
            row_copy(slot, e, 0).wait()

    sub = lax.broadcasted_iota(jnp.int32, (SUBLANES, LANES), 0)
    lane_minus_sub = lax.broadcasted_iota(jnp.int32, (SUBLANES, LANES), 1) - sub

    def picks_tile(slot, g, c):
        return bufs[slot][pl.ds(g * SUBLANES * PEER_PITCH + c, SUBLANES, stride=PEER_PITCH), :]

    def u_pass(t, slot):
        row = pl.ds(t, 1)
        x_row = xn_ref[row, :]
        xb = [jnp.broadcast_to(x_row[:, c * LANES:(c + 1) * LANES], (SUBLANES, LANES))
              for c in range(PEER_CHUNKS)]
        a_diag = jnp.zeros((SUBLANES, LANES), F32)
        for g in range(PEER_EGROUPS):
            acc = None
            for c in range(PEER_CHUNKS):
                u = pltpu.bitcast(picks_tile(slot, g, c) & jnp.uint32(U_MASK), F32)
                acc = u * xb[c] if acc is None else acc + u * xb[c]
            a_g = jnp.sum(acc, axis=1, keepdims=True)
            a_diag = jnp.where(lane_minus_sub == g * SUBLANES, a_g, a_diag)
        a_row = jnp.sum(a_diag, axis=0, keepdims=True)
        hk_scr[slot:slot + 1, :] = _gelu(a_row) * gate_ref[row, :]

    def v_pass(t, slot):
        row = pl.ds(t, 1)
        hk_row = jnp.broadcast_to(hk_scr[slot:slot + 1, :], (SUBLANES, LANES))
        hk_g = [jnp.broadcast_to(
                    jnp.sum(jnp.where(lane_minus_sub == g * SUBLANES, hk_row, 0.0), axis=1, keepdims=True),
                    (SUBLANES, LANES))
                for g in range(PEER_EGROUPS)]
        for c in range(PEER_CHUNKS):
            acc = None
            for g in range(PEER_EGROUPS):
                v = pltpu.bitcast(picks_tile(slot, g, c) << 16, F32)
                acc = v * hk_g[g] if acc is None else acc + v * hk_g[g]
            stage[slot:slot + 1, c * LANES:(c + 1) * LANES] = jnp.sum(acc, axis=0, keepdims=True)
        o_ref[row, :] = h_ref[row, :] + stage[slot:slot + 1, :]

    step, last_step = pl.program_id(0), pl.num_programs(0) - 1
    ahead = PEER_RING - 2

    def phase(k, s, do_v):
        wait(s)
        j, slot = k + ahead, (s + ahead) % PEER_RING
        if isinstance(j, int) and j >= tokens:
            @pl.when(step < last_step)
            def _():
                issue(nidx_ref, j - tokens, slot)
        else:
            issue(idx_ref, j, slot)
        u_pass(k, s)
        if do_v:
            v_pass(k - 1, (s - 1) % PEER_RING)

    def static_group(g):
        for s in range(PEER_RING):
            k = g * PEER_RING + s
            phase(k, s, k >= 1)

    def group(g, carry):
        for s in range(PEER_RING):
            phase(g * PEER_RING + s, s, True)
        return carry

    @pl.when(step == 0)
    def _():
        for s in range(ahead):
            issue(idx_ref, s, s)

    n_groups = tokens // PEER_RING
    static_group(0)
    if n_groups > 1:
        lax.fori_loop(1, n_groups - 1, group, 0)
        static_group(n_groups - 1)
    v_pass(tokens - 1, (tokens - 1) % PEER_RING)


def _peer_expert(idx, xn, gate, h, tab, tokens):
    n = h.shape[0]
    assert tokens % PEER_RING == 0 and n % tokens == 0
    tok = lambda w: pl.BlockSpec((tokens, w), lambda i: (i, 0))
    kern = functools.partial(_peer_expert_kernel, tokens=tokens)
    steps = n // tokens
    return pl.pallas_call(
        kern,
        grid=(steps,),
        in_specs=[pl.BlockSpec((tokens, PEER_PICKS), lambda i: (i, 0), memory_space=pltpu.SMEM),
                  pl.BlockSpec((tokens, PEER_PICKS), lambda i: (jnp.minimum(i + 1, steps - 1), 0),
                               memory_space=pltpu.SMEM),
                  tok(D_MODEL), tok(PEER_PICKS), tok(D_MODEL),
                  pl.BlockSpec(memory_space=pl.ANY)],
        out_specs=tok(D_MODEL),
        out_shape=jax.ShapeDtypeStruct((n, D_MODEL), F32),
        scratch_shapes=[pltpu.VMEM((PEER_PICKS * PEER_PITCH, LANES), jnp.uint32)
                        for _ in range(PEER_RING)] + [
                        pltpu.VMEM((PEER_RING, D_MODEL), F32),
                        pltpu.VMEM((PEER_RING, PEER_PICKS), F32),
                        pltpu.SemaphoreType.DMA((PEER_RING,))],
        compiler_params=_params(VMEM_DENSE_MIB),
        name="peer_expert",
    )(idx, idx, xn, gate, h, tab)


def _final_kernel(h_ref, p_ref, gg_ref, wg_ref, wp_ref, gp_ref, gf_ref, y_ref):
    h = h_ref[...]
    gate = jax.nn.sigmoid(jnp.dot(_rms(h, gg_ref[...]).astype(BF16), wg_ref[...],
                                  preferred_element_type=F32))
    ple = jnp.dot(p_ref[...].astype(BF16), wp_ref[...], preferred_element_type=F32)
    h = h + _rms(ple, gp_ref[...]) * gate
    y_ref[...] = _rms(h, gf_ref[...])


def _final(h, p, g_gate, w_gate_bf, w_ple_bf, g_ple, g_final, tb):
    n = h.shape[0]
    tok = lambda w: pl.BlockSpec((tb, w), lambda i: (i, 0))
    return pl.pallas_call(
        _final_kernel,
        grid=(n // tb,),
        in_specs=[tok(D_MODEL), tok(PLE_DIM), _const_spec((1, D_MODEL)),
                  _const_spec((D_MODEL, D_MODEL)), _const_spec((PLE_DIM, D_MODEL)),
                  _const_spec((1, D_MODEL)), _const_spec((1, D_MODEL))],
        out_specs=tok(D_MODEL),
        out_shape=jax.ShapeDtypeStruct((n, D_MODEL), F32),
        compiler_params=_params(VMEM_DENSE_MIB),
        name="final",
    )(h, p, g_gate, w_gate_bf, w_ple_bf, g_ple, g_final)


def _rope_tables(pos):
    half = HEAD_DIM // 2
    inv_freq = 1.0 / (ROPE_THETA ** (jnp.arange(0, HEAD_DIM, 2, dtype=F32) / HEAD_DIM))
    ang = pos.astype(F32)[:, None] * inv_freq[None, :]
    cos, sin = jnp.cos(ang), jnp.sin(ang)
    reps = LANES // HEAD_DIM
    return (jnp.tile(jnp.concatenate([cos, cos], axis=1), (1, reps)),
            jnp.tile(jnp.concatenate([-sin, sin], axis=1), (1, reps)))


def _row(v):
    return v.reshape(1, -1).astype(F32)


def _token_stages(x, p, cos_t, sin_t, attn_fn, wm_bf, bias, lw, tb):
    q, k, v, u, vn = _in_proj(x, lw["g_mix"], lw["w_in"], cos_t, sin_t, lw["g_sgu"], lw["b_sgu"], tb)
    an = attn_fn(q, k, v)
    h, xn, idx, gate = _sgu_topk(u, vn, wm_bf, bias, lw["g_sgu_out"], an, lw["w_out"], x,
                                 lw["g_ffn"], lw["w_pq"], lw["sub_k1"], lw["sub_k2"], tb)
    h = _peer_expert(idx, xn, gate, h, lw["uv_tab"], min(256, x.shape[0]))
    return h, k, v, vn


def kernel(x_prompt, x_sample, cache_k, cache_v, p_prompt, p_sample, g_mix, w_in, sinks, g_sgu, b_sgu,
           w_s, b_s, g_attn_out, g_sgu_out, w_out, g_ffn, w_pq, sub_k1, sub_k2, u_tab, v_tab, g_gate,
           w_gate, w_ple, g_ple, g_final):
    depth = w_in.shape[0]
    bsz, seq, _ = x_prompt.shape
    dbsz, dseq, _ = x_sample.shape
    wc = cache_k.shape[2]
    assert depth == 1, "the final kernel fuses the layer tail with the final norm"
    assert seq % WINDOW == 0 and wc == WINDOW and WINDOW % dseq == 0 and dseq <= SGU_CHUNK
    n_p, n_s = bsz * seq, dbsz * dseq
    tb_p = 256 if seq % 256 == 0 else WINDOW
    tb_s = WINDOW
    assert n_s % tb_s == 0 and seq % tb_p == 0
    nb = seq // WINDOW

    cos_p, sin_p = _rope_tables(jnp.arange(seq, dtype=jnp.int32))
    cos_s, sin_s = _rope_tables(PAST_LEN + (jnp.arange(tb_s, dtype=jnp.int32) % dseq))
    causal = jnp.tril(jnp.ones((SGU_CHUNK, SGU_CHUNK), F32))
    eye = jnp.eye(SGU_CHUNK // dseq, dtype=F32)

    hp = x_prompt.reshape(n_p, D_MODEL)
    hs = x_sample.reshape(n_s, D_MODEL)
    kp_l, vp_l, ks_l, vs_l, gs_l = [], [], [], [], []
    for i in range(depth):
        lw = dict(
            g_mix=_row(g_mix[i]), w_in=w_in[i].astype(BF16), g_sgu=_row(g_sgu[i]), b_sgu=_row(b_sgu[i]),
            g_sgu_out=_row(g_sgu_out[i]), w_out=w_out[i].astype(BF16), g_ffn=_row(g_ffn[i]),
            w_pq=w_pq[i].astype(BF16), sub_k1=sub_k1[i].astype(BF16), sub_k2=sub_k2[i].astype(BF16),
            uv_tab=_pack_uv(u_tab[i], v_tab[i]))
        sinks_i = sinks[i].astype(F32)
        g_ao = _row(g_attn_out[i])
        ws = w_s[i].astype(F32) * causal
        wm_p = ws.astype(BF16)
        bias_p = jnp.broadcast_to(b_s[i].astype(F32)[:, :, None], (SGU_HEADS, SGU_CHUNK, SGU_GROUP))
        wm_s = jnp.einsum("ab,gts->gatbs", eye, ws[:, :dseq, :dseq]).reshape(
            SGU_HEADS, SGU_CHUNK, SGU_CHUNK).astype(BF16)
        bias_s = jnp.broadcast_to(jnp.tile(b_s[i].astype(F32)[:, :dseq], (1, SGU_CHUNK // dseq))[:, :, None],
                                  (SGU_HEADS, SGU_CHUNK, SGU_GROUP))

        def attn_prompt(q, k, v):
            return _attention(sinks_i, q, k, v, g_ao, nb)

        def attn_sample(q, k, v, ck=cache_k[i], cv=cache_v[i]):
            return _attention_sample(sinks_i, q, ck.reshape(dbsz * wc, KV_COLS), cv.reshape(dbsz * wc, KV_COLS),
                                     k, v, g_ao, dseq, SGU_CHUNK // dseq)

        hp, kp, vp, _ = _token_stages(hp, None, cos_p, sin_p, attn_prompt, wm_p, bias_p, lw, tb_p)
        hs, ks, vs, vns = _token_stages(hs, None, cos_s, sin_s, attn_sample, wm_s, bias_s, lw, tb_s)
        fin = (_row(g_gate[i]), w_gate[i].astype(BF16), w_ple[i].astype(BF16), _row(g_ple[i]),
               _row(g_final))
        hp = _final(hp, p_prompt[i].reshape(n_p, PLE_DIM), *fin, tb_p)
        hs = _final(hs, p_sample[i].reshape(n_s, PLE_DIM), *fin, tb_s)

        tail = lambda a: a.reshape(bsz, seq, KV_COLS)[:, seq - WINDOW:].reshape(bsz, WINDOW, N_KV, HEAD_DIM)
        kp_l.append(tail(kp))
        vp_l.append(tail(vp))
        ks_l.append(jnp.concatenate([cache_k[i][:, dseq:], ks.reshape(dbsz, dseq, N_KV, HEAD_DIM)], axis=1))
        vs_l.append(jnp.concatenate([cache_v[i][:, dseq:], vs.reshape(dbsz, dseq, N_KV, HEAD_DIM)], axis=1))
        gs_l.append(vns.reshape(dbsz, dseq, SGU_HEADS, SGU_GROUP))

    return (hp.reshape(bsz, seq, D_MODEL), hs.reshape(dbsz, dseq, D_MODEL),
            jnp.stack(kp_l), jnp.stack(vp_l), jnp.stack(ks_l), jnp.stack(vs_l), jnp.stack(gs_l))
```
